```python
import numpy as np
import jax, jax.numpy as jnp
from jax import lax

D_MODEL = 1024
BATCH = 8
SEQ = 4096
DEPTH = 1

HEAD_DIM = 64
ROT_DIM = HEAD_DIM // 4
ROPE_THETA = 500000.0
NORM_EPS = 1e-6

NSA_HEADS = 8
NSA_KV_GROUPS = 2
NSA_GROUP_SIZE = NSA_HEADS // NSA_KV_GROUPS
CMP_BLOCK = 32
CMP_STRIDE = 16
SEL_BLOCK = 64
SEL_TOPN = 16
WINDOW = 512
NSA_Q_CHUNK = 32

MOBA_HEADS = 8
MOBA_BLOCK = 256
MOBA_TOPK = 3
MOBA_Q_CHUNK = 16

D_FF = 4 * D_MODEL

NSA_WIDTH = NSA_HEADS * HEAD_DIM
KV_WIDTH = NSA_KV_GROUPS * HEAD_DIM
MOBA_WIDTH = MOBA_HEADS * HEAD_DIM
OFF_KV = NSA_WIDTH
OFF_GN = OFF_KV + 6 * KV_WIDTH
OFF_M = OFF_GN + 3 * NSA_HEADS
OFF_GM = OFF_M + 3 * MOBA_WIDTH
IN_WIDTH = OFF_GM + 2 * D_MODEL

kernel_name = "hybrid_nsa_moba_gated_block"


def rmsnorm(x, g):
    xf = x.astype(jnp.float32)
    y = xf * lax.rsqrt(jnp.mean(xf * xf, axis=-1, keepdims=True) + NORM_EPS)
    return (y * g.astype(jnp.float32)).astype(x.dtype)


def rope_tables(seq):
    inv = ROPE_THETA ** (-jnp.arange(0, ROT_DIM, 2, dtype=jnp.float32) / ROT_DIM)
    ang = jnp.arange(seq, dtype=jnp.float32)[:, None] * inv[None, :]
    return jnp.cos(ang), jnp.sin(ang)


def partial_rope(x, cos, sin):
    half = ROT_DIM // 2
    x1 = x[..., :half].astype(jnp.float32)
    x2 = x[..., half:ROT_DIM].astype(jnp.float32)
    c = cos[:, None, :]
    s = sin[:, None, :]
    rot = jnp.concatenate([x1 * c - x2 * s, x1 * s + x2 * c], axis=-1).astype(x.dtype)
    return jnp.concatenate([rot, x[..., ROT_DIM:]], axis=-1)


def masked_softmax(s, mask):
    s = jnp.where(mask, s.astype(jnp.float32), -jnp.inf)
    m = jnp.max(s, axis=-1, keepdims=True)
    m = jnp.where(jnp.isfinite(m), m, 0.0)
    p = jnp.where(mask, jnp.exp(s - m), 0.0)
    return p / jnp.maximum(jnp.sum(p, axis=-1, keepdims=True), 1e-30)


def compress_blocks(k, pe, w1, w2):
    b, seq, g, d = k.shape
    n_cmp = (seq - CMP_BLOCK) // CMP_STRIDE + 1
    idx = jnp.arange(n_cmp)[:, None] * CMP_STRIDE + jnp.arange(CMP_BLOCK)[None, :]
    kb = k[:, idx] + pe[None, None, :, None, :]
    kb = jnp.transpose(kb, (0, 3, 1, 2, 4)).reshape(b, g, n_cmp, CMP_BLOCK * d)
    return jax.nn.gelu(kb @ w1) @ w2


def cmp_to_sel_overlap(n_cmp, n_blk):
    i = np.arange(n_cmp)[:, None]
    j = np.arange(n_blk)[None, :]
    start = i * CMP_STRIDE
    end = start + CMP_BLOCK - 1
    return ((end >= j * SEL_BLOCK) & (start <= j * SEL_BLOCK + SEL_BLOCK - 1)).astype(np.float32)


def nsa_attention(q, q_rot, kc, vc, ks, vs, kw, vw, gates):
    b, g, r, seq, d = q.shape
    n_cmp = kc.shape[2]
    n_blk = seq // SEL_BLOCK
    n_top = min(SEL_TOPN, n_blk)
    scale = d ** -0.5
    qc_n = NSA_Q_CHUNK
    overlap = jnp.asarray(cmp_to_sel_overlap(n_cmp, n_blk))
    ks_blk = ks.reshape(b, g, n_blk, SEL_BLOCK, d)
    vs_blk = vs.reshape(b, g, n_blk, SEL_BLOCK, d)
    kw_pad = jnp.pad(kw, ((0, 0), (0, 0), (WINDOW, 0), (0, 0)))
    vw_pad = jnp.pad(vw, ((0, 0), (0, 0), (WINDOW, 0), (0, 0)))
    cmp_end = jnp.arange(n_cmp) * CMP_STRIDE + CMP_BLOCK - 1
    blk_ids = jnp.arange(n_blk)
    bi = jnp.arange(b)[:, None, None, None]
    gi = jnp.arange(g)[None, :, None, None]

    def chunk(c):
        t0 = c * qc_n
        t = t0 + jnp.arange(qc_n)
        qq = lax.dynamic_slice_in_dim(q, t0, qc_n, axis=3)
        qr = lax.dynamic_slice_in_dim(q_rot, t0, qc_n, axis=3)
        gc = lax.dynamic_slice_in_dim(gates, t0, qc_n, axis=3)
        s_cmp = jnp.einsum('bgrqd,bgnd->bgrqn', qq, kc) * scale
        p_cmp = masked_softmax(s_cmp, cmp_end[None, :] <= t[:, None])
        o_cmp = jnp.einsum('bgrqn,bgnd->bgrqd', p_cmp.astype(vc.dtype), vc)
        imp = jnp.einsum('bgrqn,nj->bgqj', p_cmp, overlap)
        cur = t // SEL_BLOCK
        jj = blk_ids[None, :]
        forced = (jj == 0) | (jj == cur[:, None]) | (jj == cur[:, None] - 1)
        imp = jnp.where(forced, jnp.inf, imp)
        imp = jnp.where(jj <= cur[:, None], imp, -jnp.inf)
        _, top_i = lax.top_k(imp, n_top)
        ksel = ks_blk[bi, gi, top_i]
        vsel = vs_blk[bi, gi, top_i]
        s_sel = jnp.einsum('bgrqd,bgqnld->bgrqnl', qr, ksel) * scale
        kpos = top_i[..., None] * SEL_BLOCK + jnp.arange(SEL_BLOCK)
        m_sel = kpos <= t[:, None, None]
        p_sel = masked_softmax(s_sel.reshape(b, g, r, qc_n, -1),
                               m_sel[:, :, None].reshape(b, g, 1, qc_n, -1))
        o_sel = jnp.einsum('bgrqm,bgqmd->bgrqd', p_sel.astype(vs.dtype),
                           vsel.reshape(b, g, qc_n, -1, d))
        kwin = lax.dynamic_slice_in_dim(kw_pad, t0, WINDOW + qc_n, axis=2)
        vwin = lax.dynamic_slice_in_dim(vw_pad, t0, WINDOW + qc_n, axis=2)
        wpos = t0 - WINDOW + jnp.arange(WINDOW + qc_n)
        m_win = ((wpos[None, :] <= t[:, None]) & (wpos[None, :] > t[:, None] - WINDOW)
                 & (wpos[None, :] >= 0))
        s_win = jnp.einsum('bgrqd,bgkd->bgrqk', qr, kwin) * scale
        p_win = masked_softmax(s_win, m_win)
        o_win = jnp.einsum('bgrqk,bgkd->bgrqd', p_win.astype(vw.dtype), vwin)
        return gc[..., 0:1] * o_cmp + gc[..., 1:2] * o_sel + gc[..., 2:3] * o_win

    out = lax.map(chunk, jnp.arange(seq // qc_n))
    return jnp.transpose(out, (1, 0, 4, 2, 3, 5)).reshape(b, seq, g * r * d)


def moba_attention(q, k, v):
    b, h, seq, d = q.shape
    n_blk = -(-seq // MOBA_BLOCK)
    pad = n_blk * MOBA_BLOCK - seq
    scale = d ** -0.5
    qc_n = MOBA_Q_CHUNK
    kp = jnp.pad(k, ((0, 0), (0, 0), (0, pad), (0, 0)))
    vp = jnp.pad(v, ((0, 0), (0, 0), (0, pad), (0, 0)))
    k_blk = kp.reshape(b, h, n_blk, MOBA_BLOCK, d)
    v_blk = vp.reshape(b, h, n_blk, MOBA_BLOCK, d)
    k_mean = jnp.mean(k_blk.astype(jnp.float32), axis=3).astype(k.dtype)
    n_top = min(MOBA_TOPK, n_blk)
    blk_ids = jnp.arange(n_blk)
    bi = jnp.arange(b)[:, None, None, None]
    hi = jnp.arange(h)[None, :, None, None]

    def chunk(c):
        t0 = c * qc_n
        t = t0 + jnp.arange(qc_n)
        qq = lax.dynamic_slice_in_dim(q, t0, qc_n, axis=2)
        own = t0 // MOBA_BLOCK
        gate = jnp.einsum('bhqd,bhnd->bhqn', qq, k_mean).astype(jnp.float32)
        gate = jnp.where(blk_ids < own, gate, -jnp.inf)
        top_s, top_i = lax.top_k(gate, n_top)
        valid = top_s > -jnp.inf
        kg = k_blk[bi, hi, top_i]
        vg = v_blk[bi, hi, top_i]
        s_past = jnp.einsum('bhqd,bhqkld->bhqkl', qq, kg).reshape(b, h, qc_n, -1) * scale
        m_past = jnp.broadcast_to(valid[..., None], (b, h, qc_n, n_top, MOBA_BLOCK)).reshape(b, h, qc_n, -1)
        k_own = lax.dynamic_slice_in_dim(kp, own * MOBA_BLOCK, MOBA_BLOCK, axis=2)
        v_own = lax.dynamic_slice_in_dim(vp, own * MOBA_BLOCK, MOBA_BLOCK, axis=2)
        s_own = jnp.einsum('bhqd,bhld->bhql', qq, k_own) * scale
        m_own = (own * MOBA_BLOCK + jnp.arange(MOBA_BLOCK))[None, :] <= t[:, None]
        s = jnp.concatenate([s_past, s_own], axis=-1)
        m = jnp.concatenate([m_past, jnp.broadcast_to(m_own, (b, h, qc_n, MOBA_BLOCK))], axis=-1)
        p = masked_softmax(s, m).astype(v.dtype)
        n_past = n_top * MOBA_BLOCK
        return (jnp.einsum('bhqm,bhqmd->bhqd', p[..., :n_past], vg.reshape(b, h, qc_n, -1, d))
                + jnp.einsum('bhql,bhld->bhqd', p[..., n_past:], v_own))

    out = lax.map(chunk, jnp.arange(seq // qc_n))
    return jnp.transpose(out, (1, 0, 3, 2, 4)).reshape(b, seq, h * d)


def setup_inputs(seed: int = 0) -> dict:
    key = jax.random.key(seed)
    ks = jax.random.split(key, 17)
    f32 = jnp.float32

    def nrm(k, shape, fan_in):
        return jax.random.normal(k, shape, f32) * (fan_in ** -0.5)

    def gain(k, shape):
        return 1.0 + 0.01 * jax.random.normal(k, shape, f32)

    L = DEPTH
    return {
        "x": jax.random.normal(ks[0], (BATCH, SEQ, D_MODEL), f32),
        "norm1_g": gain(ks[1], (L, D_MODEL)),
        "w_in": nrm(ks[2], (L, D_MODEL, IN_WIDTH), D_MODEL),
        "cmp_pe_k": 0.02 * jax.random.normal(ks[3], (L, CMP_BLOCK, HEAD_DIM), f32),
        "cmp_pe_v": 0.02 * jax.random.normal(ks[4], (L, CMP_BLOCK, HEAD_DIM), f32),
        "cmp_k_w1": nrm(ks[5], (L, CMP_BLOCK * HEAD_DIM, HEAD_DIM), CMP_BLOCK * HEAD_DIM),
        "cmp_k_w2": nrm(ks[6], (L, HEAD_DIM, HEAD_DIM), HEAD_DIM),
        "cmp_v_w1": nrm(ks[7], (L, CMP_BLOCK * HEAD_DIM, HEAD_DIM), CMP_BLOCK * HEAD_DIM),
        "cmp_v_w2": nrm(ks[8], (L, HEAD_DIM, HEAD_DIM), HEAD_DIM),
        "w_up_nsa": nrm(ks[9], (L, NSA_WIDTH, D_MODEL), NSA_WIDTH),
        "w_up_moba": nrm(ks[10], (L, MOBA_WIDTH, D_MODEL), MOBA_WIDTH),
        "w_out": nrm(ks[11], (L, D_MODEL, D_MODEL), D_MODEL),
        "norm2_g": gain(ks[12], (L, D_MODEL)),
        "w_ff1": nrm(ks[13], (L, D_MODEL, D_FF), D_MODEL),
        "w_ff2": nrm(ks[14], (L, D_FF, D_MODEL), D_FF),
        "norm_f_g": gain(ks[15], (D_MODEL,)),
    }


def reference(x, norm1_g, w_in, cmp_pe_k, cmp_pe_v, cmp_k_w1, cmp_k_w2, cmp_v_w1, cmp_v_w2,
              w_up_nsa, w_up_moba, w_out, norm2_g, w_ff1, w_ff2, norm_f_g):
    b, seq, _ = x.shape
    cos, sin = rope_tables(seq)
    G, R, d = NSA_KV_GROUPS, NSA_GROUP_SIZE, HEAD_DIM
    for layer in range(DEPTH):
        h = rmsnorm(x, norm1_g[layer])
        proj = h @ w_in[layer]
        q_n = proj[..., :OFF_KV]
        kv_n = proj[..., OFF_KV:OFF_GN]
        g_n = proj[..., OFF_GN:OFF_M]
        qkv_m = proj[..., OFF_M:OFF_GM]
        g_m = proj[..., OFF_GM:]

        q = q_n.reshape(b, seq, NSA_HEADS, d)
        q_r = partial_rope(q, cos, sin)
        kv = kv_n.reshape(b, seq, 6, G, d)
        k_c, v_c, k_s, v_s, k_w, v_w = (kv[:, :, i] for i in range(6))
        k_s = partial_rope(k_s, cos, sin)
        k_w = partial_rope(k_w, cos, sin)
        kc = compress_blocks(k_c, cmp_pe_k[layer], cmp_k_w1[layer], cmp_k_w2[layer])
        vc = compress_blocks(v_c, cmp_pe_v[layer], cmp_v_w1[layer], cmp_v_w2[layer])
        to_q = lambda t: jnp.transpose(t.reshape(b, seq, G, R, d), (0, 2, 3, 1, 4))
        to_kv = lambda t: jnp.transpose(t, (0, 2, 1, 3))
        gates = jnp.transpose(
            jax.nn.sigmoid(g_n.astype(jnp.float32)).reshape(b, seq, G, R, 3),
            (0, 2, 3, 1, 4)).astype(x.dtype)
        y_nsa = nsa_attention(to_q(q), to_q(q_r), kc, vc, to_kv(k_s), to_kv(v_s),
                              to_kv(k_w), to_kv(v_w), gates)

        qkv = qkv_m.reshape(b, seq, 3, MOBA_HEADS, d)
        qm = partial_rope(qkv[:, :, 0], cos, sin)
        km = partial_rope(qkv[:, :, 1], cos, sin)
        vm = qkv[:, :, 2]
        y_moba = moba_attention(to_kv(qm), to_kv(km), to_kv(vm))

        gm = jax.nn.sigmoid(g_m.astype(jnp.float32)).astype(x.dtype)
        g_nsa = gm[..., :D_MODEL]
        g_moba = gm[..., D_MODEL:]
        mixed = g_nsa * (y_nsa @ w_up_nsa[layer]) + g_moba * (y_moba @ w_up_moba[layer])
        x = x + mixed @ w_out[layer]

        h2 = rmsnorm(x, norm2_g[layer])
        u = jax.nn.relu(h2 @ w_ff1[layer])
        x = x + (u * u) @ w_ff2[layer]
    return rmsnorm(x, norm_f_g)
```

```python
import functools

import numpy as np
import jax
import jax.numpy as jnp
from jax import lax
from jax.experimental import pallas as pl
from jax.experimental.pallas import tpu as pltpu

F32 = jnp.float32
BF16 = jnp.bfloat16

D_MODEL = 1024
HEAD_DIM = 64
ROT_DIM = HEAD_DIM // 4
ROPE_THETA = 500000.0
NORM_EPS = 1e-6

NSA_HEADS = 8
NSA_KV_GROUPS = 2
NSA_GROUP_SIZE = NSA_HEADS // NSA_KV_GROUPS
CMP_BLOCK = 32
CMP_STRIDE = 16
SEL_BLOCK = 64
SEL_TOPN = 16
WINDOW = 512

MOBA_HEADS = 8
MOBA_BLOCK = 256
MOBA_TOPK = 3

D_FF = 4 * D_MODEL
NSA_WIDTH = NSA_HEADS * HEAD_DIM
KV_WIDTH = NSA_KV_GROUPS * HEAD_DIM
MOBA_WIDTH = MOBA_HEADS * HEAD_DIM
OFF_KV = NSA_WIDTH
OFF_GN = OFF_KV + 6 * KV_WIDTH
OFF_M = OFF_GN + 3 * NSA_HEADS
OFF_GM = OFF_M + 3 * MOBA_WIDTH
IN_WIDTH = OFF_GM + 2 * D_MODEL

LANES = 128
KEY_BLOCK = 256
Q_TILE = 256
V_ROWS = 80
NEG = -1e30
SCALE = HEAD_DIM ** -0.5
VMEM_LIMIT = 56 * 1024 * 1024

GN_PAD = LANES
SEC_Q = 0
SEC_KV = SEC_Q + NSA_WIDTH
SEC_GN = SEC_KV + 6 * KV_WIDTH
SEC_M = SEC_GN + GN_PAD
SEC_GM = SEC_M + 3 * MOBA_WIDTH
W_ALL = SEC_GM + 2 * D_MODEL


def _dot(a, b):
    return jnp.dot(a, b, preferred_element_type=F32)


def _sigmoid(y):
    return 1.0 / (1.0 + jnp.exp(-y))


def _in_proj_body(x_ref, g_ref, w_ref, c_ref, s1_ref, s2_ref,
                  oq_ref, okv_ref, ogn_ref, om_ref, ogm_ref, h_ref):
    x = x_ref[...]
    ms = jnp.mean(x * x, axis=-1, keepdims=True)
    h_ref[...] = (x * lax.rsqrt(ms + NORM_EPS) * g_ref[...]).astype(BF16)
    cos = c_ref[...]
    sin_hi = s1_ref[...]
    sin_lo = s2_ref[...]

    def proj(c0, width):
        return _dot(h_ref[...], w_ref[:, c0:c0 + width])

    def rope(yc):
        return (yc * cos + pltpu.roll(yc, LANES - ROT_DIM // 2, 1) * sin_hi
                + pltpu.roll(yc, ROT_DIM // 2, 1) * sin_lo)

    y = proj(SEC_Q, NSA_WIDTH)
    oq_ref[:, 0:NSA_WIDTH] = y
    for j in range(NSA_WIDTH // LANES):
        oq_ref[:, NSA_WIDTH + j * LANES:NSA_WIDTH + (j + 1) * LANES] = rope(y[:, j * LANES:(j + 1) * LANES])

    y = proj(SEC_KV, 6 * KV_WIDTH)
    for j in range(6):
        yc = y[:, j * LANES:(j + 1) * LANES]
        okv_ref[:, j * LANES:(j + 1) * LANES] = rope(yc) if j in (2, 4) else yc

    ogn_ref[...] = _sigmoid(proj(SEC_GN, GN_PAD))

    for part in range(3):
        y = proj(SEC_M + part * MOBA_WIDTH, MOBA_WIDTH)
        for j in range(MOBA_WIDTH // LANES):
            yc = y[:, j * LANES:(j + 1) * LANES]
            om_ref[:, part * MOBA_WIDTH + j * LANES:part * MOBA_WIDTH + (j + 1) * LANES] = (
                rope(yc) if part < 2 else yc)

    for part in range(4):
        w = 2 * D_MODEL // 4
        ogm_ref[:, part * w:(part + 1) * w] = _sigmoid(proj(SEC_GM + part * w, w))


def _in_proj(x2, g1, w_all, cos_t, sin_hi_t, sin_lo_t, seq):
    m = x2.shape[0]
    tm = 256
    nt_seq = seq // tm
    row = lambda i: (i, 0)
    const = lambda i: (0, 0)
    tab = lambda i: (i % nt_seq, 0)
    out_w = (2 * NSA_WIDTH, 6 * KV_WIDTH, GN_PAD, 3 * MOBA_WIDTH, 2 * D_MODEL)
    return pl.pallas_call(
        _in_proj_body,
        grid=(m // tm,),
        in_specs=[pl.BlockSpec((tm, D_MODEL), row),
                  pl.BlockSpec((1, D_MODEL), const),
                  pl.BlockSpec((D_MODEL, W_ALL), const),
                  pl.BlockSpec((tm, LANES), tab),
                  pl.BlockSpec((tm, LANES), tab),
                  pl.BlockSpec((tm, LANES), tab)],
        out_specs=[pl.BlockSpec((tm, w), row) for w in out_w],
        out_shape=[jax.ShapeDtypeStruct((m, w), F32) for w in out_w],
        scratch_shapes=[pltpu.VMEM((tm, D_MODEL), BF16)],
        compiler_params=pltpu.CompilerParams(dimension_semantics=("parallel",),
                                             vmem_limit_bytes=VMEM_LIMIT),
        name="in_proj",
    )(x2, g1, w_all, cos_t, sin_hi_t, sin_lo_t)


def _gelu_tanh(x):
    return 0.5 * x * (1.0 + jnp.tanh(np.sqrt(2.0 / np.pi) * (x + 0.044715 * (x * x * x))))


def _compress_body(kx_ref, vx_ref, pek_ref, pev_ref, kw1_ref, vw1_ref, kw2_ref, vw2t_ref,
                   kc_ref, vct_ref):
    half = CMP_STRIDE * HEAD_DIM
    ncp = kx_ref.shape[0]

    def hidden(x_ref, pe_ref, w1_ref):
        xb = x_ref[...].astype(BF16)
        ya = _dot(xb, w1_ref[0:half, :])
        yb = _dot(xb, w1_ref[half:2 * half, :])
        pe = _dot(pe_ref[...].astype(BF16), w1_ref[...])[0:1, :]
        return _gelu_tanh(ya + pltpu.roll(yb, ncp - 1, 0) + pe)

    hk = hidden(kx_ref, pek_ref, kw1_ref)
    kc_ref[...] = _dot(hk.astype(BF16), kw2_ref[...])
    hv = hidden(vx_ref, pev_ref, vw1_ref)
    vct_ref[...] = lax.dot_general(vw2t_ref[...], hv.astype(BF16), (((1,), (1,)), ((), ())),
                                   preferred_element_type=F32)


def _compress(kx, vx, pek, pev, kw1, vw1, kw2, vw2t):
    b, g, ncp, wide = kx.shape
    blk = lambda bi, gi: (bi, gi, 0, 0)
    const = lambda bi, gi: (0, 0)
    return pl.pallas_call(
        _compress_body,
        grid=(b, g),
        in_specs=[pl.BlockSpec((None, None, ncp, wide), blk),
                  pl.BlockSpec((None, None, ncp, wide), blk),
                  pl.BlockSpec((8, CMP_BLOCK * HEAD_DIM), const),
                  pl.BlockSpec((8, CMP_BLOCK * HEAD_DIM), const),
                  pl.BlockSpec((CMP_BLOCK * HEAD_DIM, HEAD_DIM), const),
                  pl.BlockSpec((CMP_BLOCK * HEAD_DIM, HEAD_DIM), const),
                  pl.BlockSpec((HEAD_DIM, HEAD_DIM), const),
                  pl.BlockSpec((HEAD_DIM, HEAD_DIM), const)],
        out_specs=[pl.BlockSpec((None, None, ncp, HEAD_DIM), blk),
                   pl.BlockSpec((None, None, HEAD_DIM, ncp), blk)],
        out_shape=[jax.ShapeDtypeStruct((b, g, ncp, HEAD_DIM), F32),
                   jax.ShapeDtypeStruct((b, g, HEAD_DIM, ncp), F32)],
        compiler_params=pltpu.CompilerParams(dimension_semantics=("parallel", "parallel")),
        name="compress",
    )(kx, vx, pek, pev, kw1, vw1, kw2, vw2t)


def _kmean_body(k_ref, o_ref):
    o_ref[...] = jnp.mean(k_ref[...], axis=0, keepdims=True)


def _kmean(om3):
    b, seq, _ = om3.shape
    nb = seq // MOBA_BLOCK
    return pl.pallas_call(
        _kmean_body,
        grid=(b, nb),
        in_specs=[pl.BlockSpec((None, MOBA_BLOCK, MOBA_WIDTH), lambda bi, j: (bi, j, 1))],
        out_specs=pl.BlockSpec((None, None, 1, MOBA_WIDTH), lambda bi, j: (bi, j, 0, 0)),
        out_shape=jax.ShapeDtypeStruct((b, nb, 1, MOBA_WIDTH), F32),
        compiler_params=pltpu.CompilerParams(dimension_semantics=("parallel", "parallel")),
        name="kmean",
    )(om3)


def _topk_mask(score, n_rows, k_top):
    rows = score.shape[0]
    j = lax.broadcasted_iota(jnp.int32, (rows, 1), 0)
    rank = jnp.zeros(score.shape, F32)
    for k in range(n_rows):
        rk = score[k:k + 1, :]
        tie_ahead = jnp.where(j > k, 1.0, 0.0)
        rank = rank + jnp.where(rk > score, 1.0, jnp.where(rk == score, tie_ahead, 0.0))
    return rank < k_top


def _finish(acc):
    return acc[0:HEAD_DIM, :] / acc[HEAD_DIM:HEAD_DIM + 1, :]


def _nsa_body(q_ref, qr_ref, gn_ref, kc_ref, vct_ref, ke_ref, vts_ref, kw_ref, vtw_ref, ovt_ref,
              y_ref, *, n_sel_blocks):
    tq = q_ref.shape[1]
    rg = NSA_GROUP_SIZE
    i = pl.program_id(2)
    t0 = i * tq
    heads = lambda a: jnp.concatenate([a[r * HEAD_DIM:(r + 1) * HEAD_DIM, :] for r in range(rg)], axis=1)
    q = q_ref[...] * SCALE
    qr = qr_ref[...] * SCALE
    qc = heads(q).astype(BF16)
    qw = heads(qr).astype(BF16)
    tpos = t0 + lax.broadcasted_iota(jnp.int32, (1, tq), 1)
    tpos4 = jnp.concatenate([tpos] * rg, axis=1)

    ncp = kc_ref.shape[0]
    s = _dot(kc_ref[...].astype(BF16), qc)
    n = lax.broadcasted_iota(jnp.int32, (ncp, 1), 0)
    cmask = (n * CMP_STRIDE + (CMP_BLOCK - 1)) <= tpos4
    s = jnp.where(cmask, s, NEG)
    m = jnp.max(s, axis=0, keepdims=True)
    p = jnp.where(cmask, jnp.exp(s - m), 0.0)
    pn = p / jnp.maximum(jnp.sum(p, axis=0, keepdims=True), 1e-30)
    o_cmp = _dot(vct_ref[...].astype(BF16), pn.astype(BF16))

    ps = pn[:, 0:tq]
    for r in range(1, rg):
        ps = ps + pn[:, r * tq:(r + 1) * tq]
    ps_hi = ps.astype(BF16)
    ps_lo = (ps - ps_hi.astype(F32)).astype(BF16)
    imp = _dot(ovt_ref[...], ps_hi) + _dot(ovt_ref[...], ps_lo)
    jb = lax.broadcasted_iota(jnp.int32, (imp.shape[0], 1), 0)
    cur = tpos >> (SEL_BLOCK.bit_length() - 1)
    forced = (jb == 0) | (jb == cur) | (jb == cur - 1)
    imp = jnp.where(forced, jnp.inf, imp)
    imp = jnp.where(jb <= cur, imp, -jnp.inf)
    sel = _topk_mask(imp, n_sel_blocks, min(SEL_TOPN, n_sel_blocks))
    bias = jnp.where(sel, 0.0, NEG).astype(BF16)

    qrb = qr.astype(BF16)
    qa = jnp.concatenate(
        [jnp.concatenate([qrb[r * HEAD_DIM:(r + 1) * HEAD_DIM, :], bias], axis=0) for r in range(rg)],
        axis=1)
    kpos = t0 + lax.broadcasted_iota(jnp.int32, (KEY_BLOCK, 1), 0)
    causal = kpos <= tpos4
    s = jnp.where(causal, _dot(ke_ref[i], qa), NEG)
    m0 = jnp.max(s, axis=0, keepdims=True)
    acc0 = _dot(vts_ref[i], jnp.exp(s - m0).astype(BF16))

    def sel_step(kb, carry):
        m_old, acc = carry
        sb = _dot(ke_ref[kb], qa)
        m_new = jnp.maximum(m_old, jnp.max(sb, axis=0, keepdims=True))
        alpha = jnp.exp(m_old - m_new)
        pb = jnp.exp(sb - m_new).astype(BF16)
        return m_new, alpha * acc + _dot(vts_ref[kb], pb)

    _, acc = lax.fori_loop(0, i, sel_step, (m0, acc0))
    o_sel = _finish(acc)

    i1 = jnp.maximum(i - 1, 0)
    i2 = jnp.maximum(i - 2, 0)
    s0 = jnp.where(causal, _dot(kw_ref[i], qw), NEG)
    s1 = jnp.where(i >= 1, _dot(kw_ref[i1], qw), NEG)
    kpos2 = i2 * KEY_BLOCK + lax.broadcasted_iota(jnp.int32, (KEY_BLOCK, 1), 0)
    s2 = jnp.where((i >= 2) & (kpos2 > tpos4 - WINDOW), _dot(kw_ref[i2], qw), NEG)
    mw = jnp.maximum(jnp.maximum(jnp.max(s0, axis=0, keepdims=True), jnp.max(s1, axis=0, keepdims=True)),
                     jnp.max(s2, axis=0, keepdims=True))
    accw = (_dot(vtw_ref[i], jnp.exp(s0 - mw).astype(BF16))
            + _dot(vtw_ref[i1], jnp.exp(s1 - mw).astype(BF16))
            + _dot(vtw_ref[i2], jnp.exp(s2 - mw).astype(BF16)))
    o_win = _finish(accw)

    gates = gn_ref[...]
    for r in range(rg):
        c = slice(r * tq, (r + 1) * tq)
        y_ref[r * HEAD_DIM:(r + 1) * HEAD_DIM, :] = (
            gates[3 * r:3 * r + 1, :] * o_cmp[:, c]
            + gates[3 * r + 1:3 * r + 2, :] * o_sel[:, c]
            + gates[3 * r + 2:3 * r + 3, :] * o_win[:, c])


def _nsa(qt, gnt, kc, vct, ke, vts, kw, vtw, ovt):
    b, _, seq = qt.shape
    g = NSA_KV_GROUPS
    nkb = seq // KEY_BLOCK
    ncp = kc.shape[2]
    tq = Q_TILE
    gw = NSA_GROUP_SIZE * HEAD_DIM
    full5 = lambda bi, gi, i: (bi, gi, 0, 0, 0)
    full4 = lambda bi, gi, i: (bi, gi, 0, 0)
    return pl.pallas_call(
        functools.partial(_nsa_body, n_sel_blocks=seq // SEL_BLOCK),
        grid=(b, g, seq // tq),
        in_specs=[pl.BlockSpec((None, gw, tq), lambda bi, gi, i: (bi, gi, i)),
                  pl.BlockSpec((None, gw, tq), lambda bi, gi, i: (bi, g + gi, i)),
                  pl.BlockSpec((None, None, 16, tq), lambda bi, gi, i: (bi, gi, 0, i)),
                  pl.BlockSpec((None, None, ncp, HEAD_DIM), full4),
                  pl.BlockSpec((None, None, HEAD_DIM, ncp), full4),
                  pl.BlockSpec((None, None, nkb, KEY_BLOCK, LANES), full5),
                  pl.BlockSpec((None, None, nkb, V_ROWS, KEY_BLOCK), full5),
                  pl.BlockSpec((None, None, nkb, KEY_BLOCK, HEAD_DIM), full5),
                  pl.BlockSpec((None, None, nkb, V_ROWS, KEY_BLOCK), full5),
                  pl.BlockSpec((HEAD_DIM, ncp), lambda bi, gi, i: (0, 0))],
        out_specs=pl.BlockSpec((None, gw, tq), lambda bi, gi, i: (bi, gi, i)),
        out_shape=jax.ShapeDtypeStruct((b, NSA_WIDTH, seq), F32),
        compiler_params=pltpu.CompilerParams(
            dimension_semantics=("parallel", "parallel", "arbitrary"), vmem_limit_bytes=VMEM_LIMIT),
        name="nsa",
    )(qt, qt, gnt, kc, vct, ke, vts, kw, vtw, ovt)


def _moba_body(q_ref, km_ref, ke_ref, vt_ref, y_ref, *, n_blocks):
    tq = q_ref.shape[1]
    i = pl.program_id(2)
    t0 = i * tq
    q = q_ref[...] * SCALE
    qb = q.astype(BF16)
    q_lo = (q - qb.astype(F32)).astype(BF16)
    km = km_ref[...]
    km_hi = km.astype(BF16)
    km_lo = (km - km_hi.astype(F32)).astype(BF16)
    gate = _dot(km_hi, qb) + (_dot(km_hi, q_lo) + _dot(km_lo, qb))
    jb = lax.broadcasted_iota(jnp.int32, (gate.shape[0], 1), 0)
    gate = jnp.where(jb < i, gate, -jnp.inf)
    sel = _topk_mask(gate, n_blocks, min(MOBA_TOPK, n_blocks)) & (gate > -jnp.inf)
    bias = jnp.where(sel, 0.0, NEG).astype(BF16)
    pad = jnp.zeros((LANES - HEAD_DIM - bias.shape[0], tq), BF16)
    qa = jnp.concatenate([qb, bias, pad], axis=0)
    qd = jnp.concatenate([qb, jnp.zeros((LANES - HEAD_DIM, tq), BF16)], axis=0)

    tpos = t0 + lax.broadcasted_iota(jnp.int32, (1, tq), 1)
    kpos = t0 + lax.broadcasted_iota(jnp.int32, (KEY_BLOCK, 1), 0)
    s = jnp.where(kpos <= tpos, _dot(ke_ref[i], qd), NEG)
    m0 = jnp.max(s, axis=0, keepdims=True)
    acc0 = _dot(vt_ref[i], jnp.exp(s - m0).astype(BF16))

    def step(kb, carry):
        m_old, acc = carry
        sb = _dot(ke_ref[kb], qa)
        m_new = jnp.maximum(m_old, jnp.max(sb, axis=0, keepdims=True))
        alpha = jnp.exp(m_old - m_new)
        pb = jnp.exp(sb - m_new).astype(BF16)
        return m_new, alpha * acc + _dot(vt_ref[kb], pb)

    _, acc = lax.fori_loop(0, i, step, (m0, acc0))
    y_ref[...] = _finish(acc)


def _moba(qmt, kmean, ke, vt):
    b, _, seq = qmt.shape
    nkb = seq // KEY_BLOCK
    tq = MOBA_BLOCK
    full5 = lambda bi, hi, i: (bi, hi, 0, 0, 0)
    return pl.pallas_call(
        functools.partial(_moba_body, n_blocks=nkb),
        grid=(b, MOBA_HEADS, seq // tq),
        in_specs=[pl.BlockSpec((None, HEAD_DIM, tq), lambda bi, hi, i: (bi, hi, i)),
                  pl.BlockSpec((None, None, 16, HEAD_DIM), lambda bi, hi, i: (bi, hi, 0, 0)),
                  pl.BlockSpec((None, None, nkb, KEY_BLOCK, LANES), full5),
                  pl.BlockSpec((None, None, nkb, V_ROWS, KEY_BLOCK), full5)],
        out_specs=pl.BlockSpec((None, HEAD_DIM, tq), lambda bi, hi, i: (bi, hi, i)),
        out_shape=jax.ShapeDtypeStruct((b, MOBA_WIDTH, seq), F32),
        compiler_params=pltpu.CompilerParams(
            dimension_semantics=("parallel", "parallel", "arbitrary"), vmem_limit_bytes=VMEM_LIMIT),
        name="moba",
    )(qmt, kmean, ke, vt)


def _merge_body(x_ref, yn_ref, ym_ref, gm_ref, wun_ref, wum_ref, wo_ref, g2_ref, x1_ref, h2_ref):
    a = _dot(yn_ref[...].astype(BF16), wun_ref[...])
    c = _dot(ym_ref[...].astype(BF16), wum_ref[...])
    mixed = gm_ref[:, 0:D_MODEL] * a + gm_ref[:, D_MODEL:2 * D_MODEL] * c
    x1 = x_ref[...] + _dot(mixed.astype(BF16), wo_ref[...])
    x1_ref[...] = x1
    ms = jnp.mean(x1 * x1, axis=-1, keepdims=True)
    h2_ref[...] = (x1 * lax.rsqrt(ms + NORM_EPS) * g2_ref[...]).astype(BF16)


def _merge(x2, yn, ym, gm, wun, wum, wo, g2):
    m = x2.shape[0]
    tm = 512
    row = lambda i: (i, 0)
    const = lambda i: (0, 0)
    return pl.pallas_call(
        _merge_body,
        grid=(m // tm,),
        in_specs=[pl.BlockSpec((tm, D_MODEL), row),
                  pl.BlockSpec((tm, NSA_WIDTH), row),
                  pl.BlockSpec((tm, MOBA_WIDTH), row),
                  pl.BlockSpec((tm, 2 * D_MODEL), row),
                  pl.BlockSpec((NSA_WIDTH, D_MODEL), const),
                  pl.BlockSpec((MOBA_WIDTH, D_MODEL), const),
                  pl.BlockSpec((D_MODEL, D_MODEL), const),
                  pl.BlockSpec((1, D_MODEL), const)],
        out_specs=[pl.BlockSpec((tm, D_MODEL), row), pl.BlockSpec((tm, D_MODEL), row)],
        out_shape=[jax.ShapeDtypeStruct((m, D_MODEL), F32), jax.ShapeDtypeStruct((m, D_MODEL), BF16)],
        compiler_params=pltpu.CompilerParams(dimension_semantics=("parallel",),
                                             vmem_limit_bytes=VMEM_LIMIT),
        name="merge",
    )(x2, yn, ym, gm, wun, wum, wo, g2)


def _ffn_body(x1_ref, h2_ref, w1_ref, w2_ref, gf_ref, o_ref, acc_ref):
    k = pl.program_id(1)
    u = jnp.maximum(_dot(h2_ref[...], w1_ref[...]), 0.0)
    part = _dot((u * u).astype(BF16), w2_ref[...])

    @pl.when(k == 0)
    def _():
        acc_ref[...] = x1_ref[...] + part

    @pl.when(k > 0)
    def _():
        acc_ref[...] += part

    @pl.when(k == pl.num_programs(1) - 1)
    def _():
        x2 = acc_ref[...]
        ms = jnp.mean(x2 * x2, axis=-1, keepdims=True)
        o_ref[...] = x2 * lax.rsqrt(ms + NORM_EPS) * gf_ref[...]


def _ffn(x1, h2, w1, w2, gf):
    m = x1.shape[0]
    tm = 512
    tf = 1024
    return pl.pallas_call(
        _ffn_body,
        grid=(m // tm, D_FF // tf),
        in_specs=[pl.BlockSpec((tm, D_MODEL), lambda i, k: (i, 0)),
                  pl.BlockSpec((tm, D_MODEL), lambda i, k: (i, 0)),
                  pl.BlockSpec((D_MODEL, tf), lambda i, k: (0, k)),
                  pl.BlockSpec((tf, D_MODEL), lambda i, k: (k, 0)),
                  pl.BlockSpec((1, D_MODEL), lambda i, k: (0, 0))],
        out_specs=pl.BlockSpec((tm, D_MODEL), lambda i, k: (i, 0)),
        out_shape=jax.ShapeDtypeStruct((m, D_MODEL), F32),
        scratch_shapes=[pltpu.VMEM((tm, D_MODEL), F32)],
        compiler_params=pltpu.CompilerParams(dimension_semantics=("parallel", "arbitrary"),
                                             vmem_limit_bytes=VMEM_LIMIT),
        name="ffn",
    )(x1, h2, w1, w2, gf)


def _rope_tables(seq):
    half = ROT_DIM // 2
    inv = ROPE_THETA ** (-jnp.arange(0, ROT_DIM, 2, dtype=F32) / ROT_DIM)
    ang = jnp.arange(seq, dtype=F32)[:, None] * inv[None, :]
    cos, sin = jnp.cos(ang), jnp.sin(ang)
    pad = HEAD_DIM - ROT_DIM
    one_head = lambda a, b, fill: jnp.concatenate([a, b, jnp.full((seq, pad), fill, F32)], axis=1)
    zeros = jnp.zeros((seq, half), F32)
    cos_t = one_head(cos, cos, 1.0)
    sin_hi = one_head(-sin, zeros, 0.0)
    sin_lo = one_head(zeros, sin, 0.0)
    two = lambda a: jnp.concatenate([a, a], axis=1)
    return two(cos_t), two(sin_hi), two(sin_lo)


def _block_indicator(seq, block, width):
    key_blk = np.arange(seq)[:, None] // block
    return jnp.asarray((key_blk == np.arange(width)[None, :]).astype(np.float32), dtype=BF16)


def _overlap_t(ncp, nsb):
    i = np.arange(ncp)[None, :]
    j = np.arange(HEAD_DIM)[:, None]
    start = i * CMP_STRIDE
    end = start + CMP_BLOCK - 1
    ov = (end >= j * SEL_BLOCK) & (start <= j * SEL_BLOCK + SEL_BLOCK - 1) & (i < ncp - 1) & (j < nsb)
    return jnp.asarray(ov.astype(np.float32), dtype=BF16)


def _values_t(v):
    b, n, seq, _ = v.shape
    vt = jnp.swapaxes(v, 2, 3).astype(BF16)
    ext = jnp.concatenate([vt, jnp.ones((b, n, 1, seq), BF16),
                           jnp.zeros((b, n, V_ROWS - HEAD_DIM - 1, seq), BF16)], axis=2)
    ext = ext.reshape(b, n, V_ROWS, seq // KEY_BLOCK, KEY_BLOCK)
    return jnp.transpose(ext, (0, 1, 3, 2, 4))


def _mixers(x, norm1_g, w_in, cmp_pe_k, cmp_pe_v, cmp_k_w1, cmp_k_w2, cmp_v_w1, cmp_v_w2):
    b, seq, _ = x.shape
    g, d = NSA_KV_GROUPS, HEAD_DIM
    nkb = seq // KEY_BLOCK
    ncp = seq // CMP_STRIDE
    nsb = seq // SEL_BLOCK
    layer = 0

    w = w_in[layer]
    w_all = jnp.concatenate(
        [w[:, :OFF_GN], w[:, OFF_GN:OFF_M], jnp.zeros((D_MODEL, GN_PAD - 3 * NSA_HEADS), w.dtype),
         w[:, OFF_M:]], axis=1).astype(BF16)
    cos_t, sin_hi_t, sin_lo_t = _rope_tables(seq)
    x2 = x.reshape(b * seq, D_MODEL)
    oq, okv, ogn, om, ogm = _in_proj(x2, norm1_g[layer][None, :], w_all, cos_t, sin_hi_t, sin_lo_t, seq)

    qt = jnp.swapaxes(oq.reshape(b, seq, 2 * NSA_WIDTH), 1, 2)
    gnt = jnp.swapaxes(ogn.reshape(b, seq, GN_PAD)[:, :, :3 * NSA_HEADS], 1, 2)
    gnt = gnt.reshape(b, g, 3 * NSA_GROUP_SIZE, seq)
    gnt = jnp.concatenate([gnt, jnp.zeros((b, g, 16 - 3 * NSA_GROUP_SIZE, seq), F32)], axis=2)
    kv = okv.reshape(b, seq, 6, g, d)
    per_group = lambda j: jnp.swapaxes(kv[:, :, j], 1, 2)
    blocked = lambda a: a.reshape(b, g, ncp, CMP_STRIDE * d)
    kc, vct = _compress(
        blocked(per_group(0)), blocked(per_group(1)),
        jnp.broadcast_to(cmp_pe_k[layer].reshape(1, -1), (8, CMP_BLOCK * d)),
        jnp.broadcast_to(cmp_pe_v[layer].reshape(1, -1), (8, CMP_BLOCK * d)),
        cmp_k_w1[layer].astype(BF16), cmp_v_w1[layer].astype(BF16),
        cmp_k_w2[layer].astype(BF16), cmp_v_w2[layer].T.astype(BF16))
    e_sel = jnp.broadcast_to(_block_indicator(seq, SEL_BLOCK, d), (b, g, seq, d))
    ke = jnp.concatenate([per_group(2).astype(BF16), e_sel], axis=-1).reshape(b, g, nkb, KEY_BLOCK, LANES)
    kw = per_group(4).astype(BF16).reshape(b, g, nkb, KEY_BLOCK, d)
    ynt = _nsa(qt, gnt, kc, vct, ke, _values_t(per_group(3)), kw, _values_t(per_group(5)),
               _overlap_t(ncp, nsb))

    om3 = om.reshape(b, seq, 3 * MOBA_WIDTH)
    heads = lambda a: jnp.swapaxes(a.reshape(b, seq, MOBA_HEADS, d), 1, 2)
    qmt = jnp.swapaxes(om3[:, :, :MOBA_WIDTH], 1, 2)
    kmean = _kmean(om3).reshape(b, nkb, MOBA_HEADS, d)
    kmean = jnp.swapaxes(kmean, 1, 2)
    kmean = jnp.concatenate([kmean, jnp.zeros((b, MOBA_HEADS, 16 - nkb, d), F32)], axis=2)
    e_moba = jnp.broadcast_to(_block_indicator(seq, MOBA_BLOCK, LANES - d), (b, MOBA_HEADS, seq, LANES - d))
    kem = jnp.concatenate([heads(om3[:, :, MOBA_WIDTH:2 * MOBA_WIDTH]).astype(BF16), e_moba], axis=-1)
    kem = kem.reshape(b, MOBA_HEADS, nkb, KEY_BLOCK, LANES)
    ymt = _moba(qmt, kmean, kem, _values_t(heads(om3[:, :, 2 * MOBA_WIDTH:])))

    yn = jnp.swapaxes(ynt, 1, 2).reshape(b * seq, NSA_WIDTH)
    ym = jnp.swapaxes(ymt, 1, 2).reshape(b * seq, MOBA_WIDTH)
    return x2, yn, ym, ogm


def kernel(x, norm1_g, w_in, cmp_pe_k, cmp_pe_v, cmp_k_w1, cmp_k_w2, cmp_v_w1, cmp_v_w2,
           w_up_nsa, w_up_moba, w_out, norm2_g, w_ff1, w_ff2, norm_f_g):
    b, seq, _ = x.shape
    layer = 0
    x2, yn, ym, ogm = _mixers(x, norm1_g, w_in, cmp_pe_k, cmp_pe_v, cmp_k_w1, cmp_k_w2, cmp_v_w1, cmp_v_w2)
    x1, h2 = _merge(x2, yn, ym, ogm, w_up_nsa[layer].astype(BF16), w_up_moba[layer].astype(BF16),
                    w_out[layer].astype(BF16), norm2_g[layer][None, :])
    out = _ffn(x1, h2, w_ff1[layer].astype(BF16), w_ff2[layer].astype(BF16), norm_f_g[None, :])
    return out.reshape(b, seq, D_MODEL)
```

```python
import functools

import numpy as np
import jax
import jax.numpy as jnp
from jax import lax
from jax.experimental import pallas as pl
from jax.experimental.pallas import tpu as pltpu

F32 = jnp.float32
BF16 = jnp.bfloat16

D_MODEL = 1024
HEAD_DIM = 64
ROT_DIM = HEAD_DIM // 4
ROPE_THETA = 500000.0
NORM_EPS = 1e-6

NSA_HEADS = 8
NSA_KV_GROUPS = 2
NSA_GROUP_SIZE = NSA_HEADS // NSA_KV_GROUPS
CMP_BLOCK = 32
CMP_STRIDE = 16
SEL_BLOCK = 64
SEL_TOPN = 16
WINDOW = 512

MOBA_HEADS = 8
MOBA_BLOCK = 256
MOBA_TOPK = 3

D_FF = 4 * D_MODEL
NSA_WIDTH = NSA_HEADS * HEAD_DIM
KV_WIDTH = NSA_KV_GROUPS * HEAD_DIM
MOBA_WIDTH = MOBA_HEADS * HEAD_DIM
OFF_KV = NSA_WIDTH
OFF_GN = OFF_KV + 6 * KV_WIDTH
OFF_M = OFF_GN + 3 * NSA_HEADS
OFF_GM = OFF_M + 3 * MOBA_WIDTH
IN_WIDTH = OFF_GM + 2 * D_MODEL

LANES = 128
KEY_BLOCK = 256
Q_TILE = 256
V_ROWS = 80
NEG = -1e30
SCALE = float(HEAD_DIM ** -0.5 * np.log2(np.e))
MOBA_HEADS_PER_STEP = 4
VMEM_LIMIT = 56 * 1024 * 1024

GN_PAD = LANES
SEC_Q = 0
SEC_KV = SEC_Q + NSA_WIDTH
SEC_GN = SEC_KV + 6 * KV_WIDTH
SEC_M = SEC_GN + GN_PAD
SEC_GM = SEC_M + 3 * MOBA_WIDTH
W_ALL = SEC_GM + 2 * D_MODEL


def _dot(a, b):
    return jnp.dot(a, b, preferred_element_type=F32)


def _sigmoid(y):
    return 1.0 / (1.0 + jnp.exp(-y))


def _in_proj_body(x_ref, g_ref, w_ref, c_ref, s1_ref, s2_ref,
                  oq_ref, okv_ref, ogn_ref, om_ref, ogm_ref, h_ref):
    x = x_ref[...]
    ms = jnp.mean(x * x, axis=-1, keepdims=True)
    h_ref[...] = (x * lax.rsqrt(ms + NORM_EPS) * g_ref[...]).astype(BF16)
    cos = c_ref[...]
    sin_hi = s1_ref[...]
    sin_lo = s2_ref[...]

    def proj(c0, width):
        return _dot(h_ref[...], w_ref[:, c0:c0 + width])

    def rope(yc):
        return (yc * cos + pltpu.roll(yc, LANES - ROT_DIM // 2, 1) * sin_hi
                + pltpu.roll(yc, ROT_DIM // 2, 1) * sin_lo)

    y = proj(SEC_Q, NSA_WIDTH)
    oq_ref[:, 0:NSA_WIDTH] = y
    for j in range(NSA_WIDTH // LANES):
        oq_ref[:, NSA_WIDTH + j * LANES:NSA_WIDTH + (j + 1) * LANES] = rope(y[:, j * LANES:(j + 1) * LANES])

    y = proj(SEC_KV, 6 * KV_WIDTH)
    for j in range(6):
        yc = y[:, j * LANES:(j + 1) * LANES]
        okv_ref[:, j * LANES:(j + 1) * LANES] = rope(yc) if j in (2, 4) else yc

    ogn_ref[...] = _sigmoid(proj(SEC_GN, GN_PAD))

    for part in range(3):
        y = proj(SEC_M + part * MOBA_WIDTH, MOBA_WIDTH)
        for j in range(MOBA_WIDTH // LANES):
            yc = y[:, j * LANES:(j + 1) * LANES]
            om_ref[:, part * MOBA_WIDTH + j * LANES:part * MOBA_WIDTH + (j + 1) * LANES] = (
                rope(yc) if part < 2 else yc)

    for part in range(4):
        w = 2 * D_MODEL // 4
        ogm_ref[:, part * w:(part + 1) * w] = _sigmoid(proj(SEC_GM + part * w, w))


def _in_proj(x2, g1, w_all, cos_t, sin_hi_t, sin_lo_t, seq):
    m = x2.shape[0]
    tm = 256
    nt_seq = seq // tm
    row = lambda i: (i, 0)
    const = lambda i: (0, 0)
    tab = lambda i: (i % nt_seq, 0)
    out_w = (2 * NSA_WIDTH, 6 * KV_WIDTH, GN_PAD, 3 * MOBA_WIDTH, 2 * D_MODEL)
    return pl.pallas_call(
        _in_proj_body,
        grid=(m // tm,),
        in_specs=[pl.BlockSpec((tm, D_MODEL), row),
                  pl.BlockSpec((1, D_MODEL), const),
                  pl.BlockSpec((D_MODEL, W_ALL), const),
                  pl.BlockSpec((tm, LANES), tab),
                  pl.BlockSpec((tm, LANES), tab),
                  pl.BlockSpec((tm, LANES), tab)],
        out_specs=[pl.BlockSpec((tm, w), row) for w in out_w],
        out_shape=[jax.ShapeDtypeStruct((m, w), F32) for w in out_w],
        scratch_shapes=[pltpu.VMEM((tm, D_MODEL), BF16)],
        compiler_params=pltpu.CompilerParams(dimension_semantics=("parallel",),
                                             vmem_limit_bytes=VMEM_LIMIT),
        name="in_proj",
    )(x2, g1, w_all, cos_t, sin_hi_t, sin_lo_t)


def _gelu_tanh(x):
    return 0.5 * x * (1.0 + jnp.tanh(np.sqrt(2.0 / np.pi) * (x + 0.044715 * (x * x * x))))


def _compress_body(kx_ref, vx_ref, pek_ref, pev_ref, kw1_ref, vw1_ref, kw2_ref, vw2t_ref,
                   kc_ref, vct_ref):
    half = CMP_STRIDE * HEAD_DIM
    ncp = kx_ref.shape[0]

    def hidden(x_ref, pe_ref, w1_ref):
        xb = x_ref[...].astype(BF16)
        ya = _dot(xb, w1_ref[0:half, :])
        yb = _dot(xb, w1_ref[half:2 * half, :])
        pe = _dot(pe_ref[...].astype(BF16), w1_ref[...])[0:1, :]
        return _gelu_tanh(ya + pltpu.roll(yb, ncp - 1, 0) + pe)

    hk = hidden(kx_ref, pek_ref, kw1_ref)
    kc_ref[...] = _dot(hk.astype(BF16), kw2_ref[...])
    hv = hidden(vx_ref, pev_ref, vw1_ref)
    vct_ref[...] = lax.dot_general(vw2t_ref[...], hv.astype(BF16), (((1,), (1,)), ((), ())),
                                   preferred_element_type=F32)


def _compress(kx, vx, pek, pev, kw1, vw1, kw2, vw2t):
    b, g, ncp, wide = kx.shape
    blk = lambda bi, gi: (bi, gi, 0, 0)
    const = lambda bi, gi: (0, 0)
    return pl.pallas_call(
        _compress_body,
        grid=(b, g),
        in_specs=[pl.BlockSpec((None, None, ncp, wide), blk),
                  pl.BlockSpec((None, None, ncp, wide), blk),
                  pl.BlockSpec((8, CMP_BLOCK * HEAD_DIM), const),
                  pl.BlockSpec((8, CMP_BLOCK * HEAD_DIM), const),
                  pl.BlockSpec((CMP_BLOCK * HEAD_DIM, HEAD_DIM), const),
                  pl.BlockSpec((CMP_BLOCK * HEAD_DIM, HEAD_DIM), const),
                  pl.BlockSpec((HEAD_DIM, HEAD_DIM), const),
                  pl.BlockSpec((HEAD_DIM, HEAD_DIM), const)],
        out_specs=[pl.BlockSpec((None, None, ncp, HEAD_DIM), blk),
                   pl.BlockSpec((None, None, HEAD_DIM, ncp), blk)],
        out_shape=[jax.ShapeDtypeStruct((b, g, ncp, HEAD_DIM), F32),
                   jax.ShapeDtypeStruct((b, g, HEAD_DIM, ncp), F32)],
        compiler_params=pltpu.CompilerParams(dimension_semantics=("parallel", "parallel")),
        name="compress",
    )(kx, vx, pek, pev, kw1, vw1, kw2, vw2t)


def _kmean_body(k_ref, o_ref):
    o_ref[...] = jnp.mean(k_ref[...], axis=0, keepdims=True)


def _kmean(om3):
    b, seq, _ = om3.shape
    nb = seq // MOBA_BLOCK
    return pl.pallas_call(
        _kmean_body,
        grid=(b, nb),
        in_specs=[pl.BlockSpec((None, MOBA_BLOCK, MOBA_WIDTH), lambda bi, j: (bi, j, 1))],
        out_specs=pl.BlockSpec((None, None, 1, MOBA_WIDTH), lambda bi, j: (bi, j, 0, 0)),
        out_shape=jax.ShapeDtypeStruct((b, nb, 1, MOBA_WIDTH), F32),
        compiler_params=pltpu.CompilerParams(dimension_semantics=("parallel", "parallel")),
        name="kmean",
    )(om3)


def _topk_mask(score, n_rows, k_top):
    rows = score.shape[0]
    j = lax.broadcasted_iota(jnp.int32, (rows, 1), 0)
    rank = jnp.zeros(score.shape, F32)
    for k in range(n_rows):
        rk = score[k:k + 1, :]
        tie_ahead = jnp.where(j > k, 1.0, 0.0)
        rank = rank + jnp.where(rk > score, 1.0, jnp.where(rk == score, tie_ahead, 0.0))
    return rank < k_top


def _finish(acc):
    return acc[0:HEAD_DIM, :] / acc[HEAD_DIM:HEAD_DIM + 1, :]


def _online_blocks(i, s_buf, m_ref, acc_ref, scores, values, causal):
    def park(slot, kb):
        s_buf[slot] = scores(jnp.maximum(kb, 0), kb >= 0)

    def update(sb, kb):
        m_old = m_ref[...]
        m_new = jnp.maximum(m_old, jnp.max(sb, axis=0, keepdims=True))
        pb = jnp.exp2(sb - m_new).astype(BF16)
        acc_ref[...] = jnp.exp2(m_old - m_new) * acc_ref[...] + values(jnp.maximum(kb, 0), pb)
        m_ref[...] = m_new

    m_ref[...] = jnp.full(m_ref.shape, NEG, F32)
    acc_ref[...] = jnp.zeros(acc_ref.shape, F32)
    first = -((i + 1) & 1)
    n_pairs = (i + 2) // 2
    park(0, first)

    @pl.loop(0, n_pairs - 1)
    def _(p):
        ka = first + 2 * p
        park(1, ka + 1)
        update(s_buf[0], ka)
        park(0, ka + 2)
        update(s_buf[1], ka + 1)

    park(1, i)
    update(s_buf[0], i - 1)
    update(jnp.where(causal, s_buf[1], NEG), i)
    return acc_ref[...]


def _nsa_body(q_ref, qr_ref, gn_ref, kc_ref, vct_ref, ke_ref, vts_ref, kw_ref, vtw_ref, ovt_ref,
              y_ref, s_buf, m_ref, acc_ref, *, n_sel_blocks):
    tq = q_ref.shape[1]
    rg = NSA_GROUP_SIZE
    i = pl.program_id(2)
    t0 = i * tq
    heads = lambda a: jnp.concatenate([a[r * HEAD_DIM:(r + 1) * HEAD_DIM, :] for r in range(rg)], axis=1)
    q = q_ref[...] * SCALE
    qr = qr_ref[...] * SCALE
    qc = heads(q).astype(BF16)
    qw = heads(qr).astype(BF16)
    tpos = t0 + lax.broadcasted_iota(jnp.int32, (1, tq), 1)
    tpos4 = jnp.concatenate([tpos] * rg, axis=1)

    ncp = kc_ref.shape[0]
    s = _dot(kc_ref[...].astype(BF16), qc)
    n = lax.broadcasted_iota(jnp.int32, (ncp, 1), 0)
    cmask = (n * CMP_STRIDE + (CMP_BLOCK - 1)) <= tpos4
    s = jnp.where(cmask, s, NEG)
    m = jnp.max(s, axis=0, keepdims=True)
    p = jnp.where(cmask, jnp.exp2(s - m), 0.0)
    pn = p / jnp.maximum(jnp.sum(p, axis=0, keepdims=True), 1e-30)
    o_cmp = _dot(vct_ref[...].astype(BF16), pn.astype(BF16))

    ps = pn[:, 0:tq]
    for r in range(1, rg):
        ps = ps + pn[:, r * tq:(r + 1) * tq]
    ps_hi = ps.astype(BF16)
    ps_lo = (ps - ps_hi.astype(F32)).astype(BF16)
    imp = _dot(ovt_ref[...], ps_hi) + _dot(ovt_ref[...], ps_lo)
    jb = lax.broadcasted_iota(jnp.int32, (imp.shape[0], 1), 0)
    cur = tpos >> (SEL_BLOCK.bit_length() - 1)
    forced = (jb == 0) | (jb == cur) | (jb == cur - 1)
    imp = jnp.where(forced, jnp.inf, imp)
    imp = jnp.where(jb <= cur, imp, -jnp.inf)
    sel = _topk_mask(imp, n_sel_blocks, min(SEL_TOPN, n_sel_blocks))
    bias = jnp.where(sel, 0.0, NEG).astype(BF16)

    qrb = qr.astype(BF16)
    qa = jnp.concatenate(
        [jnp.concatenate([qrb[r * HEAD_DIM:(r + 1) * HEAD_DIM, :], bias], axis=0) for r in range(rg)],
        axis=1)
    kpos = t0 + lax.broadcasted_iota(jnp.int32, (KEY_BLOCK, 1), 0)
    causal = kpos <= tpos4
    off = jnp.full(bias.shape, NEG, BF16)
    qa_off = jnp.concatenate(
        [jnp.concatenate([qrb[r * HEAD_DIM:(r + 1) * HEAD_DIM, :], off], axis=0) for r in range(rg)],
        axis=1)
    o_sel = _finish(_online_blocks(
        i, s_buf, m_ref, acc_ref,
        lambda kb, live: _dot(ke_ref[kb], jnp.where(live, qa, qa_off)),
        lambda kb, pb: _dot(vts_ref[kb], pb),
        causal))

    i1 = jnp.maximum(i - 1, 0)
    i2 = jnp.maximum(i - 2, 0)
    s0 = jnp.where(causal, _dot(kw_ref[i], qw), NEG)
    s1 = jnp.where(i >= 1, _dot(kw_ref[i1], qw), NEG)
    kpos2 = i2 * KEY_BLOCK + lax.broadcasted_iota(jnp.int32, (KEY_BLOCK, 1), 0)
    s2 = jnp.where((i >= 2) & (kpos2 > tpos4 - WINDOW), _dot(kw_ref[i2], qw), NEG)
    mw = jnp.maximum(jnp.maximum(jnp.max(s0, axis=0, keepdims=True), jnp.max(s1, axis=0, keepdims=True)),
                     jnp.max(s2, axis=0, keepdims=True))
    accw = (_dot(vtw_ref[i], jnp.exp2(s0 - mw).astype(BF16))
            + _dot(vtw_ref[i1], jnp.exp2(s1 - mw).astype(BF16))
            + _dot(vtw_ref[i2], jnp.exp2(s2 - mw).astype(BF16)))
    o_win = _finish(accw)

    gates = gn_ref[...]
    for r in range(rg):
        c = slice(r * tq, (r + 1) * tq)
        y_ref[r * HEAD_DIM:(r + 1) * HEAD_DIM, :] = (
            gates[3 * r:3 * r + 1, :] * o_cmp[:, c]
            + gates[3 * r + 1:3 * r + 2, :] * o_sel[:, c]
            + gates[3 * r + 2:3 * r + 3, :] * o_win[:, c])


def _nsa(qt, gnt, kc, vct, ke, vts, kw, vtw, ovt):
    b, _, seq = qt.shape
    g = NSA_KV_GROUPS
    nkb = seq // KEY_BLOCK
    ncp = kc.shape[2]
    tq = Q_TILE
    gw = NSA_GROUP_SIZE * HEAD_DIM
    full5 = lambda bi, gi, i: (bi, gi, 0, 0, 0)
    full4 = lambda bi, gi, i: (bi, gi, 0, 0)
    return pl.pallas_call(
        functools.partial(_nsa_body, n_sel_blocks=seq // SEL_BLOCK),
        grid=(b, g, seq // tq),
        in_specs=[pl.BlockSpec((None, gw, tq), lambda bi, gi, i: (bi, gi, i)),
                  pl.BlockSpec((None, gw, tq), lambda bi, gi, i: (bi, g + gi, i)),
                  pl.BlockSpec((None, None, 16, tq), lambda bi, gi, i: (bi, gi, 0, i)),
                  pl.BlockSpec((None, None, ncp, HEAD_DIM), full4),
                  pl.BlockSpec((None, None, HEAD_DIM, ncp), full4),
                  pl.BlockSpec((None, None, nkb, KEY_BLOCK, LANES), full5),
                  pl.BlockSpec((None, None, nkb, V_ROWS, KEY_BLOCK), full5),
                  pl.BlockSpec((None, None, nkb, KEY_BLOCK, HEAD_DIM), full5),
                  pl.BlockSpec((None, None, nkb, V_ROWS, KEY_BLOCK), full5),
                  pl.BlockSpec((HEAD_DIM, ncp), lambda bi, gi, i: (0, 0))],
        out_specs=pl.BlockSpec((None, gw, tq), lambda bi, gi, i: (bi, gi, i)),
        out_shape=jax.ShapeDtypeStruct((b, NSA_WIDTH, seq), F32),
        scratch_shapes=[pltpu.VMEM((2, KEY_BLOCK, NSA_GROUP_SIZE * tq), F32),
                        pltpu.VMEM((1, NSA_GROUP_SIZE * tq), F32),
                        pltpu.VMEM((V_ROWS, NSA_GROUP_SIZE * tq), F32)],
        compiler_params=pltpu.CompilerParams(
            dimension_semantics=("parallel", "parallel", "arbitrary"), vmem_limit_bytes=VMEM_LIMIT),
        name="nsa",
    )(qt, qt, gnt, kc, vct, ke, vts, kw, vtw, ovt)


def _moba_body(q_ref, km_ref, ke_ref, vt_ref, y_ref, s_buf, m_ref, acc_ref, *, n_blocks):
    tq = q_ref.shape[1]
    nh = km_ref.shape[0]
    i = pl.program_id(2)
    t0 = i * tq
    q = q_ref[...] * SCALE
    qb = q.astype(BF16)
    q_lo = (q - qb.astype(F32)).astype(BF16)
    head = lambda a, h: a[h * HEAD_DIM:(h + 1) * HEAD_DIM, :]
    gates = []
    for h in range(nh):
        km = km_ref[h]
        km_hi = km.astype(BF16)
        km_lo = (km - km_hi.astype(F32)).astype(BF16)
        gates.append(_dot(km_hi, head(qb, h)) + (_dot(km_hi, head(q_lo, h)) + _dot(km_lo, head(qb, h))))
    gate = jnp.concatenate(gates, axis=1)
    jb = lax.broadcasted_iota(jnp.int32, (gate.shape[0], 1), 0)
    gate = jnp.where(jb < i, gate, -jnp.inf)
    sel = _topk_mask(gate, n_blocks, min(MOBA_TOPK, n_blocks)) & (gate > -jnp.inf)
    bias = jnp.where(sel | (jb == i), 0.0, NEG).astype(BF16)
    pad = jnp.zeros((LANES - HEAD_DIM - bias.shape[0], tq), BF16)
    qa = [jnp.concatenate([head(qb, h), bias[:, h * tq:(h + 1) * tq], pad], axis=0) for h in range(nh)]

    tpos = t0 + (lax.broadcasted_iota(jnp.int32, (1, nh * tq), 1) & (tq - 1))
    kpos = t0 + lax.broadcasted_iota(jnp.int32, (KEY_BLOCK, 1), 0)

    off = jnp.full((bias.shape[0], tq), NEG, BF16)
    qa_off = [jnp.concatenate([head(qb, h), off, pad], axis=0) for h in range(nh)]

    def scores(kb, live):
        return jnp.concatenate([_dot(ke_ref[h, kb], jnp.where(live, qa[h], qa_off[h])) for h in range(nh)],
                               axis=1)

    def values(kb, pb):
        return jnp.concatenate([_dot(vt_ref[h, kb], pb[:, h * tq:(h + 1) * tq]) for h in range(nh)], axis=1)

    o = _finish(_online_blocks(i, s_buf, m_ref, acc_ref, scores, values, kpos <= tpos))
    for h in range(nh):
        y_ref[h * HEAD_DIM:(h + 1) * HEAD_DIM, :] = o[:, h * tq:(h + 1) * tq]


def _moba(qmt, kmean, ke, vt):
    b, _, seq = qmt.shape
    nkb = seq // KEY_BLOCK
    tq = MOBA_BLOCK
    nh = MOBA_HEADS_PER_STEP
    full5 = lambda bi, hi, i: (bi, hi, 0, 0, 0)
    return pl.pallas_call(
        functools.partial(_moba_body, n_blocks=nkb),
        grid=(b, MOBA_HEADS // nh, seq // tq),
        in_specs=[pl.BlockSpec((None, nh * HEAD_DIM, tq), lambda bi, hi, i: (bi, hi, i)),
                  pl.BlockSpec((None, nh, 16, HEAD_DIM), lambda bi, hi, i: (bi, hi, 0, 0)),
                  pl.BlockSpec((None, nh, nkb, KEY_BLOCK, LANES), full5),
                  pl.BlockSpec((None, nh, nkb, V_ROWS, KEY_BLOCK), full5)],
        out_specs=pl.BlockSpec((None, nh * HEAD_DIM, tq), lambda bi, hi, i: (bi, hi, i)),
        out_shape=jax.ShapeDtypeStruct((b, MOBA_WIDTH, seq), F32),
        scratch_shapes=[pltpu.VMEM((2, KEY_BLOCK, nh * tq), F32),
                        pltpu.VMEM((1, nh * tq), F32),
                        pltpu.VMEM((V_ROWS, nh * tq), F32)],
        compiler_params=pltpu.CompilerParams(
            dimension_semantics=("parallel", "parallel", "arbitrary"), vmem_limit_bytes=VMEM_LIMIT),
        name="moba",
    )(qmt, kmean, ke, vt)


def _merge_body(x_ref, yn_ref, ym_ref, gm_ref, wun_ref, wum_ref, wo_ref, g2_ref, x1_ref, h2_ref):
    a = _dot(yn_ref[...].astype(BF16), wun_ref[...])
    c = _dot(ym_ref[...].astype(BF16), wum_ref[...])
    mixed = gm_ref[:, 0:D_MODEL] * a + gm_ref[:, D_MODEL:2 * D_MODEL] * c
    x1 = x_ref[...] + _dot(mixed.astype(BF16), wo_ref[...])
    x1_ref[...] = x1
    ms = jnp.mean(x1 * x1, axis=-1, keepdims=True)
    h2_ref[...] = (x1 * lax.rsqrt(ms + NORM_EPS) * g2_ref[...]).astype(BF16)


def _merge(x2, yn, ym, gm, wun, wum, wo, g2):
    m = x2.shape[0]
    tm = 512
    row = lambda i: (i, 0)
    const = lambda i: (0, 0)
    return pl.pallas_call(
        _merge_body,
        grid=(m // tm,),
        in_specs=[pl.BlockSpec((tm, D_MODEL), row),
                  pl.BlockSpec((tm, NSA_WIDTH), row),
                  pl.BlockSpec((tm, MOBA_WIDTH), row),
                  pl.BlockSpec((tm, 2 * D_MODEL), row),
                  pl.BlockSpec((NSA_WIDTH, D_MODEL), const),
                  pl.BlockSpec((MOBA_WIDTH, D_MODEL), const),
                  pl.BlockSpec((D_MODEL, D_MODEL), const),
                  pl.BlockSpec((1, D_MODEL), const)],
        out_specs=[pl.BlockSpec((tm, D_MODEL), row), pl.BlockSpec((tm, D_MODEL), row)],
        out_shape=[jax.ShapeDtypeStruct((m, D_MODEL), F32), jax.ShapeDtypeStruct((m, D_MODEL), BF16)],
        compiler_params=pltpu.CompilerParams(dimension_semantics=("parallel",),
                                             vmem_limit_bytes=VMEM_LIMIT),
        name="merge",
    )(x2, yn, ym, gm, wun, wum, wo, g2)


def _ffn_body(x1_ref, h2_ref, w1_ref, w2_ref, gf_ref, o_ref, acc_ref):
    k = pl.program_id(1)
    u = jnp.maximum(_dot(h2_ref[...], w1_ref[...]), 0.0)
    part = _dot((u * u).astype(BF16), w2_ref[...])

    @pl.when(k == 0)
    def _():
        acc_ref[...] = x1_ref[...] + part

    @pl.when(k > 0)
    def _():
        acc_ref[...] += part

    @pl.when(k == pl.num_programs(1) - 1)
    def _():
        x2 = acc_ref[...]
        ms = jnp.mean(x2 * x2, axis=-1, keepdims=True)
        o_ref[...] = x2 * lax.rsqrt(ms + NORM_EPS) * gf_ref[...]


def _ffn(x1, h2, w1, w2, gf):
    m = x1.shape[0]
    tm = 512
    tf = 1024
    return pl.pallas_call(
        _ffn_body,
        grid=(m // tm, D_FF // tf),
        in_specs=[pl.BlockSpec((tm, D_MODEL), lambda i, k: (i, 0)),
                  pl.BlockSpec((tm, D_MODEL), lambda i, k: (i, 0)),
                  pl.BlockSpec((D_MODEL, tf), lambda i, k: (0, k)),
                  pl.BlockSpec((tf, D_MODEL), lambda i, k: (k, 0)),
                  pl.BlockSpec((1, D_MODEL), lambda i, k: (0, 0))],
        out_specs=pl.BlockSpec((tm, D_MODEL), lambda i, k: (i, 0)),
        out_shape=jax.ShapeDtypeStruct((m, D_MODEL), F32),
        scratch_shapes=[pltpu.VMEM((tm, D_MODEL), F32)],
        compiler_params=pltpu.CompilerParams(dimension_semantics=("parallel", "arbitrary"),
                                             vmem_limit_bytes=VMEM_LIMIT),
        name="ffn",
    )(x1, h2, w1, w2, gf)


def _rope_tables(seq):
    half = ROT_DIM // 2
    inv = ROPE_THETA ** (-jnp.arange(0, ROT_DIM, 2, dtype=F32) / ROT_DIM)
    ang = jnp.arange(seq, dtype=F32)[:, None] * inv[None, :]
    cos, sin = jnp.cos(ang), jnp.sin(ang)
    pad = HEAD_DIM - ROT_DIM
    one_head = lambda a, b, fill: jnp.concatenate([a, b, jnp.full((seq, pad), fill, F32)], axis=1)
    zeros = jnp.zeros((seq, half), F32)
    cos_t = one_head(cos, cos, 1.0)
    sin_hi = one_head(-sin, zeros, 0.0)
    sin_lo = one_head(zeros, sin, 0.0)
    two = lambda a: jnp.concatenate([a, a], axis=1)
    return two(cos_t), two(sin_hi), two(sin_lo)


def _block_indicator(seq, block, width):
    key_blk = np.arange(seq)[:, None] // block
    return jnp.asarray((key_blk == np.arange(width)[None, :]).astype(np.float32), dtype=BF16)


def _overlap_t(ncp, nsb):
    i = np.arange(ncp)[None, :]
    j = np.arange(HEAD_DIM)[:, None]
    start = i * CMP_STRIDE
    end = start + CMP_BLOCK - 1
    ov = (end >= j * SEL_BLOCK) & (start <= j * SEL_BLOCK + SEL_BLOCK - 1) & (i < ncp - 1) & (j < nsb)
    return jnp.asarray(ov.astype(np.float32), dtype=BF16)


def _values_t(v):
    b, n, seq, _ = v.shape
    vt = jnp.swapaxes(v, 2, 3).astype(BF16)
    ext = jnp.concatenate([vt, jnp.ones((b, n, 1, seq), BF16),
                           jnp.zeros((b, n, V_ROWS - HEAD_DIM - 1, seq), BF16)], axis=2)
    ext = ext.reshape(b, n, V_ROWS, seq // KEY_BLOCK, KEY_BLOCK)
    return jnp.transpose(ext, (0, 1, 3, 2, 4))


def _mixers(x, norm1_g, w_in, cmp_pe_k, cmp_pe_v, cmp_k_w1, cmp_k_w2, cmp_v_w1, cmp_v_w2):
    b, seq, _ = x.shape
    g, d = NSA_KV_GROUPS, HEAD_DIM
    nkb = seq // KEY_BLOCK
    ncp = seq // CMP_STRIDE
    nsb = seq // SEL_BLOCK
    layer = 0

    w = w_in[layer]
    w_all = jnp.concatenate(
        [w[:, :OFF_GN], w[:, OFF_GN:OFF_M], jnp.zeros((D_MODEL, GN_PAD - 3 * NSA_HEADS), w.dtype),
         w[:, OFF_M:]], axis=1).astype(BF16)
    cos_t, sin_hi_t, sin_lo_t = _rope_tables(seq)
    x2 = x.reshape(b * seq, D_MODEL)
    oq, okv, ogn, om, ogm = _in_proj(x2, norm1_g[layer][None, :], w_all, cos_t, sin_hi_t, sin_lo_t, seq)

    qt = jnp.swapaxes(oq.reshape(b, seq, 2 * NSA_WIDTH), 1, 2)
    gnt = jnp.swapaxes(ogn.reshape(b, seq, GN_PAD)[:, :, :3 * NSA_HEADS], 1, 2)
    gnt = gnt.reshape(b, g, 3 * NSA_GROUP_SIZE, seq)
    gnt = jnp.concatenate([gnt, jnp.zeros((b, g, 16 - 3 * NSA_GROUP_SIZE, seq), F32)], axis=2)
    kv = okv.reshape(b, seq, 6, g, d)
    per_group = lambda j: jnp.swapaxes(kv[:, :, j], 1, 2)
    blocked = lambda a: a.reshape(b, g, ncp, CMP_STRIDE * d)
    kc, vct = _compress(
        blocked(per_group(0)), blocked(per_group(1)),
        jnp.broadcast_to(cmp_pe_k[layer].reshape(1, -1), (8, CMP_BLOCK * d)),
        jnp.broadcast_to(cmp_pe_v[layer].reshape(1, -1), (8, CMP_BLOCK * d)),
        cmp_k_w1[layer].astype(BF16), cmp_v_w1[layer].astype(BF16),
        cmp_k_w2[layer].astype(BF16), cmp_v_w2[layer].T.astype(BF16))
    e_sel = jnp.broadcast_to(_block_indicator(seq, SEL_BLOCK, d), (b, g, seq, d))
    ke = jnp.concatenate([per_group(2).astype(BF16), e_sel], axis=-1).reshape(b, g, nkb, KEY_BLOCK, LANES)
    kw = per_group(4).astype(BF16).reshape(b, g, nkb, KEY_BLOCK, d)
    ynt = _nsa(qt, gnt, kc, vct, ke, _values_t(per_group(3)), kw, _values_t(per_group(5)),
               _overlap_t(ncp, nsb))

    om3 = om.reshape(b, seq, 3 * MOBA_WIDTH)
    heads = lambda a: jnp.swapaxes(a.reshape(b, seq, MOBA_HEADS, d), 1, 2)
    qmt = jnp.swapaxes(om3[:, :, :MOBA_WIDTH], 1, 2)
    kmean = _kmean(om3).reshape(b, nkb, MOBA_HEADS, d)
    kmean = jnp.swapaxes(kmean, 1, 2)
    kmean = jnp.concatenate([kmean, jnp.zeros((b, MOBA_HEADS, 16 - nkb, d), F32)], axis=2)
    e_moba = jnp.broadcast_to(_block_indicator(seq, MOBA_BLOCK, LANES - d), (b, MOBA_HEADS, seq, LANES - d))
    kem = jnp.concatenate([heads(om3[:, :, MOBA_WIDTH:2 * MOBA_WIDTH]).astype(BF16), e_moba], axis=-1)
    kem = kem.reshape(b, MOBA_HEADS, nkb, KEY_BLOCK, LANES)
    ymt = _moba(qmt, kmean, kem, _values_t(heads(om3[:, :, 2 * MOBA_WIDTH:])))

    yn = jnp.swapaxes(ynt, 1, 2).reshape(b * seq, NSA_WIDTH)
    ym = jnp.swapaxes(ymt, 1, 2).reshape(b * seq, MOBA_WIDTH)
    return x2, yn, ym, ogm


def kernel(x, norm1_g, w_in, cmp_pe_k, cmp_pe_v, cmp_k_w1, cmp_k_w2, cmp_v_w1, cmp_v_w2,
           w_up_nsa, w_up_moba, w_out, norm2_g, w_ff1, w_ff2, norm_f_g):
    b, seq, _ = x.shape
    layer = 0
    x2, yn, ym, ogm = _mixers(x, norm1_g, w_in, cmp_pe_k, cmp_pe_v, cmp_k_w1, cmp_k_w2, cmp_v_w1, cmp_v_w2)
    x1, h2 = _merge(x2, yn, ym, ogm, w_up_nsa[layer].astype(BF16), w_up_moba[layer].astype(BF16),
                    w_out[layer].astype(BF16), norm2_g[layer][None, :])
    out = _ffn(x1, h2, w_ff1[layer].astype(BF16), w_ff2[layer].astype(BF16), norm_f_g[None, :])
    return out.reshape(b, seq, D_MODEL)
```

```python
import functools

import numpy as np
import jax
import jax.numpy as jnp
from jax import lax
from jax.experimental import pallas as pl
from jax.experimental.pallas import tpu as pltpu

F32 = jnp.float32
BF16 = jnp.bfloat16

D_MODEL = 1024
HEAD_DIM = 64
ROT_DIM = HEAD_DIM // 4
ROPE_THETA = 500000.0
NORM_EPS = 1e-6

NSA_HEADS = 8
NSA_KV_GROUPS = 2
NSA_GROUP_SIZE = NSA_HEADS // NSA_KV_GROUPS
CMP_BLOCK = 32
CMP_STRIDE = 16
SEL_BLOCK = 64
SEL_TOPN = 16
WINDOW = 512

MOBA_HEADS = 8
MOBA_BLOCK = 256
MOBA_TOPK = 3

D_FF = 4 * D_MODEL
NSA_WIDTH = NSA_HEADS * HEAD_DIM
KV_WIDTH = NSA_KV_GROUPS * HEAD_DIM
MOBA_WIDTH = MOBA_HEADS * HEAD_DIM
OFF_KV = NSA_WIDTH
OFF_GN = OFF_KV + 6 * KV_WIDTH
OFF_M = OFF_GN + 3 * NSA_HEADS
OFF_GM = OFF_M + 3 * MOBA_WIDTH
IN_WIDTH = OFF_GM + 2 * D_MODEL

LANES = 128
SUBLANES = 8
KEY_BLOCK = 256
Q_TILE = 256
V_ROWS = 80
K_AUG = 2 * LANES
NEG = -1e30
SCALE = float(HEAD_DIM ** -0.5 * np.log2(np.e))
MOBA_HEADS_PER_STEP = 4
GATE_ROWS = 16
VMEM_LIMIT = 56 * 1024 * 1024

A_KC, A_KSW, A_KM, A_GM = 0, 256, 512, 1024
A_WIDTH = A_GM + 2 * D_MODEL
B_QN, B_QM, B_V, B_VM, B_GN = 0, 512, 1024, 1280, 1792
B_ROWS = B_GN + NSA_KV_GROUPS * GATE_ROWS


def _dot(a, b):
    return jnp.dot(a, b, preferred_element_type=F32)


def _dot_nt(a, b):
    return lax.dot_general(a, b, (((1,), (1,)), ((), ())), preferred_element_type=F32)


def _sigmoid(y):
    return 1.0 / (1.0 + jnp.exp(-y))


def _in_proj_body(x_ref, g_ref, wa_ref, wb_ref, c_ref, s1_ref, s2_ref, ct_ref, st_ref,
                  qt_ref, qmt_ref, gnt_ref, kc_ref, ke_ref, kw_ref, vts_ref, vtw_ref,
                  kem_ref, vtm_ref, kmean_ref, gm_ref, h_ref, *, tiles_per_seq):
    tm = x_ref.shape[0]
    it = pl.program_id(0) % tiles_per_seq
    x = x_ref[...]
    ms = jnp.mean(x * x, axis=-1, keepdims=True)
    h_ref[...] = (x * lax.rsqrt(ms + NORM_EPS) * g_ref[...]).astype(BF16)

    cos = c_ref[...]
    sin_hi = s1_ref[...]
    sin_lo = s2_ref[...]

    def proj(c0, width):
        return _dot(h_ref[...], wa_ref[:, c0:c0 + width])

    def rope(yc):
        return (yc * cos + pltpu.roll(yc, LANES - ROT_DIM // 2, 1) * sin_hi
                + pltpu.roll(yc, ROT_DIM // 2, 1) * sin_lo)

    kc_ref[...] = proj(A_KC, 2 * KV_WIDTH)

    row = lax.broadcasted_iota(jnp.int32, (tm, 1), 0)
    lane = lax.broadcasted_iota(jnp.int32, (1, LANES), 1)
    y = proj(A_KSW, 2 * KV_WIDTH)
    ke_ref[:, 0:LANES] = rope(y[:, 0:LANES]).astype(BF16)
    sel_blk = (it * tm + row) >> (SEL_BLOCK.bit_length() - 1)
    ke_ref[:, LANES:K_AUG] = jnp.where(sel_blk == lane, 1.0, 0.0).astype(BF16)
    kw_ref[...] = rope(y[:, LANES:2 * LANES]).astype(BF16)

    y = proj(A_KM, MOBA_WIDTH)
    own_blk = jnp.where(lane == it, 1.0, 0.0).astype(BF16)
    sums = []
    for j in range(MOBA_WIDTH // LANES):
        kr = rope(y[:, j * LANES:(j + 1) * LANES])
        sums.append(jnp.mean(kr, axis=0, keepdims=True))
        kem_ref[:, j * K_AUG:j * K_AUG + LANES] = kr.astype(BF16)
        kem_ref[:, j * K_AUG + LANES:(j + 1) * K_AUG] = jnp.broadcast_to(own_blk, (tm, LANES))
    kmean_ref[...] = jnp.broadcast_to(jnp.concatenate(sums, axis=1), kmean_ref.shape)

    for part in range(4):
        w = 2 * D_MODEL // 4
        gm_ref[:, part * w:(part + 1) * w] = _sigmoid(proj(A_GM + part * w, w))

    cos_t = ct_ref[...]
    sin_t = st_ref[...]

    def proj_t(r0, rows):
        return _dot_nt(wb_ref[r0:r0 + rows, :], h_ref[...])

    def rope_t(yt):
        half = ROT_DIM // 2
        out = []
        for hd in range(yt.shape[0] // HEAD_DIM):
            a = yt[hd * HEAD_DIM:hd * HEAD_DIM + half, :]
            b = yt[hd * HEAD_DIM + half:hd * HEAD_DIM + ROT_DIM, :]
            out += [a * cos_t - b * sin_t, a * sin_t + b * cos_t,
                    yt[hd * HEAD_DIM + ROT_DIM:(hd + 1) * HEAD_DIM, :]]
        return jnp.concatenate(out, axis=0)

    y = proj_t(B_QN, NSA_WIDTH)
    qt_ref[0:NSA_WIDTH, :] = (y * SCALE).astype(BF16)
    qt_ref[NSA_WIDTH:2 * NSA_WIDTH, :] = (rope_t(y) * SCALE).astype(BF16)
    qmt_ref[...] = rope_t(proj_t(B_QM, MOBA_WIDTH))

    ones_row = jnp.where(lax.broadcasted_iota(jnp.int32, (V_ROWS - HEAD_DIM, tm), 0) == 0, 1.0, 0.0).astype(BF16)
    y = proj_t(B_V, 4 * HEAD_DIM)
    for g in range(NSA_KV_GROUPS):
        vts_ref[g, 0:HEAD_DIM, :] = y[g * HEAD_DIM:(g + 1) * HEAD_DIM, :].astype(BF16)
        vts_ref[g, HEAD_DIM:V_ROWS, :] = ones_row
        vtw_ref[g, 0:HEAD_DIM, :] = y[(2 + g) * HEAD_DIM:(3 + g) * HEAD_DIM, :].astype(BF16)
        vtw_ref[g, HEAD_DIM:V_ROWS, :] = ones_row
    y = proj_t(B_VM, MOBA_WIDTH)
    for hd in range(MOBA_HEADS):
        vtm_ref[hd, 0:HEAD_DIM, :] = y[hd * HEAD_DIM:(hd + 1) * HEAD_DIM, :].astype(BF16)
        vtm_ref[hd, HEAD_DIM:V_ROWS, :] = ones_row
    y = _sigmoid(proj_t(B_GN, NSA_KV_GROUPS * GATE_ROWS))
    for g in range(NSA_KV_GROUPS):
        gnt_ref[g] = y[g * GATE_ROWS:(g + 1) * GATE_ROWS, :]


def _in_proj(x2, g1, wa, wb, tabs, batch, seq):
    m = x2.shape[0]
    tm = KEY_BLOCK
    nt = seq // tm
    row = lambda i: (i, 0)
    const = lambda i: (0, 0)
    tab = lambda i: (i % nt, 0)
    tab_t = lambda i: (0, i % nt)
    feat = lambda i: (i // nt, 0, i % nt)
    blk4 = lambda i: (i // nt, i % nt, 0, 0)
    blk5 = lambda i: (i // nt, 0, i % nt, 0, 0)
    g = NSA_KV_GROUPS
    outs = [
        (jax.ShapeDtypeStruct((batch, 2 * NSA_WIDTH, seq), BF16), pl.BlockSpec((None, 2 * NSA_WIDTH, tm), feat)),
        (jax.ShapeDtypeStruct((batch, MOBA_WIDTH, seq), F32), pl.BlockSpec((None, MOBA_WIDTH, tm), feat)),
        (jax.ShapeDtypeStruct((batch, g, GATE_ROWS, seq), F32),
         pl.BlockSpec((None, g, GATE_ROWS, tm), lambda i: (i // nt, 0, 0, i % nt))),
        (jax.ShapeDtypeStruct((m, 2 * KV_WIDTH), F32), pl.BlockSpec((tm, 2 * KV_WIDTH), row)),
        (jax.ShapeDtypeStruct((batch, nt, tm, K_AUG), BF16), pl.BlockSpec((None, None, tm, K_AUG), blk4)),
        (jax.ShapeDtypeStruct((batch, nt, tm, LANES), BF16), pl.BlockSpec((None, None, tm, LANES), blk4)),
        (jax.ShapeDtypeStruct((batch, g, nt, V_ROWS, tm), BF16), pl.BlockSpec((None, g, None, V_ROWS, tm), blk5)),
        (jax.ShapeDtypeStruct((batch, g, nt, V_ROWS, tm), BF16), pl.BlockSpec((None, g, None, V_ROWS, tm), blk5)),
        (jax.ShapeDtypeStruct((batch, nt, tm, (MOBA_HEADS // 2) * K_AUG), BF16),
         pl.BlockSpec((None, None, tm, (MOBA_HEADS // 2) * K_AUG), blk4)),
        (jax.ShapeDtypeStruct((batch, MOBA_HEADS, nt, V_ROWS, tm), BF16),
         pl.BlockSpec((None, MOBA_HEADS, None, V_ROWS, tm), blk5)),
        (jax.ShapeDtypeStruct((batch, nt * SUBLANES, MOBA_WIDTH), F32),
         pl.BlockSpec((None, SUBLANES, MOBA_WIDTH), lambda i: (i // nt, i % nt, 0))),
        (jax.ShapeDtypeStruct((m, 2 * D_MODEL), F32), pl.BlockSpec((tm, 2 * D_MODEL), row)),
    ]
    return pl.pallas_call(
        functools.partial(_in_proj_body, tiles_per_seq=nt),
        grid=(m // tm,),
        in_specs=[pl.BlockSpec((tm, D_MODEL), row),
                  pl.BlockSpec((1, D_MODEL), const),
                  pl.BlockSpec((D_MODEL, A_WIDTH), const),
                  pl.BlockSpec((B_ROWS, D_MODEL), const),
                  pl.BlockSpec((tm, LANES), tab),
                  pl.BlockSpec((tm, LANES), tab),
                  pl.BlockSpec((tm, LANES), tab),
                  pl.BlockSpec((SUBLANES, tm), tab_t),
                  pl.BlockSpec((SUBLANES, tm), tab_t)],
        out_specs=[o[1] for o in outs],
        out_shape=[o[0] for o in outs],
        scratch_shapes=[pltpu.VMEM((tm, D_MODEL), BF16)],
        compiler_params=pltpu.CompilerParams(dimension_semantics=("parallel",),
                                             vmem_limit_bytes=VMEM_LIMIT),
        name="in_proj",
    )(x2, g1, wa, wb, *tabs)


def _gelu_tanh(x):
    return 0.5 * x * (1.0 + jnp.tanh(np.sqrt(2.0 / np.pi) * (x + 0.044715 * (x * x * x))))


def _compress_body(kx_ref, vx_ref, pek_ref, pev_ref, kw1_ref, vw1_ref, kw2_ref, vw2t_ref,
                   kc_ref, vct_ref):
    half = CMP_STRIDE * HEAD_DIM
    ncp = kx_ref.shape[0]

    def hidden(x_ref, pe_ref, w1_ref):
        xb = x_ref[...].astype(BF16)
        ya = _dot(xb, w1_ref[0:half, :])
        yb = _dot(xb, w1_ref[half:2 * half, :])
        pe = _dot(pe_ref[...].astype(BF16), w1_ref[...])[0:1, :]
        return _gelu_tanh(ya + pltpu.roll(yb, ncp - 1, 0) + pe)

    hk = hidden(kx_ref, pek_ref, kw1_ref)
    kc_ref[...] = _dot(hk.astype(BF16), kw2_ref[...])
    hv = hidden(vx_ref, pev_ref, vw1_ref)
    vct_ref[...] = _dot_nt(vw2t_ref[...], hv.astype(BF16))


def _compress(kx, vx, pek, pev, kw1, vw1, kw2, vw2t):
    b, g, ncp, wide = kx.shape
    blk = lambda bi, gi: (bi, gi, 0, 0)
    const = lambda bi, gi: (0, 0)
    return pl.pallas_call(
        _compress_body,
        grid=(b, g),
        in_specs=[pl.BlockSpec((None, None, ncp, wide), blk),
                  pl.BlockSpec((None, None, ncp, wide), blk),
                  pl.BlockSpec((8, CMP_BLOCK * HEAD_DIM), const),
                  pl.BlockSpec((8, CMP_BLOCK * HEAD_DIM), const),
                  pl.BlockSpec((CMP_BLOCK * HEAD_DIM, HEAD_DIM), const),
                  pl.BlockSpec((CMP_BLOCK * HEAD_DIM, HEAD_DIM), const),
                  pl.BlockSpec((HEAD_DIM, HEAD_DIM), const),
                  pl.BlockSpec((HEAD_DIM, HEAD_DIM), const)],
        out_specs=[pl.BlockSpec((None, None, ncp, HEAD_DIM), blk),
                   pl.BlockSpec((None, None, HEAD_DIM, ncp), blk)],
        out_shape=[jax.ShapeDtypeStruct((b, g, ncp, HEAD_DIM), F32),
                   jax.ShapeDtypeStruct((b, g, HEAD_DIM, ncp), F32)],
        compiler_params=pltpu.CompilerParams(dimension_semantics=("parallel", "parallel")),
        name="compress",
    )(kx, vx, pek, pev, kw1, vw1, kw2, vw2t)


def _topk_mask(score, n_rows, k_top):
    rows = score.shape[0]
    j = lax.broadcasted_iota(jnp.int32, (rows, 1), 0)
    rank = jnp.zeros(score.shape, F32)
    for k in range(n_rows):
        rk = score[k:k + 1, :]
        tie_ahead = jnp.where(j > k, 1.0, 0.0)
        rank = rank + jnp.where(rk > score, 1.0, jnp.where(rk == score, tie_ahead, 0.0))
    return rank < k_top


def _finish(acc):
    return acc[0:HEAD_DIM, :] / acc[HEAD_DIM:HEAD_DIM + 1, :]


def _online_blocks(i, s_buf, m_ref, acc_ref, scores, values, causal):
    def park(slot, kb):
        s_buf[slot] = scores(jnp.maximum(kb, 0), kb >= 0)

    def update(sb, kb):
        m_old = m_ref[...]
        m_new = jnp.maximum(m_old, jnp.max(sb, axis=0, keepdims=True))
        pb = jnp.exp2(sb - m_new).astype(BF16)
        acc_ref[...] = jnp.exp2(m_old - m_new) * acc_ref[...] + values(jnp.maximum(kb, 0), pb)
        m_ref[...] = m_new

    m_ref[...] = jnp.full(m_ref.shape, NEG, F32)
    acc_ref[...] = jnp.zeros(acc_ref.shape, F32)
    first = -((i + 1) & 1)
    n_pairs = (i + 2) // 2
    park(0, first)

    @pl.loop(0, n_pairs - 1)
    def _(p):
        ka = first + 2 * p
        park(1, ka + 1)
        update(s_buf[0], ka)
        park(0, ka + 2)
        update(s_buf[1], ka + 1)

    park(1, i)
    update(s_buf[0], i - 1)
    update(jnp.where(causal, s_buf[1], NEG), i)
    return acc_ref[...]


def _nsa_body(q_ref, qr_ref, gn_ref, kc_ref, vct_ref, ke_ref, vts_ref, kw_ref, vtw_ref, ovt_ref,
              y_ref, s_buf, m_ref, acc_ref, *, n_sel_blocks):
    tq = q_ref.shape[1]
    rg = NSA_GROUP_SIZE
    g = pl.program_id(1)
    i = pl.program_id(2)
    t0 = i * tq
    head = lambda a, r: a[r * HEAD_DIM:(r + 1) * HEAD_DIM, :]
    q = q_ref[...]
    qr = qr_ref[...]
    qc = jnp.concatenate([head(q, r) for r in range(rg)], axis=1)
    tpos = t0 + lax.broadcasted_iota(jnp.int32, (1, tq), 1)
    tpos4 = jnp.concatenate([tpos] * rg, axis=1)

    ncp = kc_ref.shape[0]
    s = _dot(kc_ref[...].astype(BF16), qc)
    n = lax.broadcasted_iota(jnp.int32, (ncp, 1), 0)
    cmask = (n * CMP_STRIDE + (CMP_BLOCK - 1)) <= tpos4
    m = jnp.max(jnp.where(cmask, s, NEG), axis=0, keepdims=True)
    p = jnp.where(cmask, jnp.exp2(s - m), 0.0)
    pn = p * (1.0 / jnp.maximum(jnp.sum(p, axis=0, keepdims=True), 1e-30))
    o_cmp = _dot(vct_ref[...].astype(BF16), pn.astype(BF16))

    ps = pn[:, 0:tq]
    for r in range(1, rg):
        ps = ps + pn[:, r * tq:(r + 1) * tq]
    ps_hi = ps.astype(BF16)
    ps_lo = (ps - ps_hi.astype(F32)).astype(BF16)
    imp = _dot(ovt_ref[...], ps_hi) + _dot(ovt_ref[...], ps_lo)
    jb = lax.broadcasted_iota(jnp.int32, (imp.shape[0], 1), 0)
    cur = tpos >> (SEL_BLOCK.bit_length() - 1)
    forced = (jb == 0) | (jb == cur) | (jb == cur - 1)
    imp = jnp.where(forced, jnp.inf, imp)
    imp = jnp.where(jb <= cur, imp, -jnp.inf)
    sel = _topk_mask(imp, n_sel_blocks, min(SEL_TOPN, n_sel_blocks))
    bias = jnp.where(sel, 0.0, NEG).astype(BF16)

    zero = jnp.zeros((HEAD_DIM, tq), BF16)
    is_g0 = g == 0
    place = lambda a: jnp.concatenate([jnp.where(is_g0, a, zero), jnp.where(is_g0, zero, a)], axis=0)
    qw = jnp.concatenate([place(head(qr, r)) for r in range(rg)], axis=1)

    off = jnp.full(bias.shape, NEG, BF16)
    aug = lambda b_rows: jnp.concatenate(
        [jnp.concatenate([qw[:, r * tq:(r + 1) * tq], b_rows, zero], axis=0) for r in range(rg)], axis=1)
    qa = aug(bias)
    qa_off = aug(off)
    kpos = t0 + lax.broadcasted_iota(jnp.int32, (KEY_BLOCK, 1), 0)
    causal = kpos <= tpos4
    o_sel = _finish(_online_blocks(
        i, s_buf, m_ref, acc_ref,
        lambda kb, live: _dot(ke_ref[kb], jnp.where(live, qa, qa_off)),
        lambda kb, pb: _dot(vts_ref[kb], pb),
        causal))

    i1 = jnp.maximum(i - 1, 0)
    i2 = jnp.maximum(i - 2, 0)
    s0 = jnp.where(causal, _dot(kw_ref[i], qw), NEG)
    s1 = jnp.where(i >= 1, _dot(kw_ref[i1], qw), NEG)
    kpos2 = i2 * KEY_BLOCK + lax.broadcasted_iota(jnp.int32, (KEY_BLOCK, 1), 0)
    s2 = jnp.where((i >= 2) & (kpos2 > tpos4 - WINDOW), _dot(kw_ref[i2], qw), NEG)
    mw = jnp.maximum(jnp.maximum(jnp.max(s0, axis=0, keepdims=True), jnp.max(s1, axis=0, keepdims=True)),
                     jnp.max(s2, axis=0, keepdims=True))
    accw = (_dot(vtw_ref[i], jnp.exp2(s0 - mw).astype(BF16))
            + _dot(vtw_ref[i1], jnp.exp2(s1 - mw).astype(BF16))
            + _dot(vtw_ref[i2], jnp.exp2(s2 - mw).astype(BF16)))
    o_win = _finish(accw)

    gates = gn_ref[...]
    for r in range(rg):
        c = slice(r * tq, (r + 1) * tq)
        y_ref[r * HEAD_DIM:(r + 1) * HEAD_DIM, :] = (
            gates[3 * r:3 * r + 1, :] * o_cmp[:, c]
            + gates[3 * r + 1:3 * r + 2, :] * o_sel[:, c]
            + gates[3 * r + 2:3 * r + 3, :] * o_win[:, c])


def _nsa(qt, gnt, kc, vct, ke, vts, kw, vtw, ovt):
    b, _, seq = qt.shape
    g = NSA_KV_GROUPS
    nkb = seq // KEY_BLOCK
    ncp = kc.shape[2]
    tq = Q_TILE
    gw = NSA_GROUP_SIZE * HEAD_DIM
    per_group5 = lambda bi, gi, i: (bi, gi, 0, 0, 0)
    per_group4 = lambda bi, gi, i: (bi, gi, 0, 0)
    per_batch4 = lambda bi, gi, i: (bi, 0, 0, 0)
    return pl.pallas_call(
        functools.partial(_nsa_body, n_sel_blocks=seq // SEL_BLOCK),
        grid=(b, g, seq // tq),
        in_specs=[pl.BlockSpec((None, gw, tq), lambda bi, gi, i: (bi, gi, i)),
                  pl.BlockSpec((None, gw, tq), lambda bi, gi, i: (bi, g + gi, i)),
                  pl.BlockSpec((None, None, GATE_ROWS, tq), lambda bi, gi, i: (bi, gi, 0, i)),
                  pl.BlockSpec((None, None, ncp, HEAD_DIM), per_group4),
                  pl.BlockSpec((None, None, HEAD_DIM, ncp), per_group4),
                  pl.BlockSpec((None, nkb, KEY_BLOCK, K_AUG), per_batch4),
                  pl.BlockSpec((None, None, nkb, V_ROWS, KEY_BLOCK), per_group5),
                  pl.BlockSpec((None, nkb, KEY_BLOCK, LANES), per_batch4),
                  pl.BlockSpec((None, None, nkb, V_ROWS, KEY_BLOCK), per_group5),
                  pl.BlockSpec((HEAD_DIM, ncp), lambda bi, gi, i: (0, 0))],
        out_specs=pl.BlockSpec((None, gw, tq), lambda bi, gi, i: (bi, gi, i)),
        out_shape=jax.ShapeDtypeStruct((b, NSA_WIDTH, seq), F32),
        scratch_shapes=[pltpu.VMEM((2, KEY_BLOCK, NSA_GROUP_SIZE * tq), F32),
                        pltpu.VMEM((1, NSA_GROUP_SIZE * tq), F32),
                        pltpu.VMEM((V_ROWS, NSA_GROUP_SIZE * tq), F32)],
        compiler_params=pltpu.CompilerParams(
            dimension_semantics=("parallel", "parallel", "arbitrary"), vmem_limit_bytes=VMEM_LIMIT),
        name="nsa",
    )(qt, qt, gnt, kc, vct, ke, vts, kw, vtw, ovt)


def _moba_body(q_ref, km_ref, ke_ref, vt_ref, y_ref, s_buf, m_ref, acc_ref, *, n_blocks):
    tq = q_ref.shape[1]
    nh = km_ref.shape[0]
    i = pl.program_id(2)
    t0 = i * tq
    q = q_ref[...] * SCALE
    qb = q.astype(BF16)
    q_lo = (q - qb.astype(F32)).astype(BF16)
    head = lambda a, h: a[h * HEAD_DIM:(h + 1) * HEAD_DIM, :]
    gates = []
    for h in range(nh):
        km = km_ref[h]
        km_hi = km.astype(BF16)
        km_lo = (km - km_hi.astype(F32)).astype(BF16)
        gates.append(_dot(km_hi, head(qb, h)) + (_dot(km_hi, head(q_lo, h)) + _dot(km_lo, head(qb, h))))
    gate = jnp.concatenate(gates, axis=1)
    jb = lax.broadcasted_iota(jnp.int32, (gate.shape[0], 1), 0)
    gate = jnp.where(jb < i, gate, -jnp.inf)
    sel = _topk_mask(gate, n_blocks, min(MOBA_TOPK, n_blocks)) & (gate > -jnp.inf)
    bias = jnp.where(sel | (jb == i), 0.0, NEG).astype(BF16)
    off = jnp.full((bias.shape[0], tq), NEG, BF16)
    zero = jnp.zeros((HEAD_DIM, tq), BF16)
    pad = jnp.zeros((K_AUG - LANES - bias.shape[0], tq), BF16)

    def aug(h, b_rows):
        pair = [head(qb, h), zero] if h % 2 == 0 else [zero, head(qb, h)]
        return jnp.concatenate(pair + [b_rows, pad], axis=0)

    qa = [aug(h, bias[:, h * tq:(h + 1) * tq]) for h in range(nh)]
    qa_off = [aug(h, off) for h in range(nh)]

    tpos = t0 + (lax.broadcasted_iota(jnp.int32, (1, nh * tq), 1) & (tq - 1))
    kpos = t0 + lax.broadcasted_iota(jnp.int32, (KEY_BLOCK, 1), 0)

    def scores(kb, live):
        return jnp.concatenate(
            [_dot(ke_ref[kb, :, (h // 2) * K_AUG:(h // 2 + 1) * K_AUG], jnp.where(live, qa[h], qa_off[h]))
             for h in range(nh)], axis=1)

    def values(kb, pb):
        return jnp.concatenate([_dot(vt_ref[h, kb], pb[:, h * tq:(h + 1) * tq]) for h in range(nh)], axis=1)

    o = _finish(_online_blocks(i, s_buf, m_ref, acc_ref, scores, values, kpos <= tpos))
    for h in range(nh):
        y_ref[h * HEAD_DIM:(h + 1) * HEAD_DIM, :] = o[:, h * tq:(h + 1) * tq]


def _moba(qmt, kmean, ke, vt):
    b, _, seq = qmt.shape
    nkb = seq // KEY_BLOCK
    tq = MOBA_BLOCK
    nh = MOBA_HEADS_PER_STEP
    return pl.pallas_call(
        functools.partial(_moba_body, n_blocks=nkb),
        grid=(b, MOBA_HEADS // nh, seq // tq),
        in_specs=[pl.BlockSpec((None, nh * HEAD_DIM, tq), lambda bi, hi, i: (bi, hi, i)),
                  pl.BlockSpec((None, nh, 16, HEAD_DIM), lambda bi, hi, i: (bi, hi, 0, 0)),
                  pl.BlockSpec((None, nkb, KEY_BLOCK, (nh // 2) * K_AUG), lambda bi, hi, i: (bi, 0, 0, hi)),
                  pl.BlockSpec((None, nh, nkb, V_ROWS, KEY_BLOCK), lambda bi, hi, i: (bi, hi, 0, 0, 0))],
        out_specs=pl.BlockSpec((None, nh * HEAD_DIM, tq), lambda bi, hi, i: (bi, hi, i)),
        out_shape=jax.ShapeDtypeStruct((b, MOBA_WIDTH, seq), F32),
        scratch_shapes=[pltpu.VMEM((2, KEY_BLOCK, nh * tq), F32),
                        pltpu.VMEM((1, nh * tq), F32),
                        pltpu.VMEM((V_ROWS, nh * tq), F32)],
        compiler_params=pltpu.CompilerParams(
            dimension_semantics=("parallel", "parallel", "arbitrary"), vmem_limit_bytes=VMEM_LIMIT),
        name="moba",
    )(qmt, kmean, ke, vt)


def _merge_body(x_ref, yn_ref, ym_ref, gm_ref, wun_ref, wum_ref, wo_ref, g2_ref, x1_ref, h2_ref):
    tn = (((0,), (0,)), ((), ()))
    a = lax.dot_general(yn_ref[...].astype(BF16), wun_ref[...], tn, preferred_element_type=F32)
    c = lax.dot_general(ym_ref[...].astype(BF16), wum_ref[...], tn, preferred_element_type=F32)
    mixed = gm_ref[:, 0:D_MODEL] * a + gm_ref[:, D_MODEL:2 * D_MODEL] * c
    x1 = x_ref[...] + _dot(mixed.astype(BF16), wo_ref[...])
    x1_ref[...] = x1
    ms = jnp.mean(x1 * x1, axis=-1, keepdims=True)
    h2_ref[...] = (x1 * lax.rsqrt(ms + NORM_EPS) * g2_ref[...]).astype(BF16)


def _merge(x2, ynt, ymt, gm, wun, wum, wo, g2):
    m = x2.shape[0]
    seq = ynt.shape[2]
    tm = 512
    nt = seq // tm
    row = lambda i: (i, 0)
    const = lambda i: (0, 0)
    feat = lambda i: (i // nt, 0, i % nt)
    return pl.pallas_call(
        _merge_body,
        grid=(m // tm,),
        in_specs=[pl.BlockSpec((tm, D_MODEL), row),
                  pl.BlockSpec((None, NSA_WIDTH, tm), feat),
                  pl.BlockSpec((None, MOBA_WIDTH, tm), feat),
                  pl.BlockSpec((tm, 2 * D_MODEL), row),
                  pl.BlockSpec((NSA_WIDTH, D_MODEL), const),
                  pl.BlockSpec((MOBA_WIDTH, D_MODEL), const),
                  pl.BlockSpec((D_MODEL, D_MODEL), const),
                  pl.BlockSpec((1, D_MODEL), const)],
        out_specs=[pl.BlockSpec((tm, D_MODEL), row), pl.BlockSpec((tm, D_MODEL), row)],
        out_shape=[jax.ShapeDtypeStruct((m, D_MODEL), F32), jax.ShapeDtypeStruct((m, D_MODEL), BF16)],
        compiler_params=pltpu.CompilerParams(dimension_semantics=("parallel",),
                                             vmem_limit_bytes=VMEM_LIMIT),
        name="merge",
    )(x2, ynt, ymt, gm, wun, wum, wo, g2)


def _ffn_body(x1_ref, h2_ref, w1_ref, w2_ref, gf_ref, o_ref, acc_ref):
    k = pl.program_id(1)
    u = jnp.maximum(_dot(h2_ref[...], w1_ref[...]), 0.0)
    part = _dot((u * u).astype(BF16), w2_ref[...])

    @pl.when(k == 0)
    def _():
        acc_ref[...] = x1_ref[...] + part

    @pl.when(k > 0)
    def _():
        acc_ref[...] += part

    @pl.when(k == pl.num_programs(1) - 1)
    def _():
        x2 = acc_ref[...]
        ms = jnp.mean(x2 * x2, axis=-1, keepdims=True)
        o_ref[...] = x2 * lax.rsqrt(ms + NORM_EPS) * gf_ref[...]


def _ffn(x1, h2, w1, w2, gf):
    m = x1.shape[0]
    tm = 512
    tf = 1024
    return pl.pallas_call(
        _ffn_body,
        grid=(m // tm, D_FF // tf),
        in_specs=[pl.BlockSpec((tm, D_MODEL), lambda i, k: (i, 0)),
                  pl.BlockSpec((tm, D_MODEL), lambda i, k: (i, 0)),
                  pl.BlockSpec((D_MODEL, tf), lambda i, k: (0, k)),
                  pl.BlockSpec((tf, D_MODEL), lambda i, k: (k, 0)),
                  pl.BlockSpec((1, D_MODEL), lambda i, k: (0, 0))],
        out_specs=pl.BlockSpec((tm, D_MODEL), lambda i, k: (i, 0)),
        out_shape=jax.ShapeDtypeStruct((m, D_MODEL), F32),
        scratch_shapes=[pltpu.VMEM((tm, D_MODEL), F32)],
        compiler_params=pltpu.CompilerParams(dimension_semantics=("parallel", "arbitrary"),
                                             vmem_limit_bytes=VMEM_LIMIT),
        name="ffn",
    )(x1, h2, w1, w2, gf)


def _rope_tables(seq):
    half = ROT_DIM // 2
    inv = ROPE_THETA ** (-jnp.arange(0, ROT_DIM, 2, dtype=F32) / ROT_DIM)
    ang = jnp.arange(seq, dtype=F32)[:, None] * inv[None, :]
    cos, sin = jnp.cos(ang), jnp.sin(ang)
    pad = HEAD_DIM - ROT_DIM
    one_head = lambda a, b, fill: jnp.concatenate([a, b, jnp.full((seq, pad), fill, F32)], axis=1)
    zeros = jnp.zeros((seq, half), F32)
    two = lambda a: jnp.concatenate([a, a], axis=1)
    return (two(one_head(cos, cos, 1.0)),
            two(one_head(-sin, zeros, 0.0)),
            two(one_head(zeros, sin, 0.0)),
            cos.T, sin.T)


def _overlap_t(ncp, nsb):
    i = np.arange(ncp)[None, :]
    j = np.arange(HEAD_DIM)[:, None]
    start = i * CMP_STRIDE
    end = start + CMP_BLOCK - 1
    ov = (end >= j * SEL_BLOCK) & (start <= j * SEL_BLOCK + SEL_BLOCK - 1) & (i < ncp - 1) & (j < nsb)
    return jnp.asarray(ov.astype(np.float32), dtype=BF16)


def _projection_weights(w):
    kv = lambda j: w[:, OFF_KV + j * KV_WIDTH:OFF_KV + (j + 1) * KV_WIDTH]
    mo = lambda j: w[:, OFF_M + j * MOBA_WIDTH:OFF_M + (j + 1) * MOBA_WIDTH]
    wa = jnp.concatenate([kv(0), kv(1), kv(2), kv(4), mo(1), w[:, OFF_GM:]], axis=1)
    gn = w[:, OFF_GN:OFF_M].T.reshape(NSA_KV_GROUPS, 3 * NSA_GROUP_SIZE, D_MODEL)
    gn = jnp.concatenate([gn, jnp.zeros((NSA_KV_GROUPS, GATE_ROWS - 3 * NSA_GROUP_SIZE, D_MODEL), w.dtype)], axis=1)
    wb = jnp.concatenate([w[:, :OFF_KV].T, mo(0).T, kv(3).T, kv(5).T, mo(2).T,
                          gn.reshape(NSA_KV_GROUPS * GATE_ROWS, D_MODEL)], axis=0)
    return wa.astype(BF16), wb.astype(BF16)


def _mixers(x, norm1_g, w_in, cmp_pe_k, cmp_pe_v, cmp_k_w1, cmp_k_w2, cmp_v_w1, cmp_v_w2):
    b, seq, _ = x.shape
    g, d = NSA_KV_GROUPS, HEAD_DIM
    nkb = seq // KEY_BLOCK
    ncp = seq // CMP_STRIDE
    nsb = seq // SEL_BLOCK
    layer = 0

    wa, wb = _projection_weights(w_in[layer])
    x2 = x.reshape(b * seq, D_MODEL)
    (qt, qmt, gnt, okc, ke, kw, vts, vtw, kem, vtm, kmean8, ogm) = _in_proj(
        x2, norm1_g[layer][None, :], wa, wb, _rope_tables(seq), b, seq)

    kcv = okc.reshape(b, ncp, CMP_STRIDE, 2, g, d)
    blocked = lambda j: jnp.transpose(kcv[:, :, :, j], (0, 3, 1, 2, 4)).reshape(b, g, ncp, CMP_STRIDE * d)
    kc, vct = _compress(
        blocked(0), blocked(1),
        jnp.broadcast_to(cmp_pe_k[layer].reshape(1, -1), (8, CMP_BLOCK * d)),
        jnp.broadcast_to(cmp_pe_v[layer].reshape(1, -1), (8, CMP_BLOCK * d)),
        cmp_k_w1[layer].astype(BF16), cmp_v_w1[layer].astype(BF16),
        cmp_k_w2[layer].astype(BF16), cmp_v_w2[layer].T.astype(BF16))
    ynt = _nsa(qt, gnt, kc, vct, ke, vts, kw, vtw, _overlap_t(ncp, nsb))

    kmean = kmean8[:, ::SUBLANES, :].reshape(b, nkb, MOBA_HEADS, d)
    kmean = jnp.swapaxes(kmean, 1, 2)
    kmean = jnp.concatenate([kmean, jnp.zeros((b, MOBA_HEADS, 16 - nkb, d), F32)], axis=2)
    ymt = _moba(qmt, kmean, kem, vtm)
    return x2, ynt, ymt, ogm


def kernel(x, norm1_g, w_in, cmp_pe_k, cmp_pe_v, cmp_k_w1, cmp_k_w2, cmp_v_w1, cmp_v_w2,
           w_up_nsa, w_up_moba, w_out, norm2_g, w_ff1, w_ff2, norm_f_g):
    b, seq, _ = x.shape
    layer = 0
    x2, ynt, ymt, ogm = _mixers(x, norm1_g, w_in, cmp_pe_k, cmp_pe_v, cmp_k_w1, cmp_k_w2, cmp_v_w1, cmp_v_w2)
    x1, h2 = _merge(x2, ynt, ymt, ogm, w_up_nsa[layer].astype(BF16), w_up_moba[layer].astype(BF16),
                    w_out[layer].astype(BF16), norm2_g[layer][None, :])
    out = _ffn(x1, h2, w_ff1[layer].astype(BF16), w_ff2[layer].astype(BF16), norm_f_g[None, :])
    return out.reshape(b, seq, D_MODEL)
```

```python
import functools

import numpy as np
import jax
import jax.numpy as jnp
from jax import lax
from jax.experimental import pallas as pl
from jax.experimental.pallas import tpu as pltpu

F32 = jnp.float32
BF16 = jnp.bfloat16

D_MODEL = 1024
HEAD_DIM = 64
ROT_DIM = HEAD_DIM // 4
ROPE_THETA = 500000.0
NORM_EPS = 1e-6

NSA_HEADS = 8
NSA_KV_GROUPS = 2
NSA_GROUP_SIZE = NSA_HEADS // NSA_KV_GROUPS
CMP_BLOCK = 32
CMP_STRIDE = 16
SEL_BLOCK = 64
SEL_TOPN = 16
WINDOW = 512

MOBA_HEADS = 8
MOBA_BLOCK = 256
MOBA_TOPK = 3

D_FF = 4 * D_MODEL
NSA_WIDTH = NSA_HEADS * HEAD_DIM
KV_WIDTH = NSA_KV_GROUPS * HEAD_DIM
MOBA_WIDTH = MOBA_HEADS * HEAD_DIM
OFF_KV = NSA_WIDTH
OFF_GN = OFF_KV + 6 * KV_WIDTH
OFF_M = OFF_GN + 3 * NSA_HEADS
OFF_GM = OFF_M + 3 * MOBA_WIDTH
IN_WIDTH = OFF_GM + 2 * D_MODEL

LANES = 128
SUBLANES = 8
KEY_BLOCK = 256
Q_TILE = 256
V_ROWS = 80
K_AUG = 2 * LANES
NEG = -1e30
SCALE = float(HEAD_DIM ** -0.5 * np.log2(np.e))
MOBA_HEADS_PER_STEP = 4
GATE_ROWS = 16
FF_CHUNK = 1024
VMEM_LIMIT = 56 * 1024 * 1024

A_KC, A_KSW, A_KM, A_GM = 0, 256, 512, 1024
A_WIDTH = A_GM + 2 * D_MODEL
B_QN, B_QM, B_V, B_VM, B_GN = 0, 512, 1024, 1280, 1792
B_ROWS = B_GN + NSA_KV_GROUPS * GATE_ROWS


def _dot(a, b):
    return jnp.dot(a, b, preferred_element_type=F32)


def _dot_nt(a, b):
    return lax.dot_general(a, b, (((1,), (1,)), ((), ())), preferred_element_type=F32)


def _sigmoid(y):
    return 1.0 / (1.0 + jnp.exp(-y))


def _in_proj_body(x_ref, g_ref, wa_ref, wb_ref, c_ref, s1_ref, s2_ref, ct_ref, st_ref,
                  qt_ref, qmt_ref, gnt_ref, kcx_ref, vcx_ref, ke_ref, kw_ref, vts_ref, vtw_ref,
                  kem_ref, vtm_ref, kmean_ref, gm_ref, h_ref, *, tiles_per_seq):
    tm = x_ref.shape[0]
    it = pl.program_id(0) % tiles_per_seq
    x = x_ref[...]
    ms = jnp.mean(x * x, axis=-1, keepdims=True)
    h_ref[...] = (x * lax.rsqrt(ms + NORM_EPS) * g_ref[...]).astype(BF16)

    cos = c_ref[...]
    sin_hi = s1_ref[...]
    sin_lo = s2_ref[...]

    def proj(c0, width):
        return _dot(h_ref[...], wa_ref[:, c0:c0 + width])

    def rope(yc):
        return (yc * cos + pltpu.roll(yc, LANES - ROT_DIM // 2, 1) * sin_hi
                + pltpu.roll(yc, ROT_DIM // 2, 1) * sin_lo)

    y = proj(A_KC, 2 * KV_WIDTH)
    kcx_ref[...] = y[:, 0:KV_WIDTH]
    vcx_ref[...] = y[:, KV_WIDTH:2 * KV_WIDTH]

    row = lax.broadcasted_iota(jnp.int32, (tm, 1), 0)
    lane = lax.broadcasted_iota(jnp.int32, (1, LANES), 1)
    y = proj(A_KSW, 2 * KV_WIDTH)
    ke_ref[:, 0:LANES] = rope(y[:, 0:LANES]).astype(BF16)
    sel_blk = (it * tm + row) >> (SEL_BLOCK.bit_length() - 1)
    ke_ref[:, LANES:K_AUG] = jnp.where(sel_blk == lane, 1.0, 0.0).astype(BF16)
    kw_ref[...] = rope(y[:, LANES:2 * LANES]).astype(BF16)

    y = proj(A_KM, MOBA_WIDTH)
    own_blk = jnp.where(lane == it, 1.0, 0.0).astype(BF16)
    sums = []
    for j in range(MOBA_WIDTH // LANES):
        kr = rope(y[:, j * LANES:(j + 1) * LANES])
        sums.append(jnp.mean(kr, axis=0, keepdims=True))
        kem_ref[:, j * K_AUG:j * K_AUG + LANES] = kr.astype(BF16)
        kem_ref[:, j * K_AUG + LANES:(j + 1) * K_AUG] = jnp.broadcast_to(own_blk, (tm, LANES))
    kmean_ref[...] = jnp.broadcast_to(jnp.concatenate(sums, axis=1), kmean_ref.shape)

    for part in range(4):
        w = 2 * D_MODEL // 4
        gm_ref[:, part * w:(part + 1) * w] = _sigmoid(proj(A_GM + part * w, w))

    cos_t = ct_ref[...]
    sin_t = st_ref[...]

    def proj_t(r0, rows):
        return _dot_nt(wb_ref[r0:r0 + rows, :], h_ref[...])

    def rope_t(yt):
        half = ROT_DIM // 2
        out = []
        for hd in range(yt.shape[0] // HEAD_DIM):
            a = yt[hd * HEAD_DIM:hd * HEAD_DIM + half, :]
            b = yt[hd * HEAD_DIM + half:hd * HEAD_DIM + ROT_DIM, :]
            out += [a * cos_t - b * sin_t, a * sin_t + b * cos_t,
                    yt[hd * HEAD_DIM + ROT_DIM:(hd + 1) * HEAD_DIM, :]]
        return jnp.concatenate(out, axis=0)

    y = proj_t(B_QN, NSA_WIDTH)
    qt_ref[0:NSA_WIDTH, :] = (y * SCALE).astype(BF16)
    qt_ref[NSA_WIDTH:2 * NSA_WIDTH, :] = (rope_t(y) * SCALE).astype(BF16)
    qmt_ref[...] = rope_t(proj_t(B_QM, MOBA_WIDTH))

    ones_row = jnp.where(lax.broadcasted_iota(jnp.int32, (V_ROWS - HEAD_DIM, tm), 0) == 0, 1.0, 0.0).astype(BF16)
    y = proj_t(B_V, 4 * HEAD_DIM)
    for g in range(NSA_KV_GROUPS):
        vts_ref[g, 0:HEAD_DIM, :] = y[g * HEAD_DIM:(g + 1) * HEAD_DIM, :].astype(BF16)
        vts_ref[g, HEAD_DIM:V_ROWS, :] = ones_row
        vtw_ref[g, 0:HEAD_DIM, :] = y[(2 + g) * HEAD_DIM:(3 + g) * HEAD_DIM, :].astype(BF16)
        vtw_ref[g, HEAD_DIM:V_ROWS, :] = ones_row
    y = proj_t(B_VM, MOBA_WIDTH)
    for hd in range(MOBA_HEADS):
        vtm_ref[hd, 0:HEAD_DIM, :] = y[hd * HEAD_DIM:(hd + 1) * HEAD_DIM, :].astype(BF16)
        vtm_ref[hd, HEAD_DIM:V_ROWS, :] = ones_row
    y = _sigmoid(proj_t(B_GN, NSA_KV_GROUPS * GATE_ROWS))
    for g in range(NSA_KV_GROUPS):
        gnt_ref[g] = y[g * GATE_ROWS:(g + 1) * GATE_ROWS, :]


def _in_proj(x2, g1, wa, wb, tabs, batch, seq):
    m = x2.shape[0]
    tm = KEY_BLOCK
    nt = seq // tm
    row = lambda i: (i, 0)
    const = lambda i: (0, 0)
    tab = lambda i: (i % nt, 0)
    tab_t = lambda i: (0, i % nt)
    feat = lambda i: (i // nt, 0, i % nt)
    blk4 = lambda i: (i // nt, i % nt, 0, 0)
    blk5 = lambda i: (i // nt, 0, i % nt, 0, 0)
    g = NSA_KV_GROUPS
    outs = [
        (jax.ShapeDtypeStruct((batch, 2 * NSA_WIDTH, seq), BF16), pl.BlockSpec((None, 2 * NSA_WIDTH, tm), feat)),
        (jax.ShapeDtypeStruct((batch, MOBA_WIDTH, seq), F32), pl.BlockSpec((None, MOBA_WIDTH, tm), feat)),
        (jax.ShapeDtypeStruct((batch, g, GATE_ROWS, seq), F32),
         pl.BlockSpec((None, g, GATE_ROWS, tm), lambda i: (i // nt, 0, 0, i % nt))),
        (jax.ShapeDtypeStruct((m, KV_WIDTH), F32), pl.BlockSpec((tm, KV_WIDTH), row)),
        (jax.ShapeDtypeStruct((m, KV_WIDTH), F32), pl.BlockSpec((tm, KV_WIDTH), row)),
        (jax.ShapeDtypeStruct((batch, nt, tm, K_AUG), BF16), pl.BlockSpec((None, None, tm, K_AUG), blk4)),
        (jax.ShapeDtypeStruct((batch, nt, tm, LANES), BF16), pl.BlockSpec((None, None, tm, LANES), blk4)),
        (jax.ShapeDtypeStruct((batch, g, nt, V_ROWS, tm), BF16), pl.BlockSpec((None, g, None, V_ROWS, tm), blk5)),
        (jax.ShapeDtypeStruct((batch, g, nt, V_ROWS, tm), BF16), pl.BlockSpec((None, g, None, V_ROWS, tm), blk5)),
        (jax.ShapeDtypeStruct((batch, nt, tm, (MOBA_HEADS // 2) * K_AUG), BF16),
         pl.BlockSpec((None, None, tm, (MOBA_HEADS // 2) * K_AUG), blk4)),
        (jax.ShapeDtypeStruct((batch, MOBA_HEADS, nt, V_ROWS, tm), BF16),
         pl.BlockSpec((None, MOBA_HEADS, None, V_ROWS, tm), blk5)),
        (jax.ShapeDtypeStruct((batch, nt * SUBLANES, MOBA_WIDTH), F32),
         pl.BlockSpec((None, SUBLANES, MOBA_WIDTH), lambda i: (i // nt, i % nt, 0))),
        (jax.ShapeDtypeStruct((m, 2 * D_MODEL), F32), pl.BlockSpec((tm, 2 * D_MODEL), row)),
    ]
    return pl.pallas_call(
        functools.partial(_in_proj_body, tiles_per_seq=nt),
        grid=(m // tm,),
        in_specs=[pl.BlockSpec((tm, D_MODEL), row),
                  pl.BlockSpec((1, D_MODEL), const),
                  pl.BlockSpec((D_MODEL, A_WIDTH), const),
                  pl.BlockSpec((B_ROWS, D_MODEL), const),
                  pl.BlockSpec((tm, LANES), tab),
                  pl.BlockSpec((tm, LANES), tab),
                  pl.BlockSpec((tm, LANES), tab),
                  pl.BlockSpec((SUBLANES, tm), tab_t),
                  pl.BlockSpec((SUBLANES, tm), tab_t)],
        out_specs=[o[1] for o in outs],
        out_shape=[o[0] for o in outs],
        scratch_shapes=[pltpu.VMEM((tm, D_MODEL), BF16)],
        compiler_params=pltpu.CompilerParams(dimension_semantics=("parallel",),
                                             vmem_limit_bytes=VMEM_LIMIT),
        name="in_proj",
    )(x2, g1, wa, wb, *tabs)


def _gelu_tanh(x):
    return 0.5 * x * (1.0 + jnp.tanh(np.sqrt(2.0 / np.pi) * (x + 0.044715 * (x * x * x))))


def _compress_body(kx_ref, vx_ref, pek_ref, pev_ref, kw1_ref, vw1_ref, kw2_ref, vw2t_ref,
                   kc_ref, vct_ref):
    ncp = kc_ref.shape[0]

    def hidden(x_ref, pe_ref, w1_ref):
        ya = jnp.zeros((ncp, LANES), F32)
        yb = jnp.zeros((ncp, LANES), F32)
        pe = jnp.zeros((SUBLANES, LANES), F32)
        for r in range(CMP_STRIDE):
            xr = x_ref[pl.ds(r, ncp, stride=CMP_STRIDE), :].astype(BF16)
            ya = ya + _dot(xr, w1_ref[r])
            yb = yb + _dot(xr, w1_ref[CMP_STRIDE + r])
        for l in range(CMP_BLOCK):
            pe = pe + _dot(pe_ref[l], w1_ref[l])
        return _gelu_tanh(ya + pltpu.roll(yb, ncp - 1, 0) + pe[0:1, :])

    hk = hidden(kx_ref, pek_ref, kw1_ref)
    kc_ref[...] = _dot(hk.astype(BF16), kw2_ref[...]).astype(BF16)
    hv = hidden(vx_ref, pev_ref, vw1_ref)
    vct = _dot_nt(vw2t_ref[...], hv.astype(BF16)).astype(BF16)
    ones_row = jnp.where(lax.broadcasted_iota(jnp.int32, (V_ROWS - HEAD_DIM, ncp), 0) == 0, 1.0, 0.0).astype(BF16)
    for g in range(NSA_KV_GROUPS):
        vct_ref[g, 0:HEAD_DIM, :] = vct[g * HEAD_DIM:(g + 1) * HEAD_DIM, :]
        vct_ref[g, HEAD_DIM:V_ROWS, :] = ones_row


def _compress(kx, vx, pek, pev, kw1, vw1, kw2, vw2t):
    b, seq, _ = kx.shape
    ncp = seq // CMP_STRIDE
    g = NSA_KV_GROUPS
    tok = lambda bi: (bi, 0, 0)
    const3 = lambda bi: (0, 0, 0)
    const2 = lambda bi: (0, 0)
    return pl.pallas_call(
        _compress_body,
        grid=(b,),
        in_specs=[pl.BlockSpec((None, seq, LANES), tok),
                  pl.BlockSpec((None, seq, LANES), tok),
                  pl.BlockSpec((CMP_BLOCK, SUBLANES, LANES), const3),
                  pl.BlockSpec((CMP_BLOCK, SUBLANES, LANES), const3),
                  pl.BlockSpec((CMP_BLOCK, LANES, LANES), const3),
                  pl.BlockSpec((CMP_BLOCK, LANES, LANES), const3),
                  pl.BlockSpec((LANES, LANES), const2),
                  pl.BlockSpec((LANES, LANES), const2)],
        out_specs=[pl.BlockSpec((None, ncp, LANES), tok),
                   pl.BlockSpec((None, g, V_ROWS, ncp), lambda bi: (bi, 0, 0, 0))],
        out_shape=[jax.ShapeDtypeStruct((b, ncp, LANES), BF16),
                   jax.ShapeDtypeStruct((b, g, V_ROWS, ncp), BF16)],
        compiler_params=pltpu.CompilerParams(dimension_semantics=("parallel",)),
        name="compress",
    )(kx, vx, pek, pev, kw1, vw1, kw2, vw2t)


def _topk_mask(score, n_rows, k_top):
    rows = score.shape[0]
    j = lax.broadcasted_iota(jnp.int32, (rows, 1), 0)
    rank = jnp.zeros(score.shape, F32)
    for k in range(n_rows):
        rk = score[k:k + 1, :]
        tie_ahead = jnp.where(j > k, 1.0, 0.0)
        rank = rank + jnp.where(rk > score, 1.0, jnp.where(rk == score, tie_ahead, 0.0))
    return rank < k_top


def _finish(acc):
    return acc[0:HEAD_DIM, :] / acc[HEAD_DIM:HEAD_DIM + 1, :]


def _online_blocks(i, s_buf, m_ref, acc_ref, scores, values, causal):
    def park(slot, kb):
        s_buf[slot] = scores(jnp.maximum(kb, 0), kb >= 0)

    def update(sb, kb):
        m_old = m_ref[...]
        m_new = jnp.maximum(m_old, jnp.max(sb, axis=0, keepdims=True))
        pb = jnp.exp2(sb - m_new).astype(BF16)
        acc_ref[...] = jnp.exp2(m_old - m_new) * acc_ref[...] + values(jnp.maximum(kb, 0), pb)
        m_ref[...] = m_new

    m_ref[...] = jnp.full(m_ref.shape, NEG, F32)
    acc_ref[...] = jnp.zeros(acc_ref.shape, F32)
    first = -((i + 1) & 1)
    n_pairs = (i + 2) // 2
    park(0, first)

    @pl.loop(0, n_pairs - 1)
    def _(p):
        ka = first + 2 * p
        park(1, ka + 1)
        update(s_buf[0], ka)
        park(0, ka + 2)
        update(s_buf[1], ka + 1)

    park(1, i)
    update(s_buf[0], i - 1)
    update(jnp.where(causal, s_buf[1], NEG), i)
    return acc_ref[...]


def _nsa_body(q_ref, qr_ref, gn_ref, kc_ref, vct_ref, ke_ref, vts_ref, kw_ref, vtw_ref, ovt_ref,
              y_ref, s_buf, m_ref, acc_ref, *, n_sel_blocks):
    tq = q_ref.shape[1]
    rg = NSA_GROUP_SIZE
    g = pl.program_id(1)
    i = pl.program_id(2)
    t0 = i * tq
    head = lambda a, r: a[r * HEAD_DIM:(r + 1) * HEAD_DIM, :]
    q = q_ref[...]
    qr = qr_ref[...]
    tpos = t0 + lax.broadcasted_iota(jnp.int32, (1, tq), 1)
    tpos4 = jnp.concatenate([tpos] * rg, axis=1)

    zero = jnp.zeros((HEAD_DIM, tq), BF16)
    is_g0 = g == 0
    place = lambda a: jnp.concatenate([jnp.where(is_g0, a, zero), jnp.where(is_g0, zero, a)], axis=0)
    qc = jnp.concatenate([place(head(q, r)) for r in range(rg)], axis=1)
    qw = jnp.concatenate([place(head(qr, r)) for r in range(rg)], axis=1)

    causal = (lax.broadcasted_iota(jnp.int32, (KEY_BLOCK, 1), 0)
              <= (lax.broadcasted_iota(jnp.int32, (1, rg * tq), 1) & (tq - 1)))

    i1 = jnp.maximum(i - 1, 0)
    i2 = jnp.maximum(i - 2, 0)
    s0 = jnp.where(causal, _dot(kw_ref[i], qw), NEG)
    s1 = jnp.where(i >= 1, _dot(kw_ref[i1], qw), NEG)
    s2 = jnp.where(jnp.logical_or(causal, i < 2), NEG, _dot(kw_ref[i2], qw))
    mw = jnp.maximum(jnp.maximum(jnp.max(s0, axis=0, keepdims=True), jnp.max(s1, axis=0, keepdims=True)),
                     jnp.max(s2, axis=0, keepdims=True))
    accw = (_dot(vtw_ref[i], jnp.exp2(s0 - mw).astype(BF16))
            + _dot(vtw_ref[i1], jnp.exp2(s1 - mw).astype(BF16))
            + _dot(vtw_ref[i2], jnp.exp2(s2 - mw).astype(BF16)))
    o_win = _finish(accw)

    ncp = kc_ref.shape[0]
    n = lax.broadcasted_iota(jnp.int32, (ncp, 1), 0)
    s = jnp.where((n * CMP_STRIDE + (CMP_BLOCK - 1)) <= tpos4, _dot(kc_ref[...], qc), NEG)
    p = jnp.exp2(s - jnp.max(s, axis=0, keepdims=True))
    acc = _dot(vct_ref[...], p.astype(BF16))
    inv = jnp.where(tpos4 >= CMP_BLOCK - 1, 1.0 / jnp.maximum(acc[HEAD_DIM:HEAD_DIM + 1, :], 1e-30), 0.0)
    o_cmp = acc[0:HEAD_DIM, :] * inv

    ps = p[:, 0:tq] * inv[:, 0:tq]
    for r in range(1, rg):
        ps = ps + p[:, r * tq:(r + 1) * tq] * inv[:, r * tq:(r + 1) * tq]
    ps_hi = ps.astype(BF16)
    ps_lo = (ps - ps_hi.astype(F32)).astype(BF16)
    imp = _dot(ovt_ref[...], ps_hi) + _dot(ovt_ref[...], ps_lo)
    jb = lax.broadcasted_iota(jnp.int32, (imp.shape[0], 1), 0)
    cur = tpos >> (SEL_BLOCK.bit_length() - 1)
    forced = (jb == 0) | (jb == cur) | (jb == cur - 1)
    imp = jnp.where(forced, jnp.inf, imp)
    imp = jnp.where(jb <= cur, imp, -jnp.inf)
    sel = _topk_mask(imp, n_sel_blocks, min(SEL_TOPN, n_sel_blocks))
    bias = jnp.where(sel, 0.0, NEG).astype(BF16)

    off = jnp.full(bias.shape, NEG, BF16)
    aug = lambda b_rows: jnp.concatenate(
        [jnp.concatenate([qw[:, r * tq:(r + 1) * tq], b_rows, zero], axis=0) for r in range(rg)], axis=1)
    qa = aug(bias)
    qa_off = aug(off)
    o_sel = _finish(_online_blocks(
        i, s_buf, m_ref, acc_ref,
        lambda kb, live: _dot(ke_ref[kb], jnp.where(live, qa, qa_off)),
        lambda kb, pb: _dot(vts_ref[kb], pb),
        causal))

    gates = gn_ref[...]
    for r in range(rg):
        c = slice(r * tq, (r + 1) * tq)
        y_ref[r * HEAD_DIM:(r + 1) * HEAD_DIM, :] = (
            gates[3 * r:3 * r + 1, :] * o_cmp[:, c]
            + gates[3 * r + 1:3 * r + 2, :] * o_sel[:, c]
            + gates[3 * r + 2:3 * r + 3, :] * o_win[:, c])


def _nsa(qt, gnt, kc, vct, ke, vts, kw, vtw, ovt):
    b, _, seq = qt.shape
    g = NSA_KV_GROUPS
    nkb = seq // KEY_BLOCK
    ncp = kc.shape[1]
    tq = Q_TILE
    gw = NSA_GROUP_SIZE * HEAD_DIM
    per_group5 = lambda bi, gi, i: (bi, gi, 0, 0, 0)
    per_group4 = lambda bi, gi, i: (bi, gi, 0, 0)
    per_batch4 = lambda bi, gi, i: (bi, 0, 0, 0)
    return pl.pallas_call(
        functools.partial(_nsa_body, n_sel_blocks=seq // SEL_BLOCK),
        grid=(b, g, seq // tq),
        in_specs=[pl.BlockSpec((None, gw, tq), lambda bi, gi, i: (bi, gi, i)),
                  pl.BlockSpec((None, gw, tq), lambda bi, gi, i: (bi, g + gi, i)),
                  pl.BlockSpec((None, None, GATE_ROWS, tq), lambda bi, gi, i: (bi, gi, 0, i)),
                  pl.BlockSpec((None, ncp, LANES), lambda bi, gi, i: (bi, 0, 0)),
                  pl.BlockSpec((None, None, V_ROWS, ncp), per_group4),
                  pl.BlockSpec((None, nkb, KEY_BLOCK, K_AUG), per_batch4),
                  pl.BlockSpec((None, None, nkb, V_ROWS, KEY_BLOCK), per_group5),
                  pl.BlockSpec((None, nkb, KEY_BLOCK, LANES), per_batch4),
                  pl.BlockSpec((None, None, nkb, V_ROWS, KEY_BLOCK), per_group5),
                  pl.BlockSpec((HEAD_DIM, ncp), lambda bi, gi, i: (0, 0))],
        out_specs=pl.BlockSpec((None, gw, tq), lambda bi, gi, i: (bi, gi, i)),
        out_shape=jax.ShapeDtypeStruct((b, NSA_WIDTH, seq), F32),
        scratch_shapes=[pltpu.VMEM((2, KEY_BLOCK, NSA_GROUP_SIZE * tq), F32),
                        pltpu.VMEM((1, NSA_GROUP_SIZE * tq), F32),
                        pltpu.VMEM((V_ROWS, NSA_GROUP_SIZE * tq), F32)],
        compiler_params=pltpu.CompilerParams(
            dimension_semantics=("parallel", "parallel", "arbitrary"), vmem_limit_bytes=VMEM_LIMIT),
        name="nsa",
    )(qt, qt, gnt, kc, vct, ke, vts, kw, vtw, ovt)


def _moba_body(q_ref, km_ref, ke_ref, vt_ref, y_ref, s_buf, m_ref, acc_ref, *, n_blocks):
    tq = q_ref.shape[1]
    nh = km_ref.shape[0]
    i = pl.program_id(2)
    t0 = i * tq
    q = q_ref[...] * SCALE
    qb = q.astype(BF16)
    q_lo = (q - qb.astype(F32)).astype(BF16)
    head = lambda a, h: a[h * HEAD_DIM:(h + 1) * HEAD_DIM, :]
    gates = []
    for h in range(nh):
        km = km_ref[h]
        km_hi = km.astype(BF16)
        km_lo = (km - km_hi.astype(F32)).astype(BF16)
        gates.append(_dot(km_hi, head(qb, h)) + (_dot(km_hi, head(q_lo, h)) + _dot(km_lo, head(qb, h))))
    gate = jnp.concatenate(gates, axis=1)
    jb = lax.broadcasted_iota(jnp.int32, (gate.shape[0], 1), 0)
    gate = jnp.where(jb < i, gate, -jnp.inf)
    sel = _topk_mask(gate, n_blocks, min(MOBA_TOPK, n_blocks)) & (gate > -jnp.inf)
    bias = jnp.where(sel | (jb == i), 0.0, NEG).astype(BF16)
    off = jnp.full((bias.shape[0], tq), NEG, BF16)
    zero = jnp.zeros((HEAD_DIM, tq), BF16)
    pad = jnp.zeros((K_AUG - LANES - bias.shape[0], tq), BF16)

    def aug(h, b_rows):
        pair = [head(qb, h), zero] if h % 2 == 0 else [zero, head(qb, h)]
        return jnp.concatenate(pair + [b_rows, pad], axis=0)

    qa = [aug(h, bias[:, h * tq:(h + 1) * tq]) for h in range(nh)]
    qa_off = [aug(h, off) for h in range(nh)]

    tpos = t0 + (lax.broadcasted_iota(jnp.int32, (1, nh * tq), 1) & (tq - 1))
    kpos = t0 + lax.broadcasted_iota(jnp.int32, (KEY_BLOCK, 1), 0)

    def scores(kb, live):
        return jnp.concatenate(
            [_dot(ke_ref[kb, :, (h // 2) * K_AUG:(h // 2 + 1) * K_AUG], jnp.where(live, qa[h], qa_off[h]))
             for h in range(nh)], axis=1)

    def values(kb, pb):
        return jnp.concatenate([_dot(vt_ref[h, kb], pb[:, h * tq:(h + 1) * tq]) for h in range(nh)], axis=1)

    o = _finish(_online_blocks(i, s_buf, m_ref, acc_ref, scores, values, kpos <= tpos))
    for h in range(nh):
        y_ref[h * HEAD_DIM:(h + 1) * HEAD_DIM, :] = o[:, h * tq:(h + 1) * tq]


def _moba(qmt, kmean, ke, vt):
    b, _, seq = qmt.shape
    nkb = seq // KEY_BLOCK
    tq = MOBA_BLOCK
    nh = MOBA_HEADS_PER_STEP
    return pl.pallas_call(
        functools.partial(_moba_body, n_blocks=nkb),
        grid=(b, MOBA_HEADS // nh, seq // tq),
        in_specs=[pl.BlockSpec((None, nh * HEAD_DIM, tq), lambda bi, hi, i: (bi, hi, i)),
                  pl.BlockSpec((None, nh, 16, HEAD_DIM), lambda bi, hi, i: (bi, hi, 0, 0)),
                  pl.BlockSpec((None, nkb, KEY_BLOCK, (nh // 2) * K_AUG), lambda bi, hi, i: (bi, 0, 0, hi)),
                  pl.BlockSpec((None, nh, nkb, V_ROWS, KEY_BLOCK), lambda bi, hi, i: (bi, hi, 0, 0, 0))],
        out_specs=pl.BlockSpec((None, nh * HEAD_DIM, tq), lambda bi, hi, i: (bi, hi, i)),
        out_shape=jax.ShapeDtypeStruct((b, MOBA_WIDTH, seq), F32),
        scratch_shapes=[pltpu.VMEM((2, KEY_BLOCK, nh * tq), F32),
                        pltpu.VMEM((1, nh * tq), F32),
                        pltpu.VMEM((V_ROWS, nh * tq), F32)],
        compiler_params=pltpu.CompilerParams(
            dimension_semantics=("parallel", "parallel", "arbitrary"), vmem_limit_bytes=VMEM_LIMIT),
        name="moba",
    )(qmt, kmean, ke, vt)


def _merge_body(x_ref, yn_ref, ym_ref, gm_ref, wun_ref, wum_ref, wo_ref, g2_ref, x1_ref, h2_ref):
    tn = (((0,), (0,)), ((), ()))
    a = lax.dot_general(yn_ref[...].astype(BF16), wun_ref[...], tn, preferred_element_type=F32)
    c = lax.dot_general(ym_ref[...].astype(BF16), wum_ref[...], tn, preferred_element_type=F32)
    mixed = gm_ref[:, 0:D_MODEL] * a + gm_ref[:, D_MODEL:2 * D_MODEL] * c
    x1 = x_ref[...] + _dot(mixed.astype(BF16), wo_ref[...])
    x1_ref[...] = x1
    ms = jnp.mean(x1 * x1, axis=-1, keepdims=True)
    h2_ref[...] = (x1 * lax.rsqrt(ms + NORM_EPS) * g2_ref[...]).astype(BF16)


def _merge(x2, ynt, ymt, gm, wun, wum, wo, g2):
    m = x2.shape[0]
    seq = ynt.shape[2]
    tm = 512
    nt = seq // tm
    row = lambda i: (i, 0)
    const = lambda i: (0, 0)
    feat = lambda i: (i // nt, 0, i % nt)
    return pl.pallas_call(
        _merge_body,
        grid=(m // tm,),
        in_specs=[pl.BlockSpec((tm, D_MODEL), row),
                  pl.BlockSpec((None, NSA_WIDTH, tm), feat),
                  pl.BlockSpec((None, MOBA_WIDTH, tm), feat),
                  pl.BlockSpec((tm, 2 * D_MODEL), row),
                  pl.BlockSpec((NSA_WIDTH, D_MODEL), const),
                  pl.BlockSpec((MOBA_WIDTH, D_MODEL), const),
                  pl.BlockSpec((D_MODEL, D_MODEL), const),
                  pl.BlockSpec((1, D_MODEL), const)],
        out_specs=[pl.BlockSpec((tm, D_MODEL), row), pl.BlockSpec((tm, D_MODEL), row)],
        out_shape=[jax.ShapeDtypeStruct((m, D_MODEL), F32), jax.ShapeDtypeStruct((m, D_MODEL), BF16)],
        compiler_params=pltpu.CompilerParams(dimension_semantics=("parallel",),
                                             vmem_limit_bytes=VMEM_LIMIT),
        name="merge",
    )(x2, ynt, ymt, gm, wun, wum, wo, g2)


def _ffn_body(x1_ref, h2_ref, w1_ref, w2_ref, gf_ref, o_ref):
    x2 = x1_ref[...]
    for c in range(D_FF // FF_CHUNK):
        u = jnp.maximum(_dot(h2_ref[...], w1_ref[:, c * FF_CHUNK:(c + 1) * FF_CHUNK]), 0.0)
        x2 = x2 + _dot((u * u).astype(BF16), w2_ref[c * FF_CHUNK:(c + 1) * FF_CHUNK, :])
    ms = jnp.mean(x2 * x2, axis=-1, keepdims=True)
    o_ref[...] = x2 * lax.rsqrt(ms + NORM_EPS) * gf_ref[...]


def _ffn(x1, h2, w1, w2, gf):
    m = x1.shape[0]
    tm = 512
    row = lambda i: (i, 0)
    const = lambda i: (0, 0)
    resident = pl.Buffered(1)
    return pl.pallas_call(
        _ffn_body,
        grid=(m // tm,),
        in_specs=[pl.BlockSpec((tm, D_MODEL), row),
                  pl.BlockSpec((tm, D_MODEL), row),
                  pl.BlockSpec((D_MODEL, D_FF), const, pipeline_mode=resident),
                  pl.BlockSpec((D_FF, D_MODEL), const, pipeline_mode=resident),
                  pl.BlockSpec((1, D_MODEL), const)],
        out_specs=pl.BlockSpec((tm, D_MODEL), row),
        out_shape=jax.ShapeDtypeStruct((m, D_MODEL), F32),
        compiler_params=pltpu.CompilerParams(dimension_semantics=("parallel",),
                                             vmem_limit_bytes=VMEM_LIMIT),
        name="ffn",
    )(x1, h2, w1, w2, gf)


def _rope_tables(seq):
    half = ROT_DIM // 2
    inv = ROPE_THETA ** (-jnp.arange(0, ROT_DIM, 2, dtype=F32) / ROT_DIM)
    ang = jnp.arange(seq, dtype=F32)[:, None] * inv[None, :]
    cos, sin = jnp.cos(ang), jnp.sin(ang)
    pad = HEAD_DIM - ROT_DIM
    one_head = lambda a, b, fill: jnp.concatenate([a, b, jnp.full((seq, pad), fill, F32)], axis=1)
    zeros = jnp.zeros((seq, half), F32)
    two = lambda a: jnp.concatenate([a, a], axis=1)
    return (two(one_head(cos, cos, 1.0)),
            two(one_head(-sin, zeros, 0.0)),
            two(one_head(zeros, sin, 0.0)),
            cos.T, sin.T)


def _overlap_t(ncp, nsb):
    i = np.arange(ncp)[None, :]
    j = np.arange(HEAD_DIM)[:, None]
    start = i * CMP_STRIDE
    end = start + CMP_BLOCK - 1
    ov = (end >= j * SEL_BLOCK) & (start <= j * SEL_BLOCK + SEL_BLOCK - 1) & (i < ncp - 1) & (j < nsb)
    return jnp.asarray(ov.astype(np.float32), dtype=BF16)


def _block_diag(w):
    z = jnp.zeros_like(w)
    return jnp.concatenate([jnp.concatenate([w, z], axis=-1), jnp.concatenate([z, w], axis=-1)], axis=-2)


def _projection_weights(w):
    kv = lambda j: w[:, OFF_KV + j * KV_WIDTH:OFF_KV + (j + 1) * KV_WIDTH]
    mo = lambda j: w[:, OFF_M + j * MOBA_WIDTH:OFF_M + (j + 1) * MOBA_WIDTH]
    wa = jnp.concatenate([kv(0), kv(1), kv(2), kv(4), mo(1), w[:, OFF_GM:]], axis=1)
    gn = w[:, OFF_GN:OFF_M].T.reshape(NSA_KV_GROUPS, 3 * NSA_GROUP_SIZE, D_MODEL)
    gn = jnp.concatenate([gn, jnp.zeros((NSA_KV_GROUPS, GATE_ROWS - 3 * NSA_GROUP_SIZE, D_MODEL), w.dtype)], axis=1)
    wb = jnp.concatenate([w[:, :OFF_KV].T, mo(0).T, kv(3).T, kv(5).T, mo(2).T,
                          gn.reshape(NSA_KV_GROUPS * GATE_ROWS, D_MODEL)], axis=0)
    return wa.astype(BF16), wb.astype(BF16)


def _mixers(x, norm1_g, w_in, cmp_pe_k, cmp_pe_v, cmp_k_w1, cmp_k_w2, cmp_v_w1, cmp_v_w2):
    b, seq, _ = x.shape
    g, d = NSA_KV_GROUPS, HEAD_DIM
    nkb = seq // KEY_BLOCK
    ncp = seq // CMP_STRIDE
    nsb = seq // SEL_BLOCK
    layer = 0

    wa, wb = _projection_weights(w_in[layer])
    x2 = x.reshape(b * seq, D_MODEL)
    (qt, qmt, gnt, kcx, vcx, ke, kw, vts, vtw, kem, vtm, kmean8, ogm) = _in_proj(
        x2, norm1_g[layer][None, :], wa, wb, _rope_tables(seq), b, seq)

    per_group = lambda a: jnp.concatenate([a] * g, axis=-1)
    pe_rows = lambda pe: jnp.broadcast_to(per_group(pe)[:, None, :], (CMP_BLOCK, SUBLANES, LANES)).astype(BF16)
    w1_blocks = lambda w1: _block_diag(w1.reshape(CMP_BLOCK, d, d)).astype(BF16)
    kc, vct = _compress(
        kcx.reshape(b, seq, KV_WIDTH), vcx.reshape(b, seq, KV_WIDTH),
        pe_rows(cmp_pe_k[layer]), pe_rows(cmp_pe_v[layer]),
        w1_blocks(cmp_k_w1[layer]), w1_blocks(cmp_v_w1[layer]),
        _block_diag(cmp_k_w2[layer]).astype(BF16), _block_diag(cmp_v_w2[layer].T).astype(BF16))
    ynt = _nsa(qt, gnt, kc, vct, ke, vts, kw, vtw, _overlap_t(ncp, nsb))

    kmean = kmean8[:, ::SUBLANES, :].reshape(b, nkb, MOBA_HEADS, d)
    kmean = jnp.swapaxes(kmean, 1, 2)
    kmean = jnp.concatenate([kmean, jnp.zeros((b, MOBA_HEADS, 16 - nkb, d), F32)], axis=2)
    ymt = _moba(qmt, kmean, kem, vtm)
    return x2, ynt, ymt, ogm


def kernel(x, norm1_g, w_in, cmp_pe_k, cmp_pe_v, cmp_k_w1, cmp_k_w2, cmp_v_w1, cmp_v_w2,
           w_up_nsa, w_up_moba, w_out, norm2_g, w_ff1, w_ff2, norm_f_g):
    b, seq, _ = x.shape
    layer = 0
    x2, ynt, ymt, ogm = _mixers(x, norm1_g, w_in, cmp_pe_k, cmp_pe_v, cmp_k_w1, cmp_k_w2, cmp_v_w1, cmp_v_w2)
    x1, h2 = _merge(x2, ynt, ymt, ogm, w_up_nsa[layer].astype(BF16), w_up_moba[layer].astype(BF16),
                    w_out[layer].astype(BF16), norm2_g[layer][None, :])
    out = _ffn(x1, h2, w_ff1[layer].astype(BF16), w_ff2[layer].astype(BF16), norm_f_g[None, :])
    return out.reshape(b, seq, D_MODEL)
```

```python
import functools

import numpy as np
import jax
import jax.numpy as jnp
from jax import lax
from jax.experimental import pallas as pl
from jax.experimental.pallas import tpu as pltpu

F32 = jnp.float32
BF16 = jnp.bfloat16

D_MODEL = 1024
HEAD_DIM = 64
ROT_DIM = HEAD_DIM // 4
ROPE_THETA = 500000.0
NORM_EPS = 1e-6

NSA_HEADS = 8
NSA_KV_GROUPS = 2
NSA_GROUP_SIZE = NSA_HEADS // NSA_KV_GROUPS
CMP_BLOCK = 32
CMP_STRIDE = 16
SEL_BLOCK = 64
SEL_TOPN = 16
WINDOW = 512

MOBA_HEADS = 8
MOBA_BLOCK = 256
MOBA_TOPK = 3

D_FF = 4 * D_MODEL
NSA_WIDTH = NSA_HEADS * HEAD_DIM
KV_WIDTH = NSA_KV_GROUPS * HEAD_DIM
MOBA_WIDTH = MOBA_HEADS * HEAD_DIM
OFF_KV = NSA_WIDTH
OFF_GN = OFF_KV + 6 * KV_WIDTH
OFF_M = OFF_GN + 3 * NSA_HEADS
OFF_GM = OFF_M + 3 * MOBA_WIDTH
IN_WIDTH = OFF_GM + 2 * D_MODEL

LANES = 128
SUBLANES = 8
KEY_BLOCK = 256
Q_TILE = 256
V_ROWS = 80
K_AUG = 2 * LANES
NEG = -1e30
SCALE = float(HEAD_DIM ** -0.5 * np.log2(np.e))
MOBA_HEADS_PER_STEP = 8
GATE_ROWS = 16
FF_CHUNK = 1024
VMEM_LIMIT = 56 * 1024 * 1024

A_KC, A_KSW, A_KM, A_GM = 0, 256, 512, 1024
A_WIDTH = A_GM + 2 * D_MODEL
B_QN, B_QM, B_V, B_VM, B_GN = 0, 512, 1024, 1280, 1792
B_ROWS = B_GN + NSA_KV_GROUPS * GATE_ROWS


def _dot(a, b):
    return jnp.dot(a, b, preferred_element_type=F32)


def _dot_nt(a, b):
    return lax.dot_general(a, b, (((1,), (1,)), ((), ())), preferred_element_type=F32)


def _sigmoid(y):
    return 1.0 / (1.0 + jnp.exp(-y))


def _in_proj_body(x_ref, g_ref, wa_ref, wb_ref, c_ref, s1_ref, s2_ref, ct_ref, st_ref,
                  qt_ref, qmt_ref, gnt_ref, kcx_ref, vcx_ref, ke_ref, kw_ref, vts_ref, vtw_ref,
                  kem_ref, vtm_ref, kmean_ref, gm_ref, h_ref, *, tiles_per_seq):
    tm = x_ref.shape[0]
    it = pl.program_id(0) % tiles_per_seq
    x = x_ref[...]
    ms = jnp.mean(x * x, axis=-1, keepdims=True)
    h_ref[...] = (x * lax.rsqrt(ms + NORM_EPS) * g_ref[...]).astype(BF16)

    cos = c_ref[...]
    sin_hi = s1_ref[...]
    sin_lo = s2_ref[...]

    def proj(c0, width):
        return _dot(h_ref[...], wa_ref[:, c0:c0 + width])

    def rope(yc):
        return (yc * cos + pltpu.roll(yc, LANES - ROT_DIM // 2, 1) * sin_hi
                + pltpu.roll(yc, ROT_DIM // 2, 1) * sin_lo)

    y = proj(A_KC, 2 * KV_WIDTH)
    kcx_ref[...] = y[:, 0:KV_WIDTH]
    vcx_ref[...] = y[:, KV_WIDTH:2 * KV_WIDTH]

    row = lax.broadcasted_iota(jnp.int32, (tm, 1), 0)
    lane = lax.broadcasted_iota(jnp.int32, (1, LANES), 1)
    y = proj(A_KSW, 2 * KV_WIDTH)
    ke_ref[:, 0:LANES] = rope(y[:, 0:LANES]).astype(BF16)
    sel_blk = (it * tm + row) >> (SEL_BLOCK.bit_length() - 1)
    ke_ref[:, LANES:K_AUG] = jnp.where(sel_blk == lane, 1.0, 0.0).astype(BF16)
    kw_ref[...] = rope(y[:, LANES:2 * LANES]).astype(BF16)

    y = proj(A_KM, MOBA_WIDTH)
    own_blk = jnp.where(lane == it, 1.0, 0.0).astype(BF16)
    sums = []
    for j in range(MOBA_WIDTH // LANES):
        kr = rope(y[:, j * LANES:(j + 1) * LANES])
        sums.append(jnp.mean(kr, axis=0, keepdims=True))
        kem_ref[:, j * K_AUG:j * K_AUG + LANES] = kr.astype(BF16)
        kem_ref[:, j * K_AUG + LANES:(j + 1) * K_AUG] = jnp.broadcast_to(own_blk, (tm, LANES))
    kmean_ref[...] = jnp.broadcast_to(jnp.concatenate(sums, axis=1), kmean_ref.shape)

    for part in range(4):
        w = 2 * D_MODEL // 4
        gm_ref[:, part * w:(part + 1) * w] = _sigmoid(proj(A_GM + part * w, w)).astype(BF16)

    cos_t = ct_ref[...]
    sin_t = st_ref[...]

    def proj_t(r0, rows):
        return _dot_nt(wb_ref[r0:r0 + rows, :], h_ref[...])

    def rope_t(yt):
        half = ROT_DIM // 2
        out = []
        for hd in range(yt.shape[0] // HEAD_DIM):
            a = yt[hd * HEAD_DIM:hd * HEAD_DIM + half, :]
            b = yt[hd * HEAD_DIM + half:hd * HEAD_DIM + ROT_DIM, :]
            out += [a * cos_t - b * sin_t, a * sin_t + b * cos_t,
                    yt[hd * HEAD_DIM + ROT_DIM:(hd + 1) * HEAD_DIM, :]]
        return jnp.concatenate(out, axis=0)

    y = proj_t(B_QN, NSA_WIDTH)
    qt_ref[0:NSA_WIDTH, :] = (y * SCALE).astype(BF16)
    qt_ref[NSA_WIDTH:2 * NSA_WIDTH, :] = (rope_t(y) * SCALE).astype(BF16)
    qmt_ref[...] = rope_t(proj_t(B_QM, MOBA_WIDTH))

    ones_row = jnp.where(lax.broadcasted_iota(jnp.int32, (V_ROWS - HEAD_DIM, tm), 0) == 0, 1.0, 0.0).astype(BF16)
    y = proj_t(B_V, 4 * HEAD_DIM)
    for g in range(NSA_KV_GROUPS):
        vts_ref[g, 0:HEAD_DIM, :] = y[g * HEAD_DIM:(g + 1) * HEAD_DIM, :].astype(BF16)
        vts_ref[g, HEAD_DIM:V_ROWS, :] = ones_row
        vtw_ref[g, 0:HEAD_DIM, :] = y[(2 + g) * HEAD_DIM:(3 + g) * HEAD_DIM, :].astype(BF16)
        vtw_ref[g, HEAD_DIM:V_ROWS, :] = ones_row
    y = proj_t(B_VM, MOBA_WIDTH)
    for hd in range(MOBA_HEADS):
        vtm_ref[hd, 0:HEAD_DIM, :] = y[hd * HEAD_DIM:(hd + 1) * HEAD_DIM, :].astype(BF16)
        vtm_ref[hd, HEAD_DIM:V_ROWS, :] = ones_row
    y = _sigmoid(proj_t(B_GN, NSA_KV_GROUPS * GATE_ROWS))
    for g in range(NSA_KV_GROUPS):
        gnt_ref[g] = y[g * GATE_ROWS:(g + 1) * GATE_ROWS, :]


def _in_proj(x2, g1, wa, wb, tabs, batch, seq):
    m = x2.shape[0]
    tm = KEY_BLOCK
    nt = seq // tm
    row = lambda i: (i, 0)
    const = lambda i: (0, 0)
    tab = lambda i: (i % nt, 0)
    tab_t = lambda i: (0, i % nt)
    feat = lambda i: (i // nt, 0, i % nt)
    blk4 = lambda i: (i // nt, i % nt, 0, 0)
    blk5 = lambda i: (i // nt, 0, i % nt, 0, 0)
    g = NSA_KV_GROUPS
    outs = [
        (jax.ShapeDtypeStruct((batch, 2 * NSA_WIDTH, seq), BF16), pl.BlockSpec((None, 2 * NSA_WIDTH, tm), feat)),
        (jax.ShapeDtypeStruct((batch, MOBA_WIDTH, seq), F32), pl.BlockSpec((None, MOBA_WIDTH, tm), feat)),
        (jax.ShapeDtypeStruct((batch, g, GATE_ROWS, seq), F32),
         pl.BlockSpec((None, g, GATE_ROWS, tm), lambda i: (i // nt, 0, 0, i % nt))),
        (jax.ShapeDtypeStruct((m, KV_WIDTH), F32), pl.BlockSpec((tm, KV_WIDTH), row)),
        (jax.ShapeDtypeStruct((m, KV_WIDTH), F32), pl.BlockSpec((tm, KV_WIDTH), row)),
        (jax.ShapeDtypeStruct((batch, nt, tm, K_AUG), BF16), pl.BlockSpec((None, None, tm, K_AUG), blk4)),
        (jax.ShapeDtypeStruct((batch, nt, tm, LANES), BF16), pl.BlockSpec((None, None, tm, LANES), blk4)),
        (jax.ShapeDtypeStruct((batch, g, nt, V_ROWS, tm), BF16), pl.BlockSpec((None, g, None, V_ROWS, tm), blk5)),
        (jax.ShapeDtypeStruct((batch, g, nt, V_ROWS, tm), BF16), pl.BlockSpec((None, g, None, V_ROWS, tm), blk5)),
        (jax.ShapeDtypeStruct((batch, nt, tm, (MOBA_HEADS // 2) * K_AUG), BF16),
         pl.BlockSpec((None, None, tm, (MOBA_HEADS // 2) * K_AUG), blk4)),
        (jax.ShapeDtypeStruct((batch, MOBA_HEADS, nt, V_ROWS, tm), BF16),
         pl.BlockSpec((None, MOBA_HEADS, None, V_ROWS, tm), blk5)),
        (jax.ShapeDtypeStruct((batch, nt * SUBLANES, MOBA_WIDTH), F32),
         pl.BlockSpec((None, SUBLANES, MOBA_WIDTH), lambda i: (i // nt, i % nt, 0))),
        (jax.ShapeDtypeStruct((m, 2 * D_MODEL), BF16), pl.BlockSpec((tm, 2 * D_MODEL), row)),
    ]
    return pl.pallas_call(
        functools.partial(_in_proj_body, tiles_per_seq=nt),
        grid=(m // tm,),
        in_specs=[pl.BlockSpec((tm, D_MODEL), row),
                  pl.BlockSpec((1, D_MODEL), const),
                  pl.BlockSpec((D_MODEL, A_WIDTH), const),
                  pl.BlockSpec((B_ROWS, D_MODEL), const),
                  pl.BlockSpec((tm, LANES), tab),
                  pl.BlockSpec((tm, LANES), tab),
                  pl.BlockSpec((tm, LANES), tab),
                  pl.BlockSpec((SUBLANES, tm), tab_t),
                  pl.BlockSpec((SUBLANES, tm), tab_t)],
        out_specs=[o[1] for o in outs],
        out_shape=[o[0] for o in outs],
        scratch_shapes=[pltpu.VMEM((tm, D_MODEL), BF16)],
        compiler_params=pltpu.CompilerParams(dimension_semantics=("parallel",),
                                             vmem_limit_bytes=VMEM_LIMIT),
        name="in_proj",
    )(x2, g1, wa, wb, *tabs)


def _gelu_tanh(x):
    return 0.5 * x * (1.0 + jnp.tanh(np.sqrt(2.0 / np.pi) * (x + 0.044715 * (x * x * x))))


def _compress_body(kx_ref, vx_ref, pek_ref, pev_ref, kw1_ref, vw1_ref, kw2_ref, vw2t_ref,
                   kc_ref, vct_ref):
    ncp = kc_ref.shape[0]

    def hidden(x_ref, pe_ref, w1_ref):
        ya = jnp.zeros((ncp, LANES), F32)
        yb = jnp.zeros((ncp, LANES), F32)
        pe = jnp.zeros((SUBLANES, LANES), F32)
        for r in range(CMP_STRIDE):
            xr = x_ref[pl.ds(r, ncp, stride=CMP_STRIDE), :].astype(BF16)
            ya = ya + _dot(xr, w1_ref[r])
            yb = yb + _dot(xr, w1_ref[CMP_STRIDE + r])
        for l in range(CMP_BLOCK):
            pe = pe + _dot(pe_ref[l], w1_ref[l])
        return _gelu_tanh(ya + pltpu.roll(yb, ncp - 1, 0) + pe[0:1, :])

    hk = hidden(kx_ref, pek_ref, kw1_ref)
    kc_ref[...] = _dot(hk.astype(BF16), kw2_ref[...]).astype(BF16)
    hv = hidden(vx_ref, pev_ref, vw1_ref)
    vct = _dot_nt(vw2t_ref[...], hv.astype(BF16)).astype(BF16)
    ones_row = jnp.where(lax.broadcasted_iota(jnp.int32, (V_ROWS - HEAD_DIM, ncp), 0) == 0, 1.0, 0.0).astype(BF16)
    for g in range(NSA_KV_GROUPS):
        vct_ref[g, 0:HEAD_DIM, :] = vct[g * HEAD_DIM:(g + 1) * HEAD_DIM, :]
        vct_ref[g, HEAD_DIM:V_ROWS, :] = ones_row


def _compress(kx, vx, pek, pev, kw1, vw1, kw2, vw2t):
    b, seq, _ = kx.shape
    ncp = seq // CMP_STRIDE
    g = NSA_KV_GROUPS
    tok = lambda bi: (bi, 0, 0)
    const3 = lambda bi: (0, 0, 0)
    const2 = lambda bi: (0, 0)
    return pl.pallas_call(
        _compress_body,
        grid=(b,),
        in_specs=[pl.BlockSpec((None, seq, LANES), tok),
                  pl.BlockSpec((None, seq, LANES), tok),
                  pl.BlockSpec((CMP_BLOCK, SUBLANES, LANES), const3),
                  pl.BlockSpec((CMP_BLOCK, SUBLANES, LANES), const3),
                  pl.BlockSpec((CMP_BLOCK, LANES, LANES), const3),
                  pl.BlockSpec((CMP_BLOCK, LANES, LANES), const3),
                  pl.BlockSpec((LANES, LANES), const2),
                  pl.BlockSpec((LANES, LANES), const2)],
        out_specs=[pl.BlockSpec((None, ncp, LANES), tok),
                   pl.BlockSpec((None, g, V_ROWS, ncp), lambda bi: (bi, 0, 0, 0))],
        out_shape=[jax.ShapeDtypeStruct((b, ncp, LANES), BF16),
                   jax.ShapeDtypeStruct((b, g, V_ROWS, ncp), BF16)],
        compiler_params=pltpu.CompilerParams(dimension_semantics=("parallel",)),
        name="compress",
    )(kx, vx, pek, pev, kw1, vw1, kw2, vw2t)


def _rank_rows(score, rank, k0, k1):
    rows = score.shape[0]
    j = lax.broadcasted_iota(jnp.int32, (SUBLANES, 1), 0)
    blocks = [score[b:b + SUBLANES, :] for b in range(0, rows, SUBLANES)]
    for k in range(k0, k1):
        rk = score[k:k + 1, :]
        before = lambda sb: jnp.where(rk > sb, 1.0, 0.0)
        unless = lambda sb: jnp.where(sb > rk, 0.0, 1.0)
        counts = []
        for b, sb in zip(range(0, rows, SUBLANES), blocks):
            if b + SUBLANES <= k:
                counts.append(before(sb))
            elif b > k:
                counts.append(unless(sb))
            else:
                counts.append(jnp.where(j + b > k, unless(sb), before(sb)))
        rank = rank + jnp.concatenate(counts, axis=0)
    return rank


def _topk_mask(score, n_rows, k_top):
    return _rank_rows(score, jnp.zeros(score.shape, F32), 0, n_rows) < k_top


def _finish(acc):
    return acc[0:HEAD_DIM, :] / acc[HEAD_DIM:HEAD_DIM + 1, :]


def _softmax_reset(m_ref, acc_ref):
    m_ref[...] = jnp.full(m_ref.shape, NEG, F32)
    acc_ref[...] = jnp.zeros(acc_ref.shape, F32)


def _softmax_update(m_ref, acc_ref, sb, values):
    m_old = m_ref[...]
    m_new = jnp.maximum(m_old, jnp.max(sb, axis=0, keepdims=True))
    pb = jnp.exp2(sb - m_new).astype(BF16)
    acc_ref[...] = jnp.exp2(m_old - m_new) * acc_ref[...] + values(pb)
    m_ref[...] = m_new


def _online_blocks(i, s_buf, m_ref, acc_ref, scores, values, causal):
    def park(slot, kb):
        s_buf[slot] = scores(jnp.maximum(kb, 0), kb >= 0)

    def update(sb, kb):
        _softmax_update(m_ref, acc_ref, sb, lambda pb: values(jnp.maximum(kb, 0), pb))

    _softmax_reset(m_ref, acc_ref)
    first = -((i + 1) & 1)
    n_pairs = (i + 2) // 2
    park(0, first)

    @pl.loop(0, n_pairs - 1)
    def _(p):
        ka = first + 2 * p
        park(1, ka + 1)
        update(s_buf[0], ka)
        park(0, ka + 2)
        update(s_buf[1], ka + 1)

    park(1, i)
    update(s_buf[0], i - 1)
    update(jnp.where(causal, s_buf[1], NEG), i)
    return acc_ref[...]


def _online_blocks_diag_first(i, s_buf, m_ref, acc_ref, diag_scores, select, values, causal):
    def park(slot, x):
        kb = x - 1
        s_buf[slot] = scores(jnp.clip(kb, 0, jnp.maximum(i - 1, 0)), kb < i)

    def update(sb, kb):
        _softmax_update(m_ref, acc_ref, sb, lambda pb: values(kb, pb))

    past = lambda x: jnp.clip(x - 1, 0, jnp.maximum(i - 1, 0))
    _softmax_reset(m_ref, acc_ref)
    s_buf[0] = diag_scores()
    scores = select()
    park(1, 1)
    update(jnp.where(causal, s_buf[0], NEG), i)
    park(0, 2)
    update(s_buf[1], past(1))

    @pl.loop(1, (i + 2) // 2)
    def _(p):
        park(1, 2 * p + 1)
        update(s_buf[0], past(2 * p))
        park(0, 2 * p + 2)
        update(s_buf[1], past(2 * p + 1))

    return acc_ref[...]


def _nsa_body(q_ref, qr_ref, gn_ref, kc_ref, vct_ref, ke_ref, vts_ref, kw_ref, vtw_ref, ovt_ref,
              y_ref, s_buf, m_ref, acc_ref, *, n_sel_blocks):
    tq = q_ref.shape[1]
    rg = NSA_GROUP_SIZE
    g = pl.program_id(1)
    i = pl.program_id(2)
    t0 = i * tq
    head = lambda a, r: a[r * HEAD_DIM:(r + 1) * HEAD_DIM, :]
    q = q_ref[...]
    qr = qr_ref[...]
    tpos = t0 + lax.broadcasted_iota(jnp.int32, (1, tq), 1)
    tpos4 = jnp.concatenate([tpos] * rg, axis=1)

    zero = jnp.zeros((HEAD_DIM, tq), BF16)
    is_g0 = g == 0
    place = lambda a: jnp.concatenate([jnp.where(is_g0, a, zero), jnp.where(is_g0, zero, a)], axis=0)
    qc = jnp.concatenate([place(head(q, r)) for r in range(rg)], axis=1)
    qw = jnp.concatenate([place(head(qr, r)) for r in range(rg)], axis=1)

    causal = (lax.broadcasted_iota(jnp.int32, (KEY_BLOCK, 1), 0)
              <= (lax.broadcasted_iota(jnp.int32, (1, rg * tq), 1) & (tq - 1)))

    ncp = kc_ref.shape[0]
    i1 = jnp.maximum(i - 1, 0)
    i2 = jnp.maximum(i - 2, 0)
    n = lax.broadcasted_iota(jnp.int32, (ncp, 1), 0)
    s_buf[0, 0:ncp, :] = jnp.where((n * CMP_STRIDE + (CMP_BLOCK - 1)) <= tpos4, _dot(kc_ref[...], qc), NEG)
    s_buf[1] = _dot(kw_ref[i], qw)

    s = s_buf[0, 0:ncp, :]
    p = jnp.exp2(s - jnp.max(s, axis=0, keepdims=True))
    acc = _dot(vct_ref[...], p.astype(BF16))
    inv = jnp.where(tpos4 >= CMP_BLOCK - 1, 1.0 / jnp.maximum(acc[HEAD_DIM:HEAD_DIM + 1, :], 1e-30), 0.0)
    o_cmp = acc[0:HEAD_DIM, :] * inv

    ps = p[:, 0:tq] * inv[:, 0:tq]
    for r in range(1, rg):
        ps = ps + p[:, r * tq:(r + 1) * tq] * inv[:, r * tq:(r + 1) * tq]
    ps_hi = ps.astype(BF16)
    ps_lo = (ps - ps_hi.astype(F32)).astype(BF16)
    imp = _dot(ovt_ref[...], ps_hi) + _dot(ovt_ref[...], ps_lo)
    jb = lax.broadcasted_iota(jnp.int32, (imp.shape[0], 1), 0)
    cur = tpos >> (SEL_BLOCK.bit_length() - 1)
    forced = (jb == 0) | (jb == cur) | (jb == cur - 1)
    imp = jnp.where(forced, jnp.inf, imp)
    imp = jnp.where(jb <= cur, imp, -jnp.inf)

    s_buf[0] = _dot(kw_ref[i1], qw)
    _softmax_reset(m_ref, acc_ref)
    _softmax_update(m_ref, acc_ref, jnp.where(causal, s_buf[1], NEG), lambda pb: _dot(vtw_ref[i], pb))
    half = n_sel_blocks // 2
    rank = _rank_rows(imp, jnp.zeros(imp.shape, F32), 0, half)
    s_buf[1] = _dot(kw_ref[i2], qw)
    _softmax_update(m_ref, acc_ref, jnp.where(i >= 1, s_buf[0], NEG), lambda pb: _dot(vtw_ref[i1], pb))
    rank = _rank_rows(imp, rank, half, n_sel_blocks)
    _softmax_update(m_ref, acc_ref, jnp.where(jnp.logical_or(causal, i < 2), NEG, s_buf[1]),
                    lambda pb: _dot(vtw_ref[i2], pb))
    o_win = _finish(acc_ref[...])
    bias = jnp.where(rank < min(SEL_TOPN, n_sel_blocks), 0.0, NEG).astype(BF16)

    off = jnp.full(bias.shape, NEG, BF16)
    aug = lambda b_rows: jnp.concatenate(
        [jnp.concatenate([qw[:, r * tq:(r + 1) * tq], b_rows, zero], axis=0) for r in range(rg)], axis=1)
    qa = aug(bias)
    qa_off = aug(off)
    o_sel = _finish(_online_blocks(
        i, s_buf, m_ref, acc_ref,
        lambda kb, live: _dot(ke_ref[kb], jnp.where(live, qa, qa_off)),
        lambda kb, pb: _dot(vts_ref[kb], pb),
        causal))

    gates = gn_ref[...]
    for r in range(rg):
        c = slice(r * tq, (r + 1) * tq)
        y_ref[r * HEAD_DIM:(r + 1) * HEAD_DIM, :] = (
            gates[3 * r:3 * r + 1, :] * o_cmp[:, c]
            + gates[3 * r + 1:3 * r + 2, :] * o_sel[:, c]
            + gates[3 * r + 2:3 * r + 3, :] * o_win[:, c]).astype(y_ref.dtype)


def _nsa(qt, gnt, kc, vct, ke, vts, kw, vtw, ovt):
    b, _, seq = qt.shape
    g = NSA_KV_GROUPS
    nkb = seq // KEY_BLOCK
    ncp = kc.shape[1]
    tq = Q_TILE
    gw = NSA_GROUP_SIZE * HEAD_DIM
    per_group5 = lambda bi, gi, i: (bi, gi, 0, 0, 0)
    per_group4 = lambda bi, gi, i: (bi, gi, 0, 0)
    per_batch4 = lambda bi, gi, i: (bi, 0, 0, 0)
    return pl.pallas_call(
        functools.partial(_nsa_body, n_sel_blocks=seq // SEL_BLOCK),
        grid=(b, g, seq // tq),
        in_specs=[pl.BlockSpec((None, gw, tq), lambda bi, gi, i: (bi, gi, i)),
                  pl.BlockSpec((None, gw, tq), lambda bi, gi, i: (bi, g + gi, i)),
                  pl.BlockSpec((None, None, GATE_ROWS, tq), lambda bi, gi, i: (bi, gi, 0, i)),
                  pl.BlockSpec((None, ncp, LANES), lambda bi, gi, i: (bi, 0, 0)),
                  pl.BlockSpec((None, None, V_ROWS, ncp), per_group4),
                  pl.BlockSpec((None, nkb, KEY_BLOCK, K_AUG), per_batch4),
                  pl.BlockSpec((None, None, nkb, V_ROWS, KEY_BLOCK), per_group5),
                  pl.BlockSpec((None, nkb, KEY_BLOCK, LANES), per_batch4),
                  pl.BlockSpec((None, None, nkb, V_ROWS, KEY_BLOCK), per_group5),
                  pl.BlockSpec((HEAD_DIM, ncp), lambda bi, gi, i: (0, 0))],
        out_specs=pl.BlockSpec((None, gw, tq), lambda bi, gi, i: (bi, gi, i)),
        out_shape=jax.ShapeDtypeStruct((b, NSA_WIDTH, seq), BF16),
        scratch_shapes=[pltpu.VMEM((2, KEY_BLOCK, NSA_GROUP_SIZE * tq), F32),
                        pltpu.VMEM((1, NSA_GROUP_SIZE * tq), F32),
                        pltpu.VMEM((V_ROWS, NSA_GROUP_SIZE * tq), F32)],
        compiler_params=pltpu.CompilerParams(
            dimension_semantics=("parallel", "parallel", "arbitrary"), vmem_limit_bytes=VMEM_LIMIT),
        name="nsa",
    )(qt, qt, gnt, kc, vct, ke, vts, kw, vtw, ovt)


def _moba_body(q_ref, km_ref, ke_ref, vt_ref, y_ref, s_buf, m_ref, acc_ref, *, n_blocks):
    tq = q_ref.shape[1]
    nh = km_ref.shape[0]
    i = pl.program_id(2)
    q = q_ref[...] * SCALE
    qb = q.astype(BF16)
    head = lambda a, h: a[h * HEAD_DIM:(h + 1) * HEAD_DIM, :]
    zero = jnp.zeros((HEAD_DIM, tq), BF16)
    n_bias = km_ref.shape[1]
    pad = jnp.zeros((K_AUG - LANES - n_bias, tq), BF16)

    def aug(h, b_rows):
        pair = [head(qb, h), zero] if h % 2 == 0 else [zero, head(qb, h)]
        return jnp.concatenate(pair + [b_rows, pad], axis=0)

    def keys(kb, h):
        return ke_ref[kb, :, (h // 2) * K_AUG:(h // 2 + 1) * K_AUG]

    def diag_scores():
        open_rows = jnp.zeros((n_bias, tq), BF16)
        return jnp.concatenate([_dot(keys(i, h), aug(h, open_rows)) for h in range(nh)], axis=1)

    def select():
        q_lo = (q - qb.astype(F32)).astype(BF16)
        gates = []
        for h in range(nh):
            km = km_ref[h]
            km_hi = km.astype(BF16)
            km_lo = (km - km_hi.astype(F32)).astype(BF16)
            gates.append(_dot(km_hi, head(qb, h)) + (_dot(km_hi, head(q_lo, h)) + _dot(km_lo, head(qb, h))))
        gate = jnp.concatenate(gates, axis=1)
        jb = lax.broadcasted_iota(jnp.int32, (gate.shape[0], 1), 0)
        gate = jnp.where(jb < i, gate, -jnp.inf)
        sel = _topk_mask(gate, n_blocks, min(MOBA_TOPK, n_blocks)) & (gate > -jnp.inf)
        bias = jnp.where(sel, 0.0, NEG).astype(BF16)
        off = jnp.full((n_bias, tq), NEG, BF16)
        qa = [aug(h, bias[:, h * tq:(h + 1) * tq]) for h in range(nh)]
        qa_off = [aug(h, off) for h in range(nh)]
        return lambda kb, live: jnp.concatenate(
            [_dot(keys(kb, h), jnp.where(live, qa[h], qa_off[h])) for h in range(nh)], axis=1)

    def values(kb, pb):
        return jnp.concatenate([_dot(vt_ref[h, kb], pb[:, h * tq:(h + 1) * tq]) for h in range(nh)], axis=1)

    causal = (lax.broadcasted_iota(jnp.int32, (KEY_BLOCK, 1), 0)
              <= (lax.broadcasted_iota(jnp.int32, (1, nh * tq), 1) & (tq - 1)))
    o = _finish(_online_blocks_diag_first(i, s_buf, m_ref, acc_ref, diag_scores, select, values, causal))
    for h in range(nh):
        y_ref[h * HEAD_DIM:(h + 1) * HEAD_DIM, :] = o[:, h * tq:(h + 1) * tq].astype(y_ref.dtype)


def _moba(qmt, kmean, ke, vt):
    b, _, seq = qmt.shape
    nkb = seq // KEY_BLOCK
    tq = MOBA_BLOCK
    nh = MOBA_HEADS_PER_STEP
    return pl.pallas_call(
        functools.partial(_moba_body, n_blocks=nkb),
        grid=(b, MOBA_HEADS // nh, seq // tq),
        in_specs=[pl.BlockSpec((None, nh * HEAD_DIM, tq), lambda bi, hi, i: (bi, hi, i)),
                  pl.BlockSpec((None, nh, 16, HEAD_DIM), lambda bi, hi, i: (bi, hi, 0, 0)),
                  pl.BlockSpec((None, nkb, KEY_BLOCK, (nh // 2) * K_AUG), lambda bi, hi, i: (bi, 0, 0, hi)),
                  pl.BlockSpec((None, nh, nkb, V_ROWS, KEY_BLOCK), lambda bi, hi, i: (bi, hi, 0, 0, 0))],
        out_specs=pl.BlockSpec((None, nh * HEAD_DIM, tq), lambda bi, hi, i: (bi, hi, i)),
        out_shape=jax.ShapeDtypeStruct((b, MOBA_WIDTH, seq), BF16),
        scratch_shapes=[pltpu.VMEM((2, KEY_BLOCK, nh * tq), F32),
                        pltpu.VMEM((1, nh * tq), F32),
                        pltpu.VMEM((V_ROWS, nh * tq), F32)],
        compiler_params=pltpu.CompilerParams(
            dimension_semantics=("parallel", "parallel", "arbitrary"), vmem_limit_bytes=VMEM_LIMIT),
        name="moba",
    )(qmt, kmean, ke, vt)


def _merge_ffn_body(x_ref, yn_ref, ym_ref, gm_ref, wun_ref, wum_ref, wo_ref, g2_ref,
                    w1_ref, w2_ref, gf_ref, o_ref):
    tn = (((0,), (0,)), ((), ()))
    a = lax.dot_general(yn_ref[...], wun_ref[...], tn, preferred_element_type=F32)
    c = lax.dot_general(ym_ref[...], wum_ref[...], tn, preferred_element_type=F32)
    mixed = gm_ref[:, 0:D_MODEL] * a + gm_ref[:, D_MODEL:2 * D_MODEL] * c
    x1 = x_ref[...] + _dot(mixed.astype(BF16), wo_ref[...])
    ms = jnp.mean(x1 * x1, axis=-1, keepdims=True)
    h2 = (x1 * lax.rsqrt(ms + NORM_EPS) * g2_ref[...]).astype(BF16)
    x2 = x1
    for k in range(D_FF // FF_CHUNK):
        u = jnp.maximum(_dot(h2, w1_ref[:, k * FF_CHUNK:(k + 1) * FF_CHUNK]), 0.0)
        x2 = x2 + _dot((u * u).astype(BF16), w2_ref[k * FF_CHUNK:(k + 1) * FF_CHUNK, :])
    ms = jnp.mean(x2 * x2, axis=-1, keepdims=True)
    o_ref[...] = x2 * lax.rsqrt(ms + NORM_EPS) * gf_ref[...]


def _merge_ffn(x2, ynt, ymt, gm, wun, wum, wo, g2, w1, w2, gf):
    m = x2.shape[0]
    seq = ynt.shape[2]
    tm = 512
    nt = seq // tm
    row = lambda i: (i, 0)
    const = lambda i: (0, 0)
    feat = lambda i: (i // nt, 0, i % nt)
    resident = lambda shape: pl.BlockSpec(shape, const, pipeline_mode=pl.Buffered(1))
    return pl.pallas_call(
        _merge_ffn_body,
        grid=(m // tm,),
        in_specs=[pl.BlockSpec((tm, D_MODEL), row),
                  pl.BlockSpec((None, NSA_WIDTH, tm), feat),
                  pl.BlockSpec((None, MOBA_WIDTH, tm), feat),
                  pl.BlockSpec((tm, 2 * D_MODEL), row),
                  resident((NSA_WIDTH, D_MODEL)),
                  resident((MOBA_WIDTH, D_MODEL)),
                  resident((D_MODEL, D_MODEL)),
                  pl.BlockSpec((1, D_MODEL), const),
                  resident((D_MODEL, D_FF)),
                  resident((D_FF, D_MODEL)),
                  pl.BlockSpec((1, D_MODEL), const)],
        out_specs=pl.BlockSpec((tm, D_MODEL), row),
        out_shape=jax.ShapeDtypeStruct((m, D_MODEL), F32),
        compiler_params=pltpu.CompilerParams(dimension_semantics=("parallel",),
                                             vmem_limit_bytes=VMEM_LIMIT),
        name="merge_ffn",
    )(x2, ynt, ymt, gm, wun, wum, wo, g2, w1, w2, gf)


def _rope_tables(seq):
    half = ROT_DIM // 2
    inv = ROPE_THETA ** (-jnp.arange(0, ROT_DIM, 2, dtype=F32) / ROT_DIM)
    ang = jnp.arange(seq, dtype=F32)[:, None] * inv[None, :]
    cos, sin = jnp.cos(ang), jnp.sin(ang)
    pad = HEAD_DIM - ROT_DIM
    one_head = lambda a, b, fill: jnp.concatenate([a, b, jnp.full((seq, pad), fill, F32)], axis=1)
    zeros = jnp.zeros((seq, half), F32)
    two = lambda a: jnp.concatenate([a, a], axis=1)
    return (two(one_head(cos, cos, 1.0)),
            two(one_head(-sin, zeros, 0.0)),
            two(one_head(zeros, sin, 0.0)),
            cos.T, sin.T)


def _overlap_t(ncp, nsb):
    i = np.arange(ncp)[None, :]
    j = np.arange(HEAD_DIM)[:, None]
    start = i * CMP_STRIDE
    end = start + CMP_BLOCK - 1
    ov = (end >= j * SEL_BLOCK) & (start <= j * SEL_BLOCK + SEL_BLOCK - 1) & (i < ncp - 1) & (j < nsb)
    return jnp.asarray(ov.astype(np.float32), dtype=BF16)


def _block_diag(w):
    z = jnp.zeros_like(w)
    return jnp.concatenate([jnp.concatenate([w, z], axis=-1), jnp.concatenate([z, w], axis=-1)], axis=-2)


def _projection_weights(w):
    kv = lambda j: w[:, OFF_KV + j * KV_WIDTH:OFF_KV + (j + 1) * KV_WIDTH]
    mo = lambda j: w[:, OFF_M + j * MOBA_WIDTH:OFF_M + (j + 1) * MOBA_WIDTH]
    wa = jnp.concatenate([kv(0), kv(1), kv(2), kv(4), mo(1), w[:, OFF_GM:]], axis=1)
    gn = w[:, OFF_GN:OFF_M].T.reshape(NSA_KV_GROUPS, 3 * NSA_GROUP_SIZE, D_MODEL)
    gn = jnp.concatenate([gn, jnp.zeros((NSA_KV_GROUPS, GATE_ROWS - 3 * NSA_GROUP_SIZE, D_MODEL), w.dtype)], axis=1)
    wb = jnp.concatenate([w[:, :OFF_KV].T, mo(0).T, kv(3).T, kv(5).T, mo(2).T,
                          gn.reshape(NSA_KV_GROUPS * GATE_ROWS, D_MODEL)], axis=0)
    return wa.astype(BF16), wb.astype(BF16)


def _mixers(x, norm1_g, w_in, cmp_pe_k, cmp_pe_v, cmp_k_w1, cmp_k_w2, cmp_v_w1, cmp_v_w2):
    b, seq, _ = x.shape
    g, d = NSA_KV_GROUPS, HEAD_DIM
    nkb = seq // KEY_BLOCK
    ncp = seq // CMP_STRIDE
    nsb = seq // SEL_BLOCK
    layer = 0

    wa, wb = _projection_weights(w_in[layer])
    x2 = x.reshape(b * seq, D_MODEL)
    (qt, qmt, gnt, kcx, vcx, ke, kw, vts, vtw, kem, vtm, kmean8, ogm) = _in_proj(
        x2, norm1_g[layer][None, :], wa, wb, _rope_tables(seq), b, seq)

    per_group = lambda a: jnp.concatenate([a] * g, axis=-1)
    pe_rows = lambda pe: jnp.broadcast_to(per_group(pe)[:, None, :], (CMP_BLOCK, SUBLANES, LANES)).astype(BF16)
    w1_blocks = lambda w1: _block_diag(w1.reshape(CMP_BLOCK, d, d)).astype(BF16)
    kc, vct = _compress(
        kcx.reshape(b, seq, KV_WIDTH), vcx.reshape(b, seq, KV_WIDTH),
        pe_rows(cmp_pe_k[layer]), pe_rows(cmp_pe_v[layer]),
        w1_blocks(cmp_k_w1[layer]), w1_blocks(cmp_v_w1[layer]),
        _block_diag(cmp_k_w2[layer]).astype(BF16), _block_diag(cmp_v_w2[layer].T).astype(BF16))
    ynt = _nsa(qt, gnt, kc, vct, ke, vts, kw, vtw, _overlap_t(ncp, nsb))

    kmean = kmean8[:, ::SUBLANES, :].reshape(b, nkb, MOBA_HEADS, d)
    kmean = jnp.swapaxes(kmean, 1, 2)
    kmean = jnp.concatenate([kmean, jnp.zeros((b, MOBA_HEADS, 16 - nkb, d), F32)], axis=2)
    ymt = _moba(qmt, kmean, kem, vtm)
    return x2, ynt, ymt, ogm


def kernel(x, norm1_g, w_in, cmp_pe_k, cmp_pe_v, cmp_k_w1, cmp_k_w2, cmp_v_w1, cmp_v_w2,
           w_up_nsa, w_up_moba, w_out, norm2_g, w_ff1, w_ff2, norm_f_g):
    b, seq, _ = x.shape
    layer = 0
    x2, ynt, ymt, ogm = _mixers(x, norm1_g, w_in, cmp_pe_k, cmp_pe_v, cmp_k_w1, cmp_k_w2, cmp_v_w1, cmp_v_w2)
    out = _merge_ffn(x2, ynt, ymt, ogm, w_up_nsa[layer].astype(BF16), w_up_moba[layer].astype(BF16),
                     w_out[layer].astype(BF16), norm2_g[layer][None, :],
                     w_ff1[layer].astype(BF16), w_ff2[layer].astype(BF16), norm_f_g[None, :])
    return out.reshape(b, seq, D_MODEL)
```

```python
import functools

import numpy as np
import jax
import jax.numpy as jnp
from jax import lax
from jax.experimental import pallas as pl
from jax.experimental.pallas import tpu as pltpu

F32 = jnp.float32
BF16 = jnp.bfloat16

D_MODEL = 1024
HEAD_DIM = 64
ROT_DIM = HEAD_DIM // 4
ROPE_THETA = 500000.0
NORM_EPS = 1e-6

NSA_HEADS = 8
NSA_KV_GROUPS = 2
NSA_GROUP_SIZE = NSA_HEADS // NSA_KV_GROUPS
CMP_BLOCK = 32
CMP_STRIDE = 16
SEL_BLOCK = 64
SEL_TOPN = 16
WINDOW = 512

MOBA_HEADS = 8
MOBA_BLOCK = 256
MOBA_TOPK = 3

D_FF = 4 * D_MODEL
NSA_WIDTH = NSA_HEADS * HEAD_DIM
KV_WIDTH = NSA_KV_GROUPS * HEAD_DIM
MOBA_WIDTH = MOBA_HEADS * HEAD_DIM
OFF_KV = NSA_WIDTH
OFF_GN = OFF_KV + 6 * KV_WIDTH
OFF_M = OFF_GN + 3 * NSA_HEADS
OFF_GM = OFF_M + 3 * MOBA_WIDTH
IN_WIDTH = OFF_GM + 2 * D_MODEL

LANES = 128
SUBLANES = 8
KEY_BLOCK = 256
Q_TILE = 256
V_ROWS = 80
K_AUG = 2 * LANES
NEG = -1e30
SCALE = float(HEAD_DIM ** -0.5 * np.log2(np.e))
MOBA_HEADS_PER_STEP = 8
GATE_ROWS = 16
FF_CHUNK = 1024
VMEM_LIMIT = 56 * 1024 * 1024

A_KC, A_KSW, A_KM, A_GM = 0, 256, 512, 1024
A_WIDTH = A_GM + 2 * D_MODEL
B_QN, B_QM, B_V, B_VM, B_GN = 0, 512, 1024, 1280, 1792
B_ROWS = B_GN + NSA_KV_GROUPS * GATE_ROWS


def _dot(a, b):
    return jnp.dot(a, b, preferred_element_type=F32)


def _dot_nt(a, b):
    return lax.dot_general(a, b, (((1,), (1,)), ((), ())), preferred_element_type=F32)


def _sigmoid(y):
    return 1.0 / (1.0 + jnp.exp(-y))


def _in_proj_body(x_ref, g_ref, wa_ref, wb_ref, c_ref, s1_ref, s2_ref, ct_ref, st_ref,
                  qt_ref, qmt_ref, gnt_ref, kcx_ref, vcx_ref, ke_ref, kw_ref, vts_ref, vtw_ref,
                  kem_ref, vtm_ref, kmean_ref, gm_ref, h_ref, *, tiles_per_seq):
    tm = x_ref.shape[0]
    it = pl.program_id(0) % tiles_per_seq
    x = x_ref[...]
    ms = jnp.mean(x * x, axis=-1, keepdims=True)
    h_ref[...] = (x * lax.rsqrt(ms + NORM_EPS) * g_ref[...]).astype(BF16)

    cos = c_ref[...]
    sin_hi = s1_ref[...]
    sin_lo = s2_ref[...]

    def proj(c0, width):
        return _dot(h_ref[...], wa_ref[:, c0:c0 + width])

    def rope(yc):
        return (yc * cos + pltpu.roll(yc, LANES - ROT_DIM // 2, 1) * sin_hi
                + pltpu.roll(yc, ROT_DIM // 2, 1) * sin_lo)

    y = proj(A_KC, 2 * KV_WIDTH)
    kcx_ref[...] = y[:, 0:KV_WIDTH]
    vcx_ref[...] = y[:, KV_WIDTH:2 * KV_WIDTH]

    row = lax.broadcasted_iota(jnp.int32, (tm, 1), 0)
    lane = lax.broadcasted_iota(jnp.int32, (1, LANES), 1)
    y = proj(A_KSW, 2 * KV_WIDTH)
    ke_ref[:, 0:LANES] = rope(y[:, 0:LANES]).astype(BF16)
    sel_blk = (it * tm + row) >> (SEL_BLOCK.bit_length() - 1)
    ke_ref[:, LANES:K_AUG] = jnp.where(sel_blk == lane, 1.0, 0.0).astype(BF16)
    kw_ref[...] = rope(y[:, LANES:2 * LANES]).astype(BF16)

    y = proj(A_KM, MOBA_WIDTH)
    own_blk = jnp.where(lane == it, 1.0, 0.0).astype(BF16)
    sums = []
    for j in range(MOBA_WIDTH // LANES):
        kr = rope(y[:, j * LANES:(j + 1) * LANES])
        sums.append(jnp.mean(kr, axis=0, keepdims=True))
        kem_ref[:, j * K_AUG:j * K_AUG + LANES] = kr.astype(BF16)
        kem_ref[:, j * K_AUG + LANES:(j + 1) * K_AUG] = jnp.broadcast_to(own_blk, (tm, LANES))
    kmean_ref[...] = jnp.broadcast_to(jnp.concatenate(sums, axis=1), kmean_ref.shape)

    for part in range(4):
        w = 2 * D_MODEL // 4
        gm_ref[:, part * w:(part + 1) * w] = _sigmoid(proj(A_GM + part * w, w)).astype(BF16)

    cos_t = ct_ref[...]
    sin_t = st_ref[...]

    def proj_t(r0, rows):
        return _dot_nt(wb_ref[r0:r0 + rows, :], h_ref[...])

    def rope_t(yt):
        half = ROT_DIM // 2
        out = []
        for hd in range(yt.shape[0] // HEAD_DIM):
            a = yt[hd * HEAD_DIM:hd * HEAD_DIM + half, :]
            b = yt[hd * HEAD_DIM + half:hd * HEAD_DIM + ROT_DIM, :]
            out += [a * cos_t - b * sin_t, a * sin_t + b * cos_t,
                    yt[hd * HEAD_DIM + ROT_DIM:(hd + 1) * HEAD_DIM, :]]
        return jnp.concatenate(out, axis=0)

    y = proj_t(B_QN, NSA_WIDTH)
    qt_ref[0:NSA_WIDTH, :] = (y * SCALE).astype(BF16)
    qt_ref[NSA_WIDTH:2 * NSA_WIDTH, :] = (rope_t(y) * SCALE).astype(BF16)
    qmt_ref[...] = rope_t(proj_t(B_QM, MOBA_WIDTH))

    ones_row = jnp.where(lax.broadcasted_iota(jnp.int32, (V_ROWS - HEAD_DIM, tm), 0) == 0, 1.0, 0.0).astype(BF16)
    y = proj_t(B_V, 4 * HEAD_DIM)
    for g in range(NSA_KV_GROUPS):
        vts_ref[g, 0:HEAD_DIM, :] = y[g * HEAD_DIM:(g + 1) * HEAD_DIM, :].astype(BF16)
        vts_ref[g, HEAD_DIM:V_ROWS, :] = ones_row
        vtw_ref[g, 0:HEAD_DIM, :] = y[(2 + g) * HEAD_DIM:(3 + g) * HEAD_DIM, :].astype(BF16)
        vtw_ref[g, HEAD_DIM:V_ROWS, :] = ones_row
    y = proj_t(B_VM, MOBA_WIDTH)
    for hd in range(MOBA_HEADS):
        vtm_ref[hd, 0:HEAD_DIM, :] = y[hd * HEAD_DIM:(hd + 1) * HEAD_DIM, :].astype(BF16)
        vtm_ref[hd, HEAD_DIM:V_ROWS, :] = ones_row
    y = _sigmoid(proj_t(B_GN, NSA_KV_GROUPS * GATE_ROWS))
    for g in range(NSA_KV_GROUPS):
        gnt_ref[g] = y[g * GATE_ROWS:(g + 1) * GATE_ROWS, :]


def _in_proj(x2, g1, wa, wb, tabs, batch, seq):
    m = x2.shape[0]
    tm = KEY_BLOCK
    nt = seq // tm
    row = lambda i: (i, 0)
    const = lambda i: (0, 0)
    tab = lambda i: (i % nt, 0)
    tab_t = lambda i: (0, i % nt)
    feat = lambda i: (i // nt, 0, i % nt)
    blk4 = lambda i: (i // nt, i % nt, 0, 0)
    blk5 = lambda i: (i // nt, 0, i % nt, 0, 0)
    g = NSA_KV_GROUPS
    outs = [
        (jax.ShapeDtypeStruct((batch, 2 * NSA_WIDTH, seq), BF16), pl.BlockSpec((None, 2 * NSA_WIDTH, tm), feat)),
        (jax.ShapeDtypeStruct((batch, MOBA_WIDTH, seq), F32), pl.BlockSpec((None, MOBA_WIDTH, tm), feat)),
        (jax.ShapeDtypeStruct((batch, g, GATE_ROWS, seq), F32),
         pl.BlockSpec((None, g, GATE_ROWS, tm), lambda i: (i // nt, 0, 0, i % nt))),
        (jax.ShapeDtypeStruct((m, KV_WIDTH), F32), pl.BlockSpec((tm, KV_WIDTH), row)),
        (jax.ShapeDtypeStruct((m, KV_WIDTH), F32), pl.BlockSpec((tm, KV_WIDTH), row)),
        (jax.ShapeDtypeStruct((batch, nt, tm, K_AUG), BF16), pl.BlockSpec((None, None, tm, K_AUG), blk4)),
        (jax.ShapeDtypeStruct((batch, nt, tm, LANES), BF16), pl.BlockSpec((None, None, tm, LANES), blk4)),
        (jax.ShapeDtypeStruct((batch, g, nt, V_ROWS, tm), BF16), pl.BlockSpec((None, g, None, V_ROWS, tm), blk5)),
        (jax.ShapeDtypeStruct((batch, g, nt, V_ROWS, tm), BF16), pl.BlockSpec((None, g, None, V_ROWS, tm), blk5)),
        (jax.ShapeDtypeStruct((batch, nt, tm, (MOBA_HEADS // 2) * K_AUG), BF16),
         pl.BlockSpec((None, None, tm, (MOBA_HEADS // 2) * K_AUG), blk4)),
        (jax.ShapeDtypeStruct((batch, MOBA_HEADS, nt, V_ROWS, tm), BF16),
         pl.BlockSpec((None, MOBA_HEADS, None, V_ROWS, tm), blk5)),
        (jax.ShapeDtypeStruct((batch, nt * SUBLANES, MOBA_WIDTH), F32),
         pl.BlockSpec((None, SUBLANES, MOBA_WIDTH), lambda i: (i // nt, i % nt, 0))),
        (jax.ShapeDtypeStruct((m, 2 * D_MODEL), BF16), pl.BlockSpec((tm, 2 * D_MODEL), row)),
    ]
    return pl.pallas_call(
        functools.partial(_in_proj_body, tiles_per_seq=nt),
        grid=(m // tm,),
        in_specs=[pl.BlockSpec((tm, D_MODEL), row),
                  pl.BlockSpec((1, D_MODEL), const),
                  pl.BlockSpec((D_MODEL, A_WIDTH), const),
                  pl.BlockSpec((B_ROWS, D_MODEL), const),
                  pl.BlockSpec((tm, LANES), tab),
                  pl.BlockSpec((tm, LANES), tab),
                  pl.BlockSpec((tm, LANES), tab),
                  pl.BlockSpec((SUBLANES, tm), tab_t),
                  pl.BlockSpec((SUBLANES, tm), tab_t)],
        out_specs=[o[1] for o in outs],
        out_shape=[o[0] for o in outs],
        scratch_shapes=[pltpu.VMEM((tm, D_MODEL), BF16)],
        compiler_params=pltpu.CompilerParams(dimension_semantics=("parallel",),
                                             vmem_limit_bytes=VMEM_LIMIT),
        name="in_proj",
    )(x2, g1, wa, wb, *tabs)


def _gelu_tanh(x):
    return 0.5 * x * (1.0 + jnp.tanh(np.sqrt(2.0 / np.pi) * (x + 0.044715 * (x * x * x))))


def _compress_body(kx_ref, vx_ref, pek_ref, pev_ref, kw1_ref, vw1_ref, kw2_ref, vw2t_ref,
                   kc_ref, vct_ref):
    ncp = kc_ref.shape[0]

    def hidden(x_ref, pe_ref, w1_ref):
        ya = jnp.zeros((ncp, LANES), F32)
        yb = jnp.zeros((ncp, LANES), F32)
        pe = jnp.zeros((SUBLANES, LANES), F32)
        for r in range(CMP_STRIDE):
            xr = x_ref[pl.ds(r, ncp, stride=CMP_STRIDE), :].astype(BF16)
            ya = ya + _dot(xr, w1_ref[r])
            yb = yb + _dot(xr, w1_ref[CMP_STRIDE + r])
        for l in range(CMP_BLOCK):
            pe = pe + _dot(pe_ref[l], w1_ref[l])
        return _gelu_tanh(ya + pltpu.roll(yb, ncp - 1, 0) + pe[0:1, :])

    hk = hidden(kx_ref, pek_ref, kw1_ref)
    kc_ref[...] = _dot(hk.astype(BF16), kw2_ref[...]).astype(BF16)
    hv = hidden(vx_ref, pev_ref, vw1_ref)
    vct = _dot_nt(vw2t_ref[...], hv.astype(BF16)).astype(BF16)
    ones_row = jnp.where(lax.broadcasted_iota(jnp.int32, (V_ROWS - HEAD_DIM, ncp), 0) == 0, 1.0, 0.0).astype(BF16)
    for g in range(NSA_KV_GROUPS):
        vct_ref[g, 0:HEAD_DIM, :] = vct[g * HEAD_DIM:(g + 1) * HEAD_DIM, :]
        vct_ref[g, HEAD_DIM:V_ROWS, :] = ones_row


def _compress(kx, vx, pek, pev, kw1, vw1, kw2, vw2t):
    b, seq, _ = kx.shape
    ncp = seq // CMP_STRIDE
    g = NSA_KV_GROUPS
    tok = lambda bi: (bi, 0, 0)
    const3 = lambda bi: (0, 0, 0)
    const2 = lambda bi: (0, 0)
    return pl.pallas_call(
        _compress_body,
        grid=(b,),
        in_specs=[pl.BlockSpec((None, seq, LANES), tok),
                  pl.BlockSpec((None, seq, LANES), tok),
                  pl.BlockSpec((CMP_BLOCK, SUBLANES, LANES), const3),
                  pl.BlockSpec((CMP_BLOCK, SUBLANES, LANES), const3),
                  pl.BlockSpec((CMP_BLOCK, LANES, LANES), const3),
                  pl.BlockSpec((CMP_BLOCK, LANES, LANES), const3),
                  pl.BlockSpec((LANES, LANES), const2),
                  pl.BlockSpec((LANES, LANES), const2)],
        out_specs=[pl.BlockSpec((None, ncp, LANES), tok),
                   pl.BlockSpec((None, g, V_ROWS, ncp), lambda bi: (bi, 0, 0, 0))],
        out_shape=[jax.ShapeDtypeStruct((b, ncp, LANES), BF16),
                   jax.ShapeDtypeStruct((b, g, V_ROWS, ncp), BF16)],
        compiler_params=pltpu.CompilerParams(dimension_semantics=("parallel",)),
        name="compress",
    )(kx, vx, pek, pev, kw1, vw1, kw2, vw2t)


def _rank_rows(score, rank, k0, k1):
    rows = score.shape[0]
    j = lax.broadcasted_iota(jnp.int32, (SUBLANES, 1), 0)
    blocks = [score[b:b + SUBLANES, :] for b in range(0, rows, SUBLANES)]
    for k in range(k0, k1):
        rk = score[k:k + 1, :]
        before = lambda sb: jnp.where(rk > sb, 1.0, 0.0)
        unless = lambda sb: jnp.where(sb > rk, 0.0, 1.0)
        counts = []
        for b, sb in zip(range(0, rows, SUBLANES), blocks):
            if b + SUBLANES <= k:
                counts.append(before(sb))
            elif b > k:
                counts.append(unless(sb))
            else:
                counts.append(jnp.where(j + b > k, unless(sb), before(sb)))
        rank = rank + jnp.concatenate(counts, axis=0)
    return rank


def _topk_mask(score, n_rows, k_top):
    return _rank_rows(score, jnp.zeros(score.shape, F32), 0, n_rows) < k_top


def _finish(acc):
    return acc[0:HEAD_DIM, :] / acc[HEAD_DIM:HEAD_DIM + 1, :]


def _softmax_reset(m_ref, acc_ref):
    m_ref[...] = jnp.full(m_ref.shape, NEG, F32)
    acc_ref[...] = jnp.zeros(acc_ref.shape, F32)


def _softmax_update(m_ref, acc_ref, sb, values):
    m_old = m_ref[...]
    m_new = jnp.maximum(m_old, jnp.max(sb, axis=0, keepdims=True))
    pb = jnp.exp2(sb - m_new).astype(BF16)
    acc_ref[...] = jnp.exp2(m_old - m_new) * acc_ref[...] + values(pb)
    m_ref[...] = m_new


def _online_blocks(i, s_buf, m_ref, acc_ref, scores, values, causal):
    block = lambda kb: jnp.maximum(kb, 0)

    def park(slot, kb):
        s_buf[slot] = scores(block(kb), kb >= 0)

    def update(sb, kb):
        _softmax_update(m_ref, acc_ref, sb, lambda pb: values(block(kb), pb))

    _softmax_reset(m_ref, acc_ref)
    first = -((i + 1) & 1)
    n_pairs = (i + 2) // 2
    park(0, first)

    @pl.loop(0, n_pairs - 1)
    def _(p):
        ka = first + 2 * p
        park(1, ka + 1)
        update(s_buf[0], ka)
        park(0, ka + 2)
        update(s_buf[1], ka + 1)

    park(1, i)
    update(s_buf[0], i - 1)
    update(jnp.where(causal, s_buf[1], NEG), i)
    return acc_ref[...]


def _online_blocks_diag_first(i, s_buf, m_ref, acc_ref, diag_scores, select, values, causal):
    past = lambda x: jnp.clip(x - 1, 0, jnp.maximum(i - 1, 0))

    def park(slot, x):
        s_buf[slot] = scores(past(x), x - 1 < i)

    def update(sb, kb):
        _softmax_update(m_ref, acc_ref, sb, lambda pb: values(kb, pb))

    _softmax_reset(m_ref, acc_ref)
    s_buf[0] = diag_scores()
    scores = select()
    park(1, 1)
    update(jnp.where(causal, s_buf[0], NEG), i)
    park(0, 2)
    update(s_buf[1], past(1))

    @pl.loop(1, (i + 2) // 2)
    def _(p):
        park(1, 2 * p + 1)
        update(s_buf[0], past(2 * p))
        park(0, 2 * p + 2)
        update(s_buf[1], past(2 * p + 1))

    return acc_ref[...]


def _nsa_body(q_ref, qr_ref, gn_ref, kc_ref, vct_ref, ke_ref, vts_ref, kw_ref, vtw_ref, ovt_ref,
              y_ref, s_buf, m_ref, acc_ref, *, n_sel_blocks):
    tq = q_ref.shape[1]
    ng, rg = NSA_KV_GROUPS, NSA_GROUP_SIZE
    nh = ng * rg
    i = pl.program_id(1)
    t0 = i * tq
    head = lambda a, h: a[h * HEAD_DIM:(h + 1) * HEAD_DIM, :]
    cols = lambda a, h, n=1: a[:, h * tq:(h + n) * tq]
    q = q_ref[...]
    qr = qr_ref[...]
    tpos = t0 + lax.broadcasted_iota(jnp.int32, (1, tq), 1)
    tpos_n = jnp.concatenate([tpos] * nh, axis=1)

    zero = jnp.zeros((HEAD_DIM, tq), BF16)
    place = lambda h, a: jnp.concatenate([a, zero] if h < rg else [zero, a], axis=0)
    qc = jnp.concatenate([place(h, head(q, h)) for h in range(nh)], axis=1)
    qw = jnp.concatenate([place(h, head(qr, h)) for h in range(nh)], axis=1)
    per_group = lambda v_ref, kb, pb: jnp.concatenate(
        [_dot(v_ref[g, kb], cols(pb, g * rg, rg)) for g in range(ng)], axis=1)

    causal = (lax.broadcasted_iota(jnp.int32, (KEY_BLOCK, 1), 0)
              <= (lax.broadcasted_iota(jnp.int32, (1, nh * tq), 1) & (tq - 1)))

    ncp = kc_ref.shape[0]
    i1 = jnp.maximum(i - 1, 0)
    i2 = jnp.maximum(i - 2, 0)
    n = lax.broadcasted_iota(jnp.int32, (ncp, 1), 0)
    s_buf[0, 0:ncp, :] = jnp.where((n * CMP_STRIDE + (CMP_BLOCK - 1)) <= tpos_n, _dot(kc_ref[...], qc), NEG)
    s_buf[1] = _dot(kw_ref[i], qw)

    s = s_buf[0, 0:ncp, :]
    p = jnp.exp2(s - jnp.max(s, axis=0, keepdims=True))
    pb16 = p.astype(BF16)
    acc = jnp.concatenate([_dot(vct_ref[g], cols(pb16, g * rg, rg)) for g in range(ng)], axis=1)
    inv = jnp.where(tpos_n >= CMP_BLOCK - 1, 1.0 / jnp.maximum(acc[HEAD_DIM:HEAD_DIM + 1, :], 1e-30), 0.0)
    o_cmp = acc[0:HEAD_DIM, :] * inv

    pn = p * inv
    ps = jnp.concatenate([sum(cols(pn, g * rg + r) for r in range(1, rg)) + cols(pn, g * rg)
                          for g in range(ng)], axis=1)
    ps_hi = ps.astype(BF16)
    ps_lo = (ps - ps_hi.astype(F32)).astype(BF16)
    imp = _dot(ovt_ref[...], ps_hi) + _dot(ovt_ref[...], ps_lo)
    jb = lax.broadcasted_iota(jnp.int32, (imp.shape[0], 1), 0)
    cur = jnp.concatenate([tpos >> (SEL_BLOCK.bit_length() - 1)] * ng, axis=1)
    forced = (jb == 0) | (jb == cur) | (jb == cur - 1)
    imp = jnp.where(forced, jnp.inf, imp)
    imp = jnp.where(jb <= cur, imp, -jnp.inf)

    s_buf[0] = _dot(kw_ref[i1], qw)
    _softmax_reset(m_ref, acc_ref)
    _softmax_update(m_ref, acc_ref, jnp.where(causal, s_buf[1], NEG), lambda pb: per_group(vtw_ref, i, pb))
    half = n_sel_blocks // 2
    rank = _rank_rows(imp, jnp.zeros(imp.shape, F32), 0, half)
    s_buf[1] = _dot(kw_ref[i2], qw)
    _softmax_update(m_ref, acc_ref, jnp.where(i >= 1, s_buf[0], NEG), lambda pb: per_group(vtw_ref, i1, pb))
    rank = _rank_rows(imp, rank, half, n_sel_blocks)
    _softmax_update(m_ref, acc_ref, jnp.where(jnp.logical_or(causal, i < 2), NEG, s_buf[1]),
                    lambda pb: per_group(vtw_ref, i2, pb))
    o_win = _finish(acc_ref[...])
    bias = jnp.where(rank < min(SEL_TOPN, n_sel_blocks), 0.0, NEG).astype(BF16)

    off = jnp.full((bias.shape[0], tq), NEG, BF16)
    aug = lambda b_rows: jnp.concatenate(
        [jnp.concatenate([cols(qw, h), b_rows(h // rg), zero], axis=0) for h in range(nh)], axis=1)
    qa = aug(lambda g: cols(bias, g))
    qa_off = aug(lambda g: off)
    o_sel = _finish(_online_blocks(
        i, s_buf, m_ref, acc_ref,
        lambda kb, live: _dot(ke_ref[kb], jnp.where(live, qa, qa_off)),
        lambda kb, pb: per_group(vts_ref, kb, pb),
        causal))

    for h in range(nh):
        gates = gn_ref[h // rg]
        r = h % rg
        y_ref[h * HEAD_DIM:(h + 1) * HEAD_DIM, :] = (
            gates[3 * r:3 * r + 1, :] * cols(o_cmp, h)
            + gates[3 * r + 1:3 * r + 2, :] * cols(o_sel, h)
            + gates[3 * r + 2:3 * r + 3, :] * cols(o_win, h)).astype(y_ref.dtype)


def _nsa(qt, gnt, kc, vct, ke, vts, kw, vtw, ovt):
    b, _, seq = qt.shape
    g = NSA_KV_GROUPS
    nkb = seq // KEY_BLOCK
    ncp = kc.shape[1]
    tq = Q_TILE
    n_cols = NSA_HEADS * tq
    whole5 = lambda bi, i: (bi, 0, 0, 0, 0)
    whole4 = lambda bi, i: (bi, 0, 0, 0)
    return pl.pallas_call(
        functools.partial(_nsa_body, n_sel_blocks=seq // SEL_BLOCK),
        grid=(b, seq // tq),
        in_specs=[pl.BlockSpec((None, NSA_WIDTH, tq), lambda bi, i: (bi, 0, i)),
                  pl.BlockSpec((None, NSA_WIDTH, tq), lambda bi, i: (bi, 1, i)),
                  pl.BlockSpec((None, g, GATE_ROWS, tq), lambda bi, i: (bi, 0, 0, i)),
                  pl.BlockSpec((None, ncp, LANES), lambda bi, i: (bi, 0, 0)),
                  pl.BlockSpec((None, g, V_ROWS, ncp), whole4),
                  pl.BlockSpec((None, nkb, KEY_BLOCK, K_AUG), whole4),
                  pl.BlockSpec((None, g, nkb, V_ROWS, KEY_BLOCK), whole5),
                  pl.BlockSpec((None, nkb, KEY_BLOCK, LANES), whole4),
                  pl.BlockSpec((None, g, nkb, V_ROWS, KEY_BLOCK), whole5),
                  pl.BlockSpec((HEAD_DIM, ncp), lambda bi, i: (0, 0))],
        out_specs=pl.BlockSpec((None, NSA_WIDTH, tq), lambda bi, i: (bi, 0, i)),
        out_shape=jax.ShapeDtypeStruct((b, NSA_WIDTH, seq), BF16),
        scratch_shapes=[pltpu.VMEM((2, KEY_BLOCK, n_cols), F32),
                        pltpu.VMEM((1, n_cols), F32),
                        pltpu.VMEM((V_ROWS, n_cols), F32)],
        compiler_params=pltpu.CompilerParams(
            dimension_semantics=("parallel", "arbitrary"), vmem_limit_bytes=VMEM_LIMIT),
        name="nsa",
    )(qt, qt, gnt, kc, vct, ke, vts, kw, vtw, ovt)


def _moba_body(q_ref, km_ref, ke_ref, vt_ref, y_ref, s_buf, m_ref, acc_ref, *, n_blocks):
    tq = q_ref.shape[1]
    nh = km_ref.shape[0]
    i = pl.program_id(2)
    q = q_ref[...] * SCALE
    qb = q.astype(BF16)
    head = lambda a, h: a[h * HEAD_DIM:(h + 1) * HEAD_DIM, :]
    zero = jnp.zeros((HEAD_DIM, tq), BF16)
    n_bias = km_ref.shape[1]
    pad = jnp.zeros((K_AUG - LANES - n_bias, tq), BF16)

    def aug(h, b_rows):
        pair = [head(qb, h), zero] if h % 2 == 0 else [zero, head(qb, h)]
        return jnp.concatenate(pair + [b_rows, pad], axis=0)

    def keys(kb, h):
        return ke_ref[kb, :, (h // 2) * K_AUG:(h // 2 + 1) * K_AUG]

    def diag_scores():
        open_rows = jnp.zeros((n_bias, tq), BF16)
        return jnp.concatenate([_dot(keys(i, h), aug(h, open_rows)) for h in range(nh)], axis=1)

    def select():
        q_lo = (q - qb.astype(F32)).astype(BF16)
        gates = []
        for h in range(nh):
            km = km_ref[h]
            km_hi = km.astype(BF16)
            km_lo = (km - km_hi.astype(F32)).astype(BF16)
            gates.append(_dot(km_hi, head(qb, h)) + (_dot(km_hi, head(q_lo, h)) + _dot(km_lo, head(qb, h))))
        gate = jnp.concatenate(gates, axis=1)
        jb = lax.broadcasted_iota(jnp.int32, (gate.shape[0], 1), 0)
        gate = jnp.where(jb < i, gate, -jnp.inf)
        sel = _topk_mask(gate, n_blocks, min(MOBA_TOPK, n_blocks)) & (gate > -jnp.inf)
        bias = jnp.where(sel, 0.0, NEG).astype(BF16)
        off = jnp.full((n_bias, tq), NEG, BF16)
        qa = [aug(h, bias[:, h * tq:(h + 1) * tq]) for h in range(nh)]
        qa_off = [aug(h, off) for h in range(nh)]
        return lambda kb, live: jnp.concatenate(
            [_dot(keys(kb, h), jnp.where(live, qa[h], qa_off[h])) for h in range(nh)], axis=1)

    def values(kb, pb):
        return jnp.concatenate([_dot(vt_ref[h, kb], pb[:, h * tq:(h + 1) * tq]) for h in range(nh)], axis=1)

    causal = (lax.broadcasted_iota(jnp.int32, (KEY_BLOCK, 1), 0)
              <= (lax.broadcasted_iota(jnp.int32, (1, nh * tq), 1) & (tq - 1)))
    o = _finish(_online_blocks_diag_first(i, s_buf, m_ref, acc_ref, diag_scores, select, values, causal))
    for h in range(nh):
        y_ref[h * HEAD_DIM:(h + 1) * HEAD_DIM, :] = o[:, h * tq:(h + 1) * tq].astype(y_ref.dtype)


def _moba(qmt, kmean, ke, vt):
    b, _, seq = qmt.shape
    nkb = seq // KEY_BLOCK
    tq = MOBA_BLOCK
    nh = MOBA_HEADS_PER_STEP
    return pl.pallas_call(
        functools.partial(_moba_body, n_blocks=nkb),
        grid=(b, MOBA_HEADS // nh, seq // tq),
        in_specs=[pl.BlockSpec((None, nh * HEAD_DIM, tq), lambda bi, hi, i: (bi, hi, i)),
                  pl.BlockSpec((None, nh, 16, HEAD_DIM), lambda bi, hi, i: (bi, hi, 0, 0)),
                  pl.BlockSpec((None, nkb, KEY_BLOCK, (nh // 2) * K_AUG), lambda bi, hi, i: (bi, 0, 0, hi)),
                  pl.BlockSpec((None, nh, nkb, V_ROWS, KEY_BLOCK), lambda bi, hi, i: (bi, hi, 0, 0, 0))],
        out_specs=pl.BlockSpec((None, nh * HEAD_DIM, tq), lambda bi, hi, i: (bi, hi, i)),
        out_shape=jax.ShapeDtypeStruct((b, MOBA_WIDTH, seq), BF16),
        scratch_shapes=[pltpu.VMEM((2, KEY_BLOCK, nh * tq), F32),
                        pltpu.VMEM((1, nh * tq), F32),
                        pltpu.VMEM((V_ROWS, nh * tq), F32)],
        compiler_params=pltpu.CompilerParams(
            dimension_semantics=("parallel", "parallel", "arbitrary"), vmem_limit_bytes=VMEM_LIMIT),
        name="moba",
    )(qmt, kmean, ke, vt)


def _merge_ffn_body(x_ref, yn_ref, ym_ref, gm_ref, wun_ref, wum_ref, wo_ref, g2_ref,
                    w1_ref, w2_ref, gf_ref, o_ref):
    tn = (((0,), (0,)), ((), ()))
    a = lax.dot_general(yn_ref[...], wun_ref[...], tn, preferred_element_type=F32)
    c = lax.dot_general(ym_ref[...], wum_ref[...], tn, preferred_element_type=F32)
    mixed = gm_ref[:, 0:D_MODEL] * a + gm_ref[:, D_MODEL:2 * D_MODEL] * c
    x1 = x_ref[...] + _dot(mixed.astype(BF16), wo_ref[...])
    ms = jnp.mean(x1 * x1, axis=-1, keepdims=True)
    h2 = (x1 * lax.rsqrt(ms + NORM_EPS) * g2_ref[...]).astype(BF16)
    x2 = x1
    for k in range(D_FF // FF_CHUNK):
        u = jnp.maximum(_dot(h2, w1_ref[:, k * FF_CHUNK:(k + 1) * FF_CHUNK]), 0.0)
        x2 = x2 + _dot((u * u).astype(BF16), w2_ref[k * FF_CHUNK:(k + 1) * FF_CHUNK, :])
    ms = jnp.mean(x2 * x2, axis=-1, keepdims=True)
    o_ref[...] = x2 * lax.rsqrt(ms + NORM_EPS) * gf_ref[...]


def _merge_ffn(x2, ynt, ymt, gm, wun, wum, wo, g2, w1, w2, gf):
    m = x2.shape[0]
    seq = ynt.shape[2]
    tm = 512
    nt = seq // tm
    row = lambda i: (i, 0)
    const = lambda i: (0, 0)
    feat = lambda i: (i // nt, 0, i % nt)
    resident = lambda shape: pl.BlockSpec(shape, const, pipeline_mode=pl.Buffered(1))
    return pl.pallas_call(
        _merge_ffn_body,
        grid=(m // tm,),
        in_specs=[pl.BlockSpec((tm, D_MODEL), row),
                  pl.BlockSpec((None, NSA_WIDTH, tm), feat),
                  pl.BlockSpec((None, MOBA_WIDTH, tm), feat),
                  pl.BlockSpec((tm, 2 * D_MODEL), row),
                  resident((NSA_WIDTH, D_MODEL)),
                  resident((MOBA_WIDTH, D_MODEL)),
                  resident((D_MODEL, D_MODEL)),
                  pl.BlockSpec((1, D_MODEL), const),
                  resident((D_MODEL, D_FF)),
                  resident((D_FF, D_MODEL)),
                  pl.BlockSpec((1, D_MODEL), const)],
        out_specs=pl.BlockSpec((tm, D_MODEL), row),
        out_shape=jax.ShapeDtypeStruct((m, D_MODEL), F32),
        compiler_params=pltpu.CompilerParams(dimension_semantics=("parallel",),
                                             vmem_limit_bytes=VMEM_LIMIT),
        name="merge_ffn",
    )(x2, ynt, ymt, gm, wun, wum, wo, g2, w1, w2, gf)


def _rope_tables(seq):
    half = ROT_DIM // 2
    inv = ROPE_THETA ** (-jnp.arange(0, ROT_DIM, 2, dtype=F32) / ROT_DIM)
    ang = jnp.arange(seq, dtype=F32)[:, None] * inv[None, :]
    cos, sin = jnp.cos(ang), jnp.sin(ang)
    pad = HEAD_DIM - ROT_DIM
    one_head = lambda a, b, fill: jnp.concatenate([a, b, jnp.full((seq, pad), fill, F32)], axis=1)
    zeros = jnp.zeros((seq, half), F32)
    two = lambda a: jnp.concatenate([a, a], axis=1)
    return (two(one_head(cos, cos, 1.0)),
            two(one_head(-sin, zeros, 0.0)),
            two(one_head(zeros, sin, 0.0)),
            cos.T, sin.T)


def _overlap_t(ncp, nsb):
    i = np.arange(ncp)[None, :]
    j = np.arange(HEAD_DIM)[:, None]
    start = i * CMP_STRIDE
    end = start + CMP_BLOCK - 1
    ov = (end >= j * SEL_BLOCK) & (start <= j * SEL_BLOCK + SEL_BLOCK - 1) & (i < ncp - 1) & (j < nsb)
    return jnp.asarray(ov.astype(np.float32), dtype=BF16)


def _block_diag(w):
    z = jnp.zeros_like(w)
    return jnp.concatenate([jnp.concatenate([w, z], axis=-1), jnp.concatenate([z, w], axis=-1)], axis=-2)


def _projection_weights(w):
    kv = lambda j: w[:, OFF_KV + j * KV_WIDTH:OFF_KV + (j + 1) * KV_WIDTH]
    mo = lambda j: w[:, OFF_M + j * MOBA_WIDTH:OFF_M + (j + 1) * MOBA_WIDTH]
    wa = jnp.concatenate([kv(0), kv(1), kv(2), kv(4), mo(1), w[:, OFF_GM:]], axis=1)
    gn = w[:, OFF_GN:OFF_M].T.reshape(NSA_KV_GROUPS, 3 * NSA_GROUP_SIZE, D_MODEL)
    gn = jnp.concatenate([gn, jnp.zeros((NSA_KV_GROUPS, GATE_ROWS - 3 * NSA_GROUP_SIZE, D_MODEL), w.dtype)], axis=1)
    wb = jnp.concatenate([w[:, :OFF_KV].T, mo(0).T, kv(3).T, kv(5).T, mo(2).T,
                          gn.reshape(NSA_KV_GROUPS * GATE_ROWS, D_MODEL)], axis=0)
    return wa.astype(BF16), wb.astype(BF16)


def _mixers(x, norm1_g, w_in, cmp_pe_k, cmp_pe_v, cmp_k_w1, cmp_k_w2, cmp_v_w1, cmp_v_w2):
    b, seq, _ = x.shape
    g, d = NSA_KV_GROUPS, HEAD_DIM
    nkb = seq // KEY_BLOCK
    ncp = seq // CMP_STRIDE
    nsb = seq // SEL_BLOCK
    layer = 0

    wa, wb = _projection_weights(w_in[layer])
    x2 = x.reshape(b * seq, D_MODEL)
    (qt, qmt, gnt, kcx, vcx, ke, kw, vts, vtw, kem, vtm, kmean8, ogm) = _in_proj(
        x2, norm1_g[layer][None, :], wa, wb, _rope_tables(seq), b, seq)

    per_group = lambda a: jnp.concatenate([a] * g, axis=-1)
    pe_rows = lambda pe: jnp.broadcast_to(per_group(pe)[:, None, :], (CMP_BLOCK, SUBLANES, LANES)).astype(BF16)
    w1_blocks = lambda w1: _block_diag(w1.reshape(CMP_BLOCK, d, d)).astype(BF16)
    kc, vct = _compress(
        kcx.reshape(b, seq, KV_WIDTH), vcx.reshape(b, seq, KV_WIDTH),
        pe_rows(cmp_pe_k[layer]), pe_rows(cmp_pe_v[layer]),
        w1_blocks(cmp_k_w1[layer]), w1_blocks(cmp_v_w1[layer]),
        _block_diag(cmp_k_w2[layer]).astype(BF16), _block_diag(cmp_v_w2[layer].T).astype(BF16))
    ynt = _nsa(qt, gnt, kc, vct, ke, vts, kw, vtw, _overlap_t(ncp, nsb))

    kmean = kmean8[:, ::SUBLANES, :].reshape(b, nkb, MOBA_HEADS, d)
    kmean = jnp.swapaxes(kmean, 1, 2)
    kmean = jnp.concatenate([kmean, jnp.zeros((b, MOBA_HEADS, 16 - nkb, d), F32)], axis=2)
    ymt = _moba(qmt, kmean, kem, vtm)
    return x2, ynt, ymt, ogm


def kernel(x, norm1_g, w_in, cmp_pe_k, cmp_pe_v, cmp_k_w1, cmp_k_w2, cmp_v_w1, cmp_v_w2,
           w_up_nsa, w_up_moba, w_out, norm2_g, w_ff1, w_ff2, norm_f_g):
    b, seq, _ = x.shape
    layer = 0
    x2, ynt, ymt, ogm = _mixers(x, norm1_g, w_in, cmp_pe_k, cmp_pe_v, cmp_k_w1, cmp_k_w2, cmp_v_w1, cmp_v_w2)
    out = _merge_ffn(x2, ynt, ymt, ogm, w_up_nsa[layer].astype(BF16), w_up_moba[layer].astype(BF16),
                     w_out[layer].astype(BF16), norm2_g[layer][None, :],
                     w_ff1[layer].astype(BF16), w_ff2[layer].astype(BF16), norm_f_g[None, :])
    return out.reshape(b, seq, D_MODEL)
```

```python
import functools

import numpy as np
import jax
import jax.numpy as jnp
from jax import lax
from jax.experimental import pallas as pl
from jax.experimental.pallas import tpu as pltpu

F32 = jnp.float32
BF16 = jnp.bfloat16

D_MODEL = 1024
HEAD_DIM = 64
ROT_DIM = HEAD_DIM // 4
ROPE_THETA = 500000.0
NORM_EPS = 1e-6

NSA_HEADS = 8
NSA_KV_GROUPS = 2
NSA_GROUP_SIZE = NSA_HEADS // NSA_KV_GROUPS
CMP_BLOCK = 32
CMP_STRIDE = 16
SEL_BLOCK = 64
SEL_TOPN = 16
WINDOW = 512

MOBA_HEADS = 8
MOBA_BLOCK = 256
MOBA_TOPK = 3

D_FF = 4 * D_MODEL
NSA_WIDTH = NSA_HEADS * HEAD_DIM
KV_WIDTH = NSA_KV_GROUPS * HEAD_DIM
MOBA_WIDTH = MOBA_HEADS * HEAD_DIM
OFF_KV = NSA_WIDTH
OFF_GN = OFF_KV + 6 * KV_WIDTH
OFF_M = OFF_GN + 3 * NSA_HEADS
OFF_GM = OFF_M + 3 * MOBA_WIDTH
IN_WIDTH = OFF_GM + 2 * D_MODEL

LANES = 128
SUBLANES = 8
KEY_BLOCK = 256
Q_TILE = 256
V_ROWS = 80
K_AUG = 2 * LANES
NEG = -1e30
SCALE = float(HEAD_DIM ** -0.5 * np.log2(np.e))
MOBA_HEADS_PER_STEP = 8
GATE_ROWS = 16
FF_CHUNK = 1024
VMEM_LIMIT = 56 * 1024 * 1024

A_KC, A_KSW, A_KM, A_GM = 0, 256, 512, 1024
A_WIDTH = A_GM + 2 * D_MODEL
B_QN, B_QM, B_V, B_VM, B_GN = 0, 512, 1024, 1280, 1792
B_ROWS = B_GN + NSA_KV_GROUPS * GATE_ROWS


def _dot(a, b):
    return jnp.dot(a, b, preferred_element_type=F32)


def _dot_nt(a, b):
    return lax.dot_general(a, b, (((1,), (1,)), ((), ())), preferred_element_type=F32)


def _sigmoid(y):
    return 1.0 / (1.0 + jnp.exp(-y))


def _in_proj_body(x_ref, g_ref, wa_ref, wb_ref, c_ref, s1_ref, s2_ref, ct_ref, st_ref,
                  qt_ref, qmt_ref, gnt_ref, kcx_ref, vcx_ref, ke_ref, kw_ref, vts_ref, vtw_ref,
                  kem_ref, vtm_ref, kmean_ref, gm_ref, h_ref, *, tiles_per_seq):
    tm = x_ref.shape[0]
    it = pl.program_id(0) % tiles_per_seq
    x = x_ref[...]
    ms = jnp.mean(x * x, axis=-1, keepdims=True)
    h_ref[...] = (x * lax.rsqrt(ms + NORM_EPS) * g_ref[...]).astype(BF16)

    cos = c_ref[...]
    sin_hi = s1_ref[...]
    sin_lo = s2_ref[...]

    def proj(c0, width):
        return _dot(h_ref[...], wa_ref[:, c0:c0 + width])

    def rope(yc):
        return (yc * cos + pltpu.roll(yc, LANES - ROT_DIM // 2, 1) * sin_hi
                + pltpu.roll(yc, ROT_DIM // 2, 1) * sin_lo)

    y = proj(A_KC, 2 * KV_WIDTH)
    kcx_ref[...] = y[:, 0:KV_WIDTH]
    vcx_ref[...] = y[:, KV_WIDTH:2 * KV_WIDTH]

    row = lax.broadcasted_iota(jnp.int32, (tm, 1), 0)
    lane = lax.broadcasted_iota(jnp.int32, (1, LANES), 1)
    y = proj(A_KSW, 2 * KV_WIDTH)
    ke_ref[:, 0:LANES] = rope(y[:, 0:LANES]).astype(BF16)
    sel_blk = (it * tm + row) >> (SEL_BLOCK.bit_length() - 1)
    ke_ref[:, LANES:K_AUG] = jnp.where(sel_blk == lane, 1.0, 0.0).astype(BF16)
    kw_ref[...] = rope(y[:, LANES:2 * LANES]).astype(BF16)

    y = proj(A_KM, MOBA_WIDTH)
    own_blk = jnp.where(lane == it, 1.0, 0.0).astype(BF16)
    sums = []
    for j in range(MOBA_WIDTH // LANES):
        kr = rope(y[:, j * LANES:(j + 1) * LANES])
        sums.append(jnp.mean(kr, axis=0, keepdims=True))
        kem_ref[:, j * K_AUG:j * K_AUG + LANES] = kr.astype(BF16)
        kem_ref[:, j * K_AUG + LANES:(j + 1) * K_AUG] = jnp.broadcast_to(own_blk, (tm, LANES))
    kmean_ref[...] = jnp.broadcast_to(jnp.concatenate(sums, axis=1), kmean_ref.shape)

    for part in range(4):
        w = 2 * D_MODEL // 4
        gm_ref[:, part * w:(part + 1) * w] = _sigmoid(proj(A_GM + part * w, w)).astype(BF16)

    cos_t = ct_ref[...]
    sin_t = st_ref[...]

    def proj_t(r0, rows):
        return _dot_nt(wb_ref[r0:r0 + rows, :], h_ref[...])

    def rope_t(yt):
        half = ROT_DIM // 2
        out = []
        for hd in range(yt.shape[0] // HEAD_DIM):
            a = yt[hd * HEAD_DIM:hd * HEAD_DIM + half, :]
            b = yt[hd * HEAD_DIM + half:hd * HEAD_DIM + ROT_DIM, :]
            out += [a * cos_t - b * sin_t, a * sin_t + b * cos_t,
                    yt[hd * HEAD_DIM + ROT_DIM:(hd + 1) * HEAD_DIM, :]]
        return jnp.concatenate(out, axis=0)

    y = proj_t(B_QN, NSA_WIDTH)
    qt_ref[0:NSA_WIDTH, :] = (y * SCALE).astype(BF16)
    qt_ref[NSA_WIDTH:2 * NSA_WIDTH, :] = (rope_t(y) * SCALE).astype(BF16)
    qmt_ref[...] = rope_t(proj_t(B_QM, MOBA_WIDTH))

    ones_row = jnp.where(lax.broadcasted_iota(jnp.int32, (V_ROWS - HEAD_DIM, tm), 0) == 0, 1.0, 0.0).astype(BF16)
    y = proj_t(B_V, 4 * HEAD_DIM)
    for g in range(NSA_KV_GROUPS):
        vts_ref[g, 0:HEAD_DIM, :] = y[g * HEAD_DIM:(g + 1) * HEAD_DIM, :].astype(BF16)
        vts_ref[g, HEAD_DIM:V_ROWS, :] = ones_row
        vtw_ref[g, 0:HEAD_DIM, :] = y[(2 + g) * HEAD_DIM:(3 + g) * HEAD_DIM, :].astype(BF16)
        vtw_ref[g, HEAD_DIM:V_ROWS, :] = ones_row
    y = proj_t(B_VM, MOBA_WIDTH)
    for hd in range(MOBA_HEADS):
        vtm_ref[hd, 0:HEAD_DIM, :] = y[hd * HEAD_DIM:(hd + 1) * HEAD_DIM, :].astype(BF16)
        vtm_ref[hd, HEAD_DIM:V_ROWS, :] = ones_row
    y = _sigmoid(proj_t(B_GN, NSA_KV_GROUPS * GATE_ROWS))
    for g in range(NSA_KV_GROUPS):
        gnt_ref[g] = y[g * GATE_ROWS:(g + 1) * GATE_ROWS, :]


def _in_proj(x2, g1, wa, wb, tabs, batch, seq):
    m = x2.shape[0]
    tm = KEY_BLOCK
    nt = seq // tm
    row = lambda i: (i, 0)
    const = lambda i: (0, 0)
    tab = lambda i: (i % nt, 0)
    tab_t = lambda i: (0, i % nt)
    feat = lambda i: (i // nt, 0, i % nt)
    blk4 = lambda i: (i // nt, i % nt, 0, 0)
    blk5 = lambda i: (i // nt, 0, i % nt, 0, 0)
    g = NSA_KV_GROUPS
    outs = [
        (jax.ShapeDtypeStruct((batch, 2 * NSA_WIDTH, seq), BF16), pl.BlockSpec((None, 2 * NSA_WIDTH, tm), feat)),
        (jax.ShapeDtypeStruct((batch, MOBA_WIDTH, seq), F32), pl.BlockSpec((None, MOBA_WIDTH, tm), feat)),
        (jax.ShapeDtypeStruct((batch, g, GATE_ROWS, seq), F32),
         pl.BlockSpec((None, g, GATE_ROWS, tm), lambda i: (i // nt, 0, 0, i % nt))),
        (jax.ShapeDtypeStruct((m, KV_WIDTH), F32), pl.BlockSpec((tm, KV_WIDTH), row)),
        (jax.ShapeDtypeStruct((m, KV_WIDTH), F32), pl.BlockSpec((tm, KV_WIDTH), row)),
        (jax.ShapeDtypeStruct((batch, nt, tm, K_AUG), BF16), pl.BlockSpec((None, None, tm, K_AUG), blk4)),
        (jax.ShapeDtypeStruct((batch, nt, tm, LANES), BF16), pl.BlockSpec((None, None, tm, LANES), blk4)),
        (jax.ShapeDtypeStruct((batch, g, nt, V_ROWS, tm), BF16), pl.BlockSpec((None, g, None, V_ROWS, tm), blk5)),
        (jax.ShapeDtypeStruct((batch, g, nt, V_ROWS, tm), BF16), pl.BlockSpec((None, g, None, V_ROWS, tm), blk5)),
        (jax.ShapeDtypeStruct((batch, nt, tm, (MOBA_HEADS // 2) * K_AUG), BF16),
         pl.BlockSpec((None, None, tm, (MOBA_HEADS // 2) * K_AUG), blk4)),
        (jax.ShapeDtypeStruct((batch, MOBA_HEADS, nt, V_ROWS, tm), BF16),
         pl.BlockSpec((None, MOBA_HEADS, None, V_ROWS, tm), blk5)),
        (jax.ShapeDtypeStruct((batch, nt * SUBLANES, MOBA_WIDTH), F32),
         pl.BlockSpec((None, SUBLANES, MOBA_WIDTH), lambda i: (i // nt, i % nt, 0))),
        (jax.ShapeDtypeStruct((m, 2 * D_MODEL), BF16), pl.BlockSpec((tm, 2 * D_MODEL), row)),
    ]
    return pl.pallas_call(
        functools.partial(_in_proj_body, tiles_per_seq=nt),
        grid=(m // tm,),
        in_specs=[pl.BlockSpec((tm, D_MODEL), row),
                  pl.BlockSpec((1, D_MODEL), const),
                  pl.BlockSpec((D_MODEL, A_WIDTH), const),
                  pl.BlockSpec((B_ROWS, D_MODEL), const),
                  pl.BlockSpec((tm, LANES), tab),
                  pl.BlockSpec((tm, LANES), tab),
                  pl.BlockSpec((tm, LANES), tab),
                  pl.BlockSpec((SUBLANES, tm), tab_t),
                  pl.BlockSpec((SUBLANES, tm), tab_t)],
        out_specs=[o[1] for o in outs],
        out_shape=[o[0] for o in outs],
        scratch_shapes=[pltpu.VMEM((tm, D_MODEL), BF16)],
        compiler_params=pltpu.CompilerParams(dimension_semantics=("parallel",),
                                             vmem_limit_bytes=VMEM_LIMIT),
        name="in_proj",
    )(x2, g1, wa, wb, *tabs)


def _gelu_tanh(x):
    return 0.5 * x * (1.0 + jnp.tanh(np.sqrt(2.0 / np.pi) * (x + 0.044715 * (x * x * x))))


def _compress_body(kx_ref, vx_ref, pek_ref, pev_ref, kw1_ref, vw1_ref, kw2_ref, vw2t_ref,
                   kc_ref, vct_ref):
    ncp = kc_ref.shape[0]

    def hidden(x_ref, pe_ref, w1_ref):
        ya = jnp.zeros((ncp, LANES), F32)
        yb = jnp.zeros((ncp, LANES), F32)
        pe = jnp.zeros((SUBLANES, LANES), F32)
        for r in range(CMP_STRIDE):
            xr = x_ref[pl.ds(r, ncp, stride=CMP_STRIDE), :].astype(BF16)
            ya = ya + _dot(xr, w1_ref[r])
            yb = yb + _dot(xr, w1_ref[CMP_STRIDE + r])
        for l in range(CMP_BLOCK):
            pe = pe + _dot(pe_ref[l], w1_ref[l])
        return _gelu_tanh(ya + pltpu.roll(yb, ncp - 1, 0) + pe[0:1, :])

    hk = hidden(kx_ref, pek_ref, kw1_ref)
    kc_ref[...] = _dot(hk.astype(BF16), kw2_ref[...]).astype(BF16)
    hv = hidden(vx_ref, pev_ref, vw1_ref)
    vct = _dot_nt(vw2t_ref[...], hv.astype(BF16)).astype(BF16)
    ones_row = jnp.where(lax.broadcasted_iota(jnp.int32, (V_ROWS - HEAD_DIM, ncp), 0) == 0, 1.0, 0.0).astype(BF16)
    for g in range(NSA_KV_GROUPS):
        vct_ref[g, 0:HEAD_DIM, :] = vct[g * HEAD_DIM:(g + 1) * HEAD_DIM, :]
        vct_ref[g, HEAD_DIM:V_ROWS, :] = ones_row


def _compress(kx, vx, pek, pev, kw1, vw1, kw2, vw2t):
    b, seq, _ = kx.shape
    ncp = seq // CMP_STRIDE
    g = NSA_KV_GROUPS
    tok = lambda bi: (bi, 0, 0)
    const3 = lambda bi: (0, 0, 0)
    const2 = lambda bi: (0, 0)
    return pl.pallas_call(
        _compress_body,
        grid=(b,),
        in_specs=[pl.BlockSpec((None, seq, LANES), tok),
                  pl.BlockSpec((None, seq, LANES), tok),
                  pl.BlockSpec((CMP_BLOCK, SUBLANES, LANES), const3),
                  pl.BlockSpec((CMP_BLOCK, SUBLANES, LANES), const3),
                  pl.BlockSpec((CMP_BLOCK, LANES, LANES), const3),
                  pl.BlockSpec((CMP_BLOCK, LANES, LANES), const3),
                  pl.BlockSpec((LANES, LANES), const2),
                  pl.BlockSpec((LANES, LANES), const2)],
        out_specs=[pl.BlockSpec((None, ncp, LANES), tok),
                   pl.BlockSpec((None, g, V_ROWS, ncp), lambda bi: (bi, 0, 0, 0))],
        out_shape=[jax.ShapeDtypeStruct((b, ncp, LANES), BF16),
                   jax.ShapeDtypeStruct((b, g, V_ROWS, ncp), BF16)],
        compiler_params=pltpu.CompilerParams(dimension_semantics=("parallel",)),
        name="compress",
    )(kx, vx, pek, pev, kw1, vw1, kw2, vw2t)


def _rank_rows(score, rank, k0, k1):
    rows = score.shape[0]
    j = lax.broadcasted_iota(jnp.int32, (SUBLANES, 1), 0)
    blocks = [score[b:b + SUBLANES, :] for b in range(0, rows, SUBLANES)]
    for k in range(k0, k1):
        rk = score[k:k + 1, :]
        before = lambda sb: jnp.where(rk > sb, 1.0, 0.0)
        unless = lambda sb: jnp.where(sb > rk, 0.0, 1.0)
        counts = []
        for b, sb in zip(range(0, rows, SUBLANES), blocks):
            if b + SUBLANES <= k:
                counts.append(before(sb))
            elif b > k:
                counts.append(unless(sb))
            else:
                counts.append(jnp.where(j + b > k, unless(sb), before(sb)))
        rank = rank + jnp.concatenate(counts, axis=0)
    return rank


def _topk_mask(score, n_rows, k_top):
    return _rank_rows(score, jnp.zeros(score.shape, F32), 0, n_rows) < k_top


def _finish(acc):
    return acc[0:HEAD_DIM, :] / acc[HEAD_DIM:HEAD_DIM + 1, :]


def _softmax_reset(m_ref, acc_ref):
    m_ref[...] = jnp.full(m_ref.shape, NEG, F32)
    acc_ref[...] = jnp.zeros(acc_ref.shape, F32)


def _park(parked, slot, sb):
    s_buf, bm_ref = parked
    s_buf[slot, 0:sb.shape[0], :] = sb
    bm_ref[slot] = jnp.max(sb, axis=0, keepdims=True)


def _softmax_update(m_ref, acc_ref, parked, slot, values):
    s_buf, bm_ref = parked
    m_old = m_ref[...]
    m_new = jnp.maximum(m_old, bm_ref[slot])
    pb = jnp.exp2(s_buf[slot] - m_new).astype(BF16)
    acc_ref[...] = jnp.exp2(m_old - m_new) * acc_ref[...] + values(pb)
    m_ref[...] = m_new


def _online_blocks(i, parked, m_ref, acc_ref, scores, values, causal):
    block = lambda kb: jnp.maximum(kb, 0)

    def park(slot, kb, live, mask=None):
        sb = scores(block(kb), live)
        _park(parked, slot, sb if mask is None else jnp.where(mask, sb, NEG))

    def update(slot, kb):
        _softmax_update(m_ref, acc_ref, parked, slot, lambda pb: values(block(kb), pb))

    _softmax_reset(m_ref, acc_ref)
    first = -((i + 1) & 1)
    n_pairs = (i + 2) // 2
    park(0, first, first >= 0)

    @pl.loop(0, n_pairs - 1)
    def _(p):
        ka = first + 2 * p
        park(1, ka + 1, True)
        update(0, ka)
        park(0, ka + 2, True)
        update(1, ka + 1)

    park(1, i, True, causal)
    update(0, i - 1)
    update(1, i)
    return acc_ref[...]


def _online_blocks_diag_first(i, parked, m_ref, acc_ref, diag_scores, select, values, causal):
    past = lambda x: jnp.clip(x - 1, 0, jnp.maximum(i - 1, 0))

    def park(slot, x):
        _park(parked, slot, scores(past(x), x - 1 < i))

    def update(slot, kb):
        _softmax_update(m_ref, acc_ref, parked, slot, lambda pb: values(kb, pb))

    _softmax_reset(m_ref, acc_ref)
    _park(parked, 0, jnp.where(causal, diag_scores(), NEG))
    scores = select()
    park(1, 1)
    update(0, i)
    park(0, 2)
    update(1, past(1))

    @pl.loop(1, (i + 2) // 2)
    def _(p):
        park(1, 2 * p + 1)
        update(0, past(2 * p))
        park(0, 2 * p + 2)
        update(1, past(2 * p + 1))

    return acc_ref[...]


def _nsa_body(q_ref, qr_ref, gn_ref, kc_ref, vct_ref, ke_ref, vts_ref, kw_ref, vtw_ref, ovt_ref,
              y_ref, s_buf, bm_ref, m_ref, acc_ref, *, n_sel_blocks):
    tq = q_ref.shape[1]
    ng, rg = NSA_KV_GROUPS, NSA_GROUP_SIZE
    nh = ng * rg
    i = pl.program_id(1)
    t0 = i * tq
    head = lambda a, h: a[h * HEAD_DIM:(h + 1) * HEAD_DIM, :]
    cols = lambda a, h, n=1: a[:, h * tq:(h + n) * tq]
    q = q_ref[...]
    qr = qr_ref[...]
    tpos = t0 + lax.broadcasted_iota(jnp.int32, (1, tq), 1)
    tpos_n = jnp.concatenate([tpos] * nh, axis=1)

    zero = jnp.zeros((HEAD_DIM, tq), BF16)
    place = lambda h, a: jnp.concatenate([a, zero] if h < rg else [zero, a], axis=0)
    qc = jnp.concatenate([place(h, head(q, h)) for h in range(nh)], axis=1)
    qw = jnp.concatenate([place(h, head(qr, h)) for h in range(nh)], axis=1)
    per_group = lambda v_ref, kb, pb: jnp.concatenate(
        [_dot(v_ref[g, kb], cols(pb, g * rg, rg)) for g in range(ng)], axis=1)

    causal = (lax.broadcasted_iota(jnp.int32, (KEY_BLOCK, 1), 0)
              <= (lax.broadcasted_iota(jnp.int32, (1, nh * tq), 1) & (tq - 1)))

    ncp = kc_ref.shape[0]
    i1 = jnp.maximum(i - 1, 0)
    i2 = jnp.maximum(i - 2, 0)
    n = lax.broadcasted_iota(jnp.int32, (ncp, 1), 0)
    parked = (s_buf, bm_ref)
    _park(parked, 0, jnp.where((n * CMP_STRIDE + (CMP_BLOCK - 1)) <= tpos_n, _dot(kc_ref[...], qc), NEG))
    _park(parked, 1, jnp.where(causal, _dot(kw_ref[i], qw), NEG))

    p = jnp.exp2(s_buf[0, 0:ncp, :] - bm_ref[0])
    pb16 = p.astype(BF16)
    acc = jnp.concatenate([_dot(vct_ref[g], cols(pb16, g * rg, rg)) for g in range(ng)], axis=1)
    inv = jnp.where(tpos_n >= CMP_BLOCK - 1, 1.0 / jnp.maximum(acc[HEAD_DIM:HEAD_DIM + 1, :], 1e-30), 0.0)
    o_cmp = acc[0:HEAD_DIM, :] * inv

    pn = p * inv
    ps = jnp.concatenate([sum(cols(pn, g * rg + r) for r in range(1, rg)) + cols(pn, g * rg)
                          for g in range(ng)], axis=1)
    ps_hi = ps.astype(BF16)
    ps_lo = (ps - ps_hi.astype(F32)).astype(BF16)
    imp = _dot(ovt_ref[...], ps_hi) + _dot(ovt_ref[...], ps_lo)
    jb = lax.broadcasted_iota(jnp.int32, (imp.shape[0], 1), 0)
    cur = jnp.concatenate([tpos >> (SEL_BLOCK.bit_length() - 1)] * ng, axis=1)
    forced = (jb == 0) | (jb == cur) | (jb == cur - 1)
    imp = jnp.where(forced, jnp.inf, imp)
    imp = jnp.where(jb <= cur, imp, -jnp.inf)

    _park(parked, 0, jnp.where(i >= 1, _dot(kw_ref[i1], qw), NEG))
    _softmax_reset(m_ref, acc_ref)
    _softmax_update(m_ref, acc_ref, parked, 1, lambda pb: per_group(vtw_ref, i, pb))
    half = n_sel_blocks // 2
    rank = _rank_rows(imp, jnp.zeros(imp.shape, F32), 0, half)
    _park(parked, 1, jnp.where(jnp.logical_or(causal, i < 2), NEG, _dot(kw_ref[i2], qw)))
    _softmax_update(m_ref, acc_ref, parked, 0, lambda pb: per_group(vtw_ref, i1, pb))
    rank = _rank_rows(imp, rank, half, n_sel_blocks)
    _softmax_update(m_ref, acc_ref, parked, 1, lambda pb: per_group(vtw_ref, i2, pb))
    o_win = _finish(acc_ref[...])
    bias = jnp.where(rank < min(SEL_TOPN, n_sel_blocks), 0.0, NEG).astype(BF16)

    off = jnp.full((bias.shape[0], tq), NEG, BF16)
    aug = lambda b_rows: jnp.concatenate(
        [jnp.concatenate([cols(qw, h), b_rows(h // rg), zero], axis=0) for h in range(nh)], axis=1)
    qa = aug(lambda g: cols(bias, g))
    qa_off = aug(lambda g: off)
    o_sel = _finish(_online_blocks(
        i, parked, m_ref, acc_ref,
        lambda kb, live: _dot(ke_ref[kb], qa if live is True else jnp.where(live, qa, qa_off)),
        lambda kb, pb: per_group(vts_ref, kb, pb),
        causal))

    for h in range(nh):
        gates = gn_ref[h // rg]
        r = h % rg
        y_ref[h * HEAD_DIM:(h + 1) * HEAD_DIM, :] = (
            gates[3 * r:3 * r + 1, :] * cols(o_cmp, h)
            + gates[3 * r + 1:3 * r + 2, :] * cols(o_sel, h)
            + gates[3 * r + 2:3 * r + 3, :] * cols(o_win, h)).astype(y_ref.dtype)


def _nsa(qt, gnt, kc, vct, ke, vts, kw, vtw, ovt):
    b, _, seq = qt.shape
    g = NSA_KV_GROUPS
    nkb = seq // KEY_BLOCK
    ncp = kc.shape[1]
    tq = Q_TILE
    n_cols = NSA_HEADS * tq
    whole5 = lambda bi, i: (bi, 0, 0, 0, 0)
    whole4 = lambda bi, i: (bi, 0, 0, 0)
    return pl.pallas_call(
        functools.partial(_nsa_body, n_sel_blocks=seq // SEL_BLOCK),
        grid=(b, seq // tq),
        in_specs=[pl.BlockSpec((None, NSA_WIDTH, tq), lambda bi, i: (bi, 0, i)),
                  pl.BlockSpec((None, NSA_WIDTH, tq), lambda bi, i: (bi, 1, i)),
                  pl.BlockSpec((None, g, GATE_ROWS, tq), lambda bi, i: (bi, 0, 0, i)),
                  pl.BlockSpec((None, ncp, LANES), lambda bi, i: (bi, 0, 0)),
                  pl.BlockSpec((None, g, V_ROWS, ncp), whole4),
                  pl.BlockSpec((None, nkb, KEY_BLOCK, K_AUG), whole4),
                  pl.BlockSpec((None, g, nkb, V_ROWS, KEY_BLOCK), whole5),
                  pl.BlockSpec((None, nkb, KEY_BLOCK, LANES), whole4),
                  pl.BlockSpec((None, g, nkb, V_ROWS, KEY_BLOCK), whole5),
                  pl.BlockSpec((HEAD_DIM, ncp), lambda bi, i: (0, 0))],
        out_specs=pl.BlockSpec((None, NSA_WIDTH, tq), lambda bi, i: (bi, 0, i)),
        out_shape=jax.ShapeDtypeStruct((b, NSA_WIDTH, seq), BF16),
        scratch_shapes=[pltpu.VMEM((2, KEY_BLOCK, n_cols), F32),
                        pltpu.VMEM((2, 1, n_cols), F32),
                        pltpu.VMEM((1, n_cols), F32),
                        pltpu.VMEM((V_ROWS, n_cols), F32)],
        compiler_params=pltpu.CompilerParams(
            dimension_semantics=("parallel", "arbitrary"), vmem_limit_bytes=VMEM_LIMIT),
        name="nsa",
    )(qt, qt, gnt, kc, vct, ke, vts, kw, vtw, ovt)


def _moba_body(q_ref, km_ref, ke_ref, vt_ref, y_ref, s_buf, bm_ref, m_ref, acc_ref, *, n_blocks):
    tq = q_ref.shape[1]
    nh = km_ref.shape[0]
    i = pl.program_id(2)
    q = q_ref[...] * SCALE
    qb = q.astype(BF16)
    head = lambda a, h: a[h * HEAD_DIM:(h + 1) * HEAD_DIM, :]
    zero = jnp.zeros((HEAD_DIM, tq), BF16)
    n_bias = km_ref.shape[1]
    pad = jnp.zeros((K_AUG - LANES - n_bias, tq), BF16)

    def aug(h, b_rows):
        pair = [head(qb, h), zero] if h % 2 == 0 else [zero, head(qb, h)]
        return jnp.concatenate(pair + [b_rows, pad], axis=0)

    def keys(kb, h):
        return ke_ref[kb, :, (h // 2) * K_AUG:(h // 2 + 1) * K_AUG]

    def diag_scores():
        open_rows = jnp.zeros((n_bias, tq), BF16)
        return jnp.concatenate([_dot(keys(i, h), aug(h, open_rows)) for h in range(nh)], axis=1)

    def select():
        q_lo = (q - qb.astype(F32)).astype(BF16)
        gates = []
        for h in range(nh):
            km = km_ref[h]
            km_hi = km.astype(BF16)
            km_lo = (km - km_hi.astype(F32)).astype(BF16)
            gates.append(_dot(km_hi, head(qb, h)) + (_dot(km_hi, head(q_lo, h)) + _dot(km_lo, head(qb, h))))
        gate = jnp.concatenate(gates, axis=1)
        jb = lax.broadcasted_iota(jnp.int32, (gate.shape[0], 1), 0)
        gate = jnp.where(jb < i, gate, -jnp.inf)
        sel = _topk_mask(gate, n_blocks, min(MOBA_TOPK, n_blocks)) & (gate > -jnp.inf)
        bias = jnp.where(sel, 0.0, NEG).astype(BF16)
        off = jnp.full((n_bias, tq), NEG, BF16)
        qa = [aug(h, bias[:, h * tq:(h + 1) * tq]) for h in range(nh)]
        qa_off = [aug(h, off) for h in range(nh)]
        return lambda kb, live: jnp.concatenate(
            [_dot(keys(kb, h), jnp.where(live, qa[h], qa_off[h])) for h in range(nh)], axis=1)

    def values(kb, pb):
        return jnp.concatenate([_dot(vt_ref[h, kb], pb[:, h * tq:(h + 1) * tq]) for h in range(nh)], axis=1)

    causal = (lax.broadcasted_iota(jnp.int32, (KEY_BLOCK, 1), 0)
              <= (lax.broadcasted_iota(jnp.int32, (1, nh * tq), 1) & (tq - 1)))
    o = _finish(_online_blocks_diag_first(i, (s_buf, bm_ref), m_ref, acc_ref,
                                          diag_scores, select, values, causal))
    for h in range(nh):
        y_ref[h * HEAD_DIM:(h + 1) * HEAD_DIM, :] = o[:, h * tq:(h + 1) * tq].astype(y_ref.dtype)


def _moba(qmt, kmean, ke, vt):
    b, _, seq = qmt.shape
    nkb = seq // KEY_BLOCK
    tq = MOBA_BLOCK
    nh = MOBA_HEADS_PER_STEP
    return pl.pallas_call(
        functools.partial(_moba_body, n_blocks=nkb),
        grid=(b, MOBA_HEADS // nh, seq // tq),
        in_specs=[pl.BlockSpec((None, nh * HEAD_DIM, tq), lambda bi, hi, i: (bi, hi, i)),
                  pl.BlockSpec((None, nh, 16, HEAD_DIM), lambda bi, hi, i: (bi, hi, 0, 0)),
                  pl.BlockSpec((None, nkb, KEY_BLOCK, (nh // 2) * K_AUG), lambda bi, hi, i: (bi, 0, 0, hi)),
                  pl.BlockSpec((None, nh, nkb, V_ROWS, KEY_BLOCK), lambda bi, hi, i: (bi, hi, 0, 0, 0))],
        out_specs=pl.BlockSpec((None, nh * HEAD_DIM, tq), lambda bi, hi, i: (bi, hi, i)),
        out_shape=jax.ShapeDtypeStruct((b, MOBA_WIDTH, seq), BF16),
        scratch_shapes=[pltpu.VMEM((2, KEY_BLOCK, nh * tq), F32),
                        pltpu.VMEM((2, 1, nh * tq), F32),
                        pltpu.VMEM((1, nh * tq), F32),
                        pltpu.VMEM((V_ROWS, nh * tq), F32)],
        compiler_params=pltpu.CompilerParams(
            dimension_semantics=("parallel", "parallel", "arbitrary"), vmem_limit_bytes=VMEM_LIMIT),
        name="moba",
    )(qmt, kmean, ke, vt)


def _merge_ffn_body(x_ref, yn_ref, ym_ref, gm_ref, wun_ref, wum_ref, wo_ref, g2_ref,
                    w1_ref, w2_ref, gf_ref, o_ref):
    tn = (((0,), (0,)), ((), ()))
    a = lax.dot_general(yn_ref[...], wun_ref[...], tn, preferred_element_type=F32)
    c = lax.dot_general(ym_ref[...], wum_ref[...], tn, preferred_element_type=F32)
    mixed = gm_ref[:, 0:D_MODEL] * a + gm_ref[:, D_MODEL:2 * D_MODEL] * c
    x1 = x_ref[...] + _dot(mixed.astype(BF16), wo_ref[...])
    ms = jnp.mean(x1 * x1, axis=-1, keepdims=True)
    h2 = (x1 * lax.rsqrt(ms + NORM_EPS) * g2_ref[...]).astype(BF16)
    x2 = x1
    for k in range(D_FF // FF_CHUNK):
        u = jnp.maximum(_dot(h2, w1_ref[:, k * FF_CHUNK:(k + 1) * FF_CHUNK]), 0.0)
        x2 = x2 + _dot((u * u).astype(BF16), w2_ref[k * FF_CHUNK:(k + 1) * FF_CHUNK, :])
    ms = jnp.mean(x2 * x2, axis=-1, keepdims=True)
    o_ref[...] = x2 * lax.rsqrt(ms + NORM_EPS) * gf_ref[...]


def _merge_ffn(x2, ynt, ymt, gm, wun, wum, wo, g2, w1, w2, gf):
    m = x2.shape[0]
    seq = ynt.shape[2]
    tm = 512
    nt = seq // tm
    row = lambda i: (i, 0)
    const = lambda i: (0, 0)
    feat = lambda i: (i // nt, 0, i % nt)
    resident = lambda shape: pl.BlockSpec(shape, const, pipeline_mode=pl.Buffered(1))
    return pl.pallas_call(
        _merge_ffn_body,
        grid=(m // tm,),
        in_specs=[pl.BlockSpec((tm, D_MODEL), row),
                  pl.BlockSpec((None, NSA_WIDTH, tm), feat),
                  pl.BlockSpec((None, MOBA_WIDTH, tm), feat),
                  pl.BlockSpec((tm, 2 * D_MODEL), row),
                  resident((NSA_WIDTH, D_MODEL)),
                  resident((MOBA_WIDTH, D_MODEL)),
                  resident((D_MODEL, D_MODEL)),
                  pl.BlockSpec((1, D_MODEL), const),
                  resident((D_MODEL, D_FF)),
                  resident((D_FF, D_MODEL)),
                  pl.BlockSpec((1, D_MODEL), const)],
        out_specs=pl.BlockSpec((tm, D_MODEL), row),
        out_shape=jax.ShapeDtypeStruct((m, D_MODEL), F32),
        compiler_params=pltpu.CompilerParams(dimension_semantics=("parallel",),
                                             vmem_limit_bytes=VMEM_LIMIT),
        name="merge_ffn",
    )(x2, ynt, ymt, gm, wun, wum, wo, g2, w1, w2, gf)


def _rope_tables(seq):
    half = ROT_DIM // 2
    inv = ROPE_THETA ** (-jnp.arange(0, ROT_DIM, 2, dtype=F32) / ROT_DIM)
    ang = jnp.arange(seq, dtype=F32)[:, None] * inv[None, :]
    cos, sin = jnp.cos(ang), jnp.sin(ang)
    pad = HEAD_DIM - ROT_DIM
    one_head = lambda a, b, fill: jnp.concatenate([a, b, jnp.full((seq, pad), fill, F32)], axis=1)
    zeros = jnp.zeros((seq, half), F32)
    two = lambda a: jnp.concatenate([a, a], axis=1)
    return (two(one_head(cos, cos, 1.0)),
            two(one_head(-sin, zeros, 0.0)),
            two(one_head(zeros, sin, 0.0)),
            cos.T, sin.T)


def _overlap_t(ncp, nsb):
    i = np.arange(ncp)[None, :]
    j = np.arange(HEAD_DIM)[:, None]
    start = i * CMP_STRIDE
    end = start + CMP_BLOCK - 1
    ov = (end >= j * SEL_BLOCK) & (start <= j * SEL_BLOCK + SEL_BLOCK - 1) & (i < ncp - 1) & (j < nsb)
    return jnp.asarray(ov.astype(np.float32), dtype=BF16)


def _block_diag(w):
    z = jnp.zeros_like(w)
    return jnp.concatenate([jnp.concatenate([w, z], axis=-1), jnp.concatenate([z, w], axis=-1)], axis=-2)


def _projection_weights(w):
    kv = lambda j: w[:, OFF_KV + j * KV_WIDTH:OFF_KV + (j + 1) * KV_WIDTH]
    mo = lambda j: w[:, OFF_M + j * MOBA_WIDTH:OFF_M + (j + 1) * MOBA_WIDTH]
    wa = jnp.concatenate([kv(0), kv(1), kv(2), kv(4), mo(1), w[:, OFF_GM:]], axis=1)
    gn = w[:, OFF_GN:OFF_M].T.reshape(NSA_KV_GROUPS, 3 * NSA_GROUP_SIZE, D_MODEL)
    gn = jnp.concatenate([gn, jnp.zeros((NSA_KV_GROUPS, GATE_ROWS - 3 * NSA_GROUP_SIZE, D_MODEL), w.dtype)], axis=1)
    wb = jnp.concatenate([w[:, :OFF_KV].T, mo(0).T, kv(3).T, kv(5).T, mo(2).T,
                          gn.reshape(NSA_KV_GROUPS * GATE_ROWS, D_MODEL)], axis=0)
    return wa.astype(BF16), wb.astype(BF16)


def _mixers(x, norm1_g, w_in, cmp_pe_k, cmp_pe_v, cmp_k_w1, cmp_k_w2, cmp_v_w1, cmp_v_w2):
    b, seq, _ = x.shape
    g, d = NSA_KV_GROUPS, HEAD_DIM
    nkb = seq // KEY_BLOCK
    ncp = seq // CMP_STRIDE
    nsb = seq // SEL_BLOCK
    layer = 0

    wa, wb = _projection_weights(w_in[layer])
    x2 = x.reshape(b * seq, D_MODEL)
    (qt, qmt, gnt, kcx, vcx, ke, kw, vts, vtw, kem, vtm, kmean8, ogm) = _in_proj(
        x2, norm1_g[layer][None, :], wa, wb, _rope_tables(seq), b, seq)

    per_group = lambda a: jnp.concatenate([a] * g, axis=-1)
    pe_rows = lambda pe: jnp.broadcast_to(per_group(pe)[:, None, :], (CMP_BLOCK, SUBLANES, LANES)).astype(BF16)
    w1_blocks = lambda w1: _block_diag(w1.reshape(CMP_BLOCK, d, d)).astype(BF16)
    kc, vct = _compress(
        kcx.reshape(b, seq, KV_WIDTH), vcx.reshape(b, seq, KV_WIDTH),
        pe_rows(cmp_pe_k[layer]), pe_rows(cmp_pe_v[layer]),
        w1_blocks(cmp_k_w1[layer]), w1_blocks(cmp_v_w1[layer]),
        _block_diag(cmp_k_w2[layer]).astype(BF16), _block_diag(cmp_v_w2[layer].T).astype(BF16))
    ynt = _nsa(qt, gnt, kc, vct, ke, vts, kw, vtw, _overlap_t(ncp, nsb))

    kmean = kmean8[:, ::SUBLANES, :].reshape(b, nkb, MOBA_HEADS, d)
    kmean = jnp.swapaxes(kmean, 1, 2)
    kmean = jnp.concatenate([kmean, jnp.zeros((b, MOBA_HEADS, 16 - nkb, d), F32)], axis=2)
    ymt = _moba(qmt, kmean, kem, vtm)
    return x2, ynt, ymt, ogm


def kernel(x, norm1_g, w_in, cmp_pe_k, cmp_pe_v, cmp_k_w1, cmp_k_w2, cmp_v_w1, cmp_v_w2,
           w_up_nsa, w_up_moba, w_out, norm2_g, w_ff1, w_ff2, norm_f_g):
    b, seq, _ = x.shape
    layer = 0
    x2, ynt, ymt, ogm = _mixers(x, norm1_g, w_in, cmp_pe_k, cmp_pe_v, cmp_k_w1, cmp_k_w2, cmp_v_w1, cmp_v_w2)
    out = _merge_ffn(x2, ynt, ymt, ogm, w_up_nsa[layer].astype(BF16), w_up_moba[layer].astype(BF16),
                     w_out[layer].astype(BF16), norm2_g[layer][None, :],
                     w_ff1[layer].astype(BF16), w_ff2[layer].astype(BF16), norm_f_g[None, :])
    return out.reshape(b, seq, D_MODEL)
```

```python
import functools

import numpy as np
import jax
import jax.numpy as jnp
from jax import lax
from jax.experimental import pallas as pl
from jax.experimental.pallas import tpu as pltpu

F32 = jnp.float32
BF16 = jnp.bfloat16

D_MODEL = 1024
HEAD_DIM = 64
ROT_DIM = HEAD_DIM // 4
ROPE_THETA = 500000.0
NORM_EPS = 1e-6

NSA_HEADS = 8
NSA_KV_GROUPS = 2
NSA_GROUP_SIZE = NSA_HEADS // NSA_KV_GROUPS
CMP_BLOCK = 32
CMP_STRIDE = 16
SEL_BLOCK = 64
SEL_TOPN = 16
WINDOW = 512

MOBA_HEADS = 8
MOBA_BLOCK = 256
MOBA_TOPK = 3

D_FF = 4 * D_MODEL
NSA_WIDTH = NSA_HEADS * HEAD_DIM
KV_WIDTH = NSA_KV_GROUPS * HEAD_DIM
MOBA_WIDTH = MOBA_HEADS * HEAD_DIM
OFF_KV = NSA_WIDTH
OFF_GN = OFF_KV + 6 * KV_WIDTH
OFF_M = OFF_GN + 3 * NSA_HEADS
OFF_GM = OFF_M + 3 * MOBA_WIDTH
IN_WIDTH = OFF_GM + 2 * D_MODEL

LANES = 128
SUBLANES = 8
KEY_BLOCK = 256
Q_TILE = 256
V_ROWS = 80
K_AUG = 2 * LANES
NEG = -1e30
SCALE = float(HEAD_DIM ** -0.5 * np.log2(np.e))
MOBA_HEADS_PER_STEP = 8
GATE_ROWS = 16
FF_CHUNK = 1024
VMEM_LIMIT = 56 * 1024 * 1024

A_KC, A_KSW, A_KM, A_GM = 0, 256, 512, 1024
A_WIDTH = A_GM + 2 * D_MODEL
B_QN, B_QM, B_V, B_VM, B_GN = 0, 512, 1024, 1280, 1792
B_ROWS = B_GN + NSA_KV_GROUPS * GATE_ROWS


def _dot(a, b):
    return jnp.dot(a, b, preferred_element_type=F32)


def _dot_nt(a, b):
    return lax.dot_general(a, b, (((1,), (1,)), ((), ())), preferred_element_type=F32)


def _sigmoid(y):
    return 1.0 / (1.0 + jnp.exp(-y))


def _in_proj_body(x_ref, g_ref, wa_ref, wb_ref, c_ref, s1_ref, s2_ref, ct_ref, st_ref,
                  qt_ref, qmt_ref, gnt_ref, kcx_ref, vcx_ref, ke_ref, kw_ref, vts_ref, vtw_ref,
                  kem_ref, vtm_ref, kmean_ref, gm_ref, h_ref, *, tiles_per_seq):
    tm = x_ref.shape[0]
    it = pl.program_id(0) % tiles_per_seq
    x = x_ref[...]
    ms = jnp.mean(x * x, axis=-1, keepdims=True)
    h_ref[...] = (x * lax.rsqrt(ms + NORM_EPS) * g_ref[...]).astype(BF16)

    cos = c_ref[...]
    sin_hi = s1_ref[...]
    sin_lo = s2_ref[...]

    def proj(c0, width):
        return _dot(h_ref[...], wa_ref[:, c0:c0 + width])

    def rope(yc):
        return (yc * cos + pltpu.roll(yc, LANES - ROT_DIM // 2, 1) * sin_hi
                + pltpu.roll(yc, ROT_DIM // 2, 1) * sin_lo)

    y = proj(A_KC, 2 * KV_WIDTH)
    kcx_ref[...] = y[:, 0:KV_WIDTH]
    vcx_ref[...] = y[:, KV_WIDTH:2 * KV_WIDTH]

    row = lax.broadcasted_iota(jnp.int32, (tm, 1), 0)
    lane = lax.broadcasted_iota(jnp.int32, (1, LANES), 1)
    y = proj(A_KSW, 2 * KV_WIDTH)
    ke_ref[:, 0:LANES] = rope(y[:, 0:LANES]).astype(BF16)
    sel_blk = (it * tm + row) >> (SEL_BLOCK.bit_length() - 1)
    ke_ref[:, LANES:K_AUG] = jnp.where(sel_blk == lane, 1.0, 0.0).astype(BF16)
    kw_ref[...] = rope(y[:, LANES:2 * LANES]).astype(BF16)

    y = proj(A_KM, MOBA_WIDTH)
    own_blk = jnp.where(lane == it, 1.0, 0.0).astype(BF16)
    sums = []
    for j in range(MOBA_WIDTH // LANES):
        kr = rope(y[:, j * LANES:(j + 1) * LANES])
        sums.append(jnp.mean(kr, axis=0, keepdims=True))
        kem_ref[:, j * K_AUG:j * K_AUG + LANES] = kr.astype(BF16)
        kem_ref[:, j * K_AUG + LANES:(j + 1) * K_AUG] = jnp.broadcast_to(own_blk, (tm, LANES))
    kmean_ref[...] = jnp.broadcast_to(jnp.concatenate(sums, axis=1), kmean_ref.shape)

    for part in range(4):
        w = 2 * D_MODEL // 4
        gm_ref[:, part * w:(part + 1) * w] = _sigmoid(proj(A_GM + part * w, w)).astype(BF16)

    cos_t = ct_ref[...]
    sin_t = st_ref[...]

    def proj_t(r0, rows):
        return _dot_nt(wb_ref[r0:r0 + rows, :], h_ref[...])

    def rope_t(yt):
        half = ROT_DIM // 2
        out = []
        for hd in range(yt.shape[0] // HEAD_DIM):
            a = yt[hd * HEAD_DIM:hd * HEAD_DIM + half, :]
            b = yt[hd * HEAD_DIM + half:hd * HEAD_DIM + ROT_DIM, :]
            out += [a * cos_t - b * sin_t, a * sin_t + b * cos_t,
                    yt[hd * HEAD_DIM + ROT_DIM:(hd + 1) * HEAD_DIM, :]]
        return jnp.concatenate(out, axis=0)

    y = proj_t(B_QN, NSA_WIDTH)
    qt_ref[0:NSA_WIDTH, :] = (y * SCALE).astype(BF16)
    qt_ref[NSA_WIDTH:2 * NSA_WIDTH, :] = (rope_t(y) * SCALE).astype(BF16)
    qmt_ref[...] = rope_t(proj_t(B_QM, MOBA_WIDTH))

    ones_row = jnp.where(lax.broadcasted_iota(jnp.int32, (V_ROWS - HEAD_DIM, tm), 0) == 0, 1.0, 0.0).astype(BF16)
    y = proj_t(B_V, 4 * HEAD_DIM)
    for g in range(NSA_KV_GROUPS):
        vts_ref[g, 0:HEAD_DIM, :] = y[g * HEAD_DIM:(g + 1) * HEAD_DIM, :].astype(BF16)
        vts_ref[g, HEAD_DIM:V_ROWS, :] = ones_row
        vtw_ref[g, 0:HEAD_DIM, :] = y[(2 + g) * HEAD_DIM:(3 + g) * HEAD_DIM, :].astype(BF16)
        vtw_ref[g, HEAD_DIM:V_ROWS, :] = ones_row
    y = proj_t(B_VM, MOBA_WIDTH)
    for hd in range(MOBA_HEADS):
        vtm_ref[hd, 0:HEAD_DIM, :] = y[hd * HEAD_DIM:(hd + 1) * HEAD_DIM, :].astype(BF16)
        vtm_ref[hd, HEAD_DIM:V_ROWS, :] = ones_row
    y = _sigmoid(proj_t(B_GN, NSA_KV_GROUPS * GATE_ROWS))
    for g in range(NSA_KV_GROUPS):
        gnt_ref[g] = y[g * GATE_ROWS:(g + 1) * GATE_ROWS, :]


def _in_proj(x2, g1, wa, wb, tabs, batch, seq):
    m = x2.shape[0]
    tm = KEY_BLOCK
    nt = seq // tm
    row = lambda i: (i, 0)
    const = lambda i: (0, 0)
    tab = lambda i: (i % nt, 0)
    tab_t = lambda i: (0, i % nt)
    feat = lambda i: (i // nt, 0, i % nt)
    blk4 = lambda i: (i // nt, i % nt, 0, 0)
    blk5 = lambda i: (i // nt, 0, i % nt, 0, 0)
    g = NSA_KV_GROUPS
    outs = [
        (jax.ShapeDtypeStruct((batch, 2 * NSA_WIDTH, seq), BF16), pl.BlockSpec((None, 2 * NSA_WIDTH, tm), feat)),
        (jax.ShapeDtypeStruct((batch, MOBA_WIDTH, seq), F32), pl.BlockSpec((None, MOBA_WIDTH, tm), feat)),
        (jax.ShapeDtypeStruct((batch, g, GATE_ROWS, seq), F32),
         pl.BlockSpec((None, g, GATE_ROWS, tm), lambda i: (i // nt, 0, 0, i % nt))),
        (jax.ShapeDtypeStruct((m, KV_WIDTH), F32), pl.BlockSpec((tm, KV_WIDTH), row)),
        (jax.ShapeDtypeStruct((m, KV_WIDTH), F32), pl.BlockSpec((tm, KV_WIDTH), row)),
        (jax.ShapeDtypeStruct((batch, nt, tm, K_AUG), BF16), pl.BlockSpec((None, None, tm, K_AUG), blk4)),
        (jax.ShapeDtypeStruct((batch, nt, tm, LANES), BF16), pl.BlockSpec((None, None, tm, LANES), blk4)),
        (jax.ShapeDtypeStruct((batch, g, nt, V_ROWS, tm), BF16), pl.BlockSpec((None, g, None, V_ROWS, tm), blk5)),
        (jax.ShapeDtypeStruct((batch, g, nt, V_ROWS, tm), BF16), pl.BlockSpec((None, g, None, V_ROWS, tm), blk5)),
        (jax.ShapeDtypeStruct((batch, nt, tm, (MOBA_HEADS // 2) * K_AUG), BF16),
         pl.BlockSpec((None, None, tm, (MOBA_HEADS // 2) * K_AUG), blk4)),
        (jax.ShapeDtypeStruct((batch, MOBA_HEADS, nt, V_ROWS, tm), BF16),
         pl.BlockSpec((None, MOBA_HEADS, None, V_ROWS, tm), blk5)),
        (jax.ShapeDtypeStruct((batch, nt * SUBLANES, MOBA_WIDTH), F32),
         pl.BlockSpec((None, SUBLANES, MOBA_WIDTH), lambda i: (i // nt, i % nt, 0))),
        (jax.ShapeDtypeStruct((m, 2 * D_MODEL), BF16), pl.BlockSpec((tm, 2 * D_MODEL), row)),
    ]
    return pl.pallas_call(
        functools.partial(_in_proj_body, tiles_per_seq=nt),
        grid=(m // tm,),
        in_specs=[pl.BlockSpec((tm, D_MODEL), row),
                  pl.BlockSpec((1, D_MODEL), const),
                  pl.BlockSpec((D_MODEL, A_WIDTH), const),
                  pl.BlockSpec((B_ROWS, D_MODEL), const),
                  pl.BlockSpec((tm, LANES), tab),
                  pl.BlockSpec((tm, LANES), tab),
                  pl.BlockSpec((tm, LANES), tab),
                  pl.BlockSpec((SUBLANES, tm), tab_t),
                  pl.BlockSpec((SUBLANES, tm), tab_t)],
        out_specs=[o[1] for o in outs],
        out_shape=[o[0] for o in outs],
        scratch_shapes=[pltpu.VMEM((tm, D_MODEL), BF16)],
        compiler_params=pltpu.CompilerParams(dimension_semantics=("parallel",),
                                             vmem_limit_bytes=VMEM_LIMIT),
        name="in_proj",
    )(x2, g1, wa, wb, *tabs)


def _gelu_tanh(x):
    return 0.5 * x * (1.0 + jnp.tanh(np.sqrt(2.0 / np.pi) * (x + 0.044715 * (x * x * x))))


def _compress_body(kx_ref, vx_ref, pek_ref, pev_ref, kw1_ref, vw1_ref, kw2_ref, vw2t_ref,
                   kc_ref, vct_ref):
    ncp = kc_ref.shape[0]

    def hidden(x_ref, pe_ref, w1_ref):
        ya = jnp.zeros((ncp, LANES), F32)
        yb = jnp.zeros((ncp, LANES), F32)
        pe = jnp.zeros((SUBLANES, LANES), F32)
        for r in range(CMP_STRIDE):
            xr = x_ref[pl.ds(r, ncp, stride=CMP_STRIDE), :].astype(BF16)
            ya = ya + _dot(xr, w1_ref[r])
            yb = yb + _dot(xr, w1_ref[CMP_STRIDE + r])
        for l in range(CMP_BLOCK):
            pe = pe + _dot(pe_ref[l], w1_ref[l])
        return _gelu_tanh(ya + pltpu.roll(yb, ncp - 1, 0) + pe[0:1, :])

    hk = hidden(kx_ref, pek_ref, kw1_ref)
    kc_ref[...] = _dot(hk.astype(BF16), kw2_ref[...]).astype(BF16)
    hv = hidden(vx_ref, pev_ref, vw1_ref)
    vct = _dot_nt(vw2t_ref[...], hv.astype(BF16)).astype(BF16)
    ones_row = jnp.where(lax.broadcasted_iota(jnp.int32, (V_ROWS - HEAD_DIM, ncp), 0) == 0, 1.0, 0.0).astype(BF16)
    for g in range(NSA_KV_GROUPS):
        vct_ref[g, 0:HEAD_DIM, :] = vct[g * HEAD_DIM:(g + 1) * HEAD_DIM, :]
        vct_ref[g, HEAD_DIM:V_ROWS, :] = ones_row


def _compress(kx, vx, pek, pev, kw1, vw1, kw2, vw2t):
    b, seq, _ = kx.shape
    ncp = seq // CMP_STRIDE
    g = NSA_KV_GROUPS
    tok = lambda bi: (bi, 0, 0)
    const3 = lambda bi: (0, 0, 0)
    const2 = lambda bi: (0, 0)
    return pl.pallas_call(
        _compress_body,
        grid=(b,),
        in_specs=[pl.BlockSpec((None, seq, LANES), tok),
                  pl.BlockSpec((None, seq, LANES), tok),
                  pl.BlockSpec((CMP_BLOCK, SUBLANES, LANES), const3),
                  pl.BlockSpec((CMP_BLOCK, SUBLANES, LANES), const3),
                  pl.BlockSpec((CMP_BLOCK, LANES, LANES), const3),
                  pl.BlockSpec((CMP_BLOCK, LANES, LANES), const3),
                  pl.BlockSpec((LANES, LANES), const2),
                  pl.BlockSpec((LANES, LANES), const2)],
        out_specs=[pl.BlockSpec((None, ncp, LANES), tok),
                   pl.BlockSpec((None, g, V_ROWS, ncp), lambda bi: (bi, 0, 0, 0))],
        out_shape=[jax.ShapeDtypeStruct((b, ncp, LANES), BF16),
                   jax.ShapeDtypeStruct((b, g, V_ROWS, ncp), BF16)],
        compiler_params=pltpu.CompilerParams(dimension_semantics=("parallel",)),
        name="compress",
    )(kx, vx, pek, pev, kw1, vw1, kw2, vw2t)


def _rank_rows(score, rank, k0, k1):
    rows = score.shape[0]
    j = lax.broadcasted_iota(jnp.int32, (SUBLANES, 1), 0)
    blocks = [score[b:b + SUBLANES, :] for b in range(0, rows, SUBLANES)]
    for k in range(k0, k1):
        rk = score[k:k + 1, :]
        before = lambda sb: jnp.where(rk > sb, 1.0, 0.0)
        unless = lambda sb: jnp.where(sb > rk, 0.0, 1.0)
        counts = []
        for b, sb in zip(range(0, rows, SUBLANES), blocks):
            if b + SUBLANES <= k:
                counts.append(before(sb))
            elif b > k:
                counts.append(unless(sb))
            else:
                counts.append(jnp.where(j + b > k, unless(sb), before(sb)))
        rank = rank + jnp.concatenate(counts, axis=0)
    return rank


def _topk_mask(score, n_rows, k_top):
    return _rank_rows(score, jnp.zeros(score.shape, F32), 0, n_rows) < k_top


def _finish(acc):
    return acc[0:HEAD_DIM, :] / acc[HEAD_DIM:HEAD_DIM + 1, :]


def _softmax_reset(m_ref, acc_ref):
    m_ref[...] = jnp.full(m_ref.shape, NEG, F32)
    acc_ref[...] = jnp.zeros(acc_ref.shape, F32)


def _park(parked, slot, sb):
    s_buf, bm_ref = parked
    s_buf[slot, 0:sb.shape[0], :] = sb
    bm_ref[slot] = jnp.max(sb, axis=0, keepdims=True)


def _softmax_update(m_ref, acc_ref, parked, slot, values):
    s_buf, bm_ref = parked
    m_old = m_ref[...]
    m_new = jnp.maximum(m_old, bm_ref[slot])
    pb = jnp.exp2(s_buf[slot] - m_new).astype(BF16)
    acc_ref[...] = jnp.exp2(m_old - m_new) * acc_ref[...] + values(pb)
    m_ref[...] = m_new


def _online_blocks(i, parked, m_ref, acc_ref, scores, values, causal):
    block = lambda kb: jnp.maximum(kb, 0)

    def park(slot, kb, live, mask=None):
        sb = scores(block(kb), live)
        _park(parked, slot, sb if mask is None else jnp.where(mask, sb, NEG))

    def update(slot, kb):
        _softmax_update(m_ref, acc_ref, parked, slot, lambda pb: values(block(kb), pb))

    _softmax_reset(m_ref, acc_ref)
    first = -((i + 1) & 1)
    n_pairs = (i + 2) // 2
    park(0, first, first >= 0)

    @pl.loop(0, n_pairs - 1)
    def _(p):
        ka = first + 2 * p
        park(1, ka + 1, True)
        update(0, ka)
        park(0, ka + 2, True)
        update(1, ka + 1)

    park(1, i, True, causal)
    update(0, i - 1)
    update(1, i)
    return acc_ref[...]


def _online_blocks_diag_first(i, parked, m_ref, acc_ref, gate, diag_scores, select, values, causal):
    past = lambda x: jnp.clip(x - 1, 0, jnp.maximum(i - 1, 0))

    def park(slot, x):
        _park(parked, slot, scores(past(x), x - 1 < i))

    def update(slot, kb):
        _softmax_update(m_ref, acc_ref, parked, slot, lambda pb: values(kb, pb))

    _softmax_reset(m_ref, acc_ref)
    gates = gate()
    _park(parked, 0, jnp.where(causal, diag_scores(), NEG))
    scores = select(gates)
    park(1, 1)
    update(0, i)
    park(0, 2)
    update(1, past(1))

    @pl.loop(1, (i + 2) // 2)
    def _(p):
        park(1, 2 * p + 1)
        update(0, past(2 * p))
        park(0, 2 * p + 2)
        update(1, past(2 * p + 1))

    return acc_ref[...]


def _nsa_body(q_ref, qr_ref, gn_ref, kc_ref, vct_ref, ke_ref, vts_ref, kw_ref, vtw_ref, ovt_ref,
              y_ref, s_buf, bm_ref, m_ref, acc_ref, rank_ref, *, n_sel_blocks):
    tq = q_ref.shape[1]
    ng, rg = NSA_KV_GROUPS, NSA_GROUP_SIZE
    nh = ng * rg
    i = pl.program_id(1)
    t0 = i * tq
    head = lambda a, h: a[h * HEAD_DIM:(h + 1) * HEAD_DIM, :]
    cols = lambda a, h, n=1: a[:, h * tq:(h + n) * tq]
    q = q_ref[...]
    qr = qr_ref[...]
    tpos = t0 + lax.broadcasted_iota(jnp.int32, (1, tq), 1)
    tpos_n = jnp.concatenate([tpos] * nh, axis=1)

    zero = jnp.zeros((HEAD_DIM, tq), BF16)
    place = lambda h, a: jnp.concatenate([a, zero] if h < rg else [zero, a], axis=0)
    qc = jnp.concatenate([place(h, head(q, h)) for h in range(nh)], axis=1)
    qw = jnp.concatenate([place(h, head(qr, h)) for h in range(nh)], axis=1)
    per_group = lambda v_ref, kb, pb: jnp.concatenate(
        [_dot(v_ref[g, kb], cols(pb, g * rg, rg)) for g in range(ng)], axis=1)

    causal = (lax.broadcasted_iota(jnp.int32, (KEY_BLOCK, 1), 0)
              <= (lax.broadcasted_iota(jnp.int32, (1, nh * tq), 1) & (tq - 1)))

    ncp = kc_ref.shape[0]
    i1 = jnp.maximum(i - 1, 0)
    i2 = jnp.maximum(i - 2, 0)
    n = lax.broadcasted_iota(jnp.int32, (ncp, 1), 0)
    parked = (s_buf, bm_ref)
    _park(parked, 0, jnp.where((n * CMP_STRIDE + (CMP_BLOCK - 1)) <= tpos_n, _dot(kc_ref[...], qc), NEG))
    _park(parked, 1, jnp.where(causal, _dot(kw_ref[i], qw), NEG))

    p = jnp.exp2(s_buf[0, 0:ncp, :] - bm_ref[0])
    pb16 = p.astype(BF16)
    acc = jnp.concatenate([_dot(vct_ref[g], cols(pb16, g * rg, rg)) for g in range(ng)], axis=1)
    inv = jnp.where(tpos_n >= CMP_BLOCK - 1, 1.0 / jnp.maximum(acc[HEAD_DIM:HEAD_DIM + 1, :], 1e-30), 0.0)
    o_cmp = acc[0:HEAD_DIM, :] * inv

    pn = p * inv
    ps = jnp.concatenate([sum(cols(pn, g * rg + r) for r in range(1, rg)) + cols(pn, g * rg)
                          for g in range(ng)], axis=1)
    ps_hi = ps.astype(BF16)
    ps_lo = (ps - ps_hi.astype(F32)).astype(BF16)
    imp = _dot(ovt_ref[...], ps_hi) + _dot(ovt_ref[...], ps_lo)
    jb = lax.broadcasted_iota(jnp.int32, (imp.shape[0], 1), 0)
    cur = jnp.concatenate([tpos >> (SEL_BLOCK.bit_length() - 1)] * ng, axis=1)
    forced = (jb == 0) | (jb == cur) | (jb == cur - 1)
    imp = jnp.where(forced, jnp.inf, imp)
    imp = jnp.where(jb <= cur, imp, -jnp.inf)

    _park(parked, 0, jnp.where(i >= 1, _dot(kw_ref[i1], qw), NEG))
    _softmax_reset(m_ref, acc_ref)
    _softmax_update(m_ref, acc_ref, parked, 1, lambda pb: per_group(vtw_ref, i, pb))
    _park(parked, 1, jnp.where(jnp.logical_or(causal, i < 2), NEG, _dot(kw_ref[i2], qw)))
    _softmax_update(m_ref, acc_ref, parked, 0, lambda pb: per_group(vtw_ref, i1, pb))
    _softmax_update(m_ref, acc_ref, parked, 1, lambda pb: per_group(vtw_ref, i2, pb))
    o_win = _finish(acc_ref[...])

    quarter = n_sel_blocks // 4
    last_blk = (tq // SEL_BLOCK) * (i + 1) - 1
    rank_ref[...] = _rank_rows(imp, jnp.zeros(imp.shape, F32), 0, quarter)
    for c in range(1, 4):
        @pl.when(last_blk >= c * quarter)
        def _():
            rank_ref[...] = _rank_rows(imp, rank_ref[...], c * quarter, (c + 1) * quarter)
    rank = rank_ref[...]
    bias = jnp.where(rank < min(SEL_TOPN, n_sel_blocks), 0.0, NEG).astype(BF16)

    off = jnp.full((bias.shape[0], tq), NEG, BF16)
    aug = lambda b_rows: jnp.concatenate(
        [jnp.concatenate([cols(qw, h), b_rows(h // rg), zero], axis=0) for h in range(nh)], axis=1)
    qa = aug(lambda g: cols(bias, g))
    qa_off = aug(lambda g: off)
    o_sel = _finish(_online_blocks(
        i, parked, m_ref, acc_ref,
        lambda kb, live: _dot(ke_ref[kb], qa if live is True else jnp.where(live, qa, qa_off)),
        lambda kb, pb: per_group(vts_ref, kb, pb),
        causal))

    for h in range(nh):
        gates = gn_ref[h // rg]
        r = h % rg
        y_ref[h * HEAD_DIM:(h + 1) * HEAD_DIM, :] = (
            gates[3 * r:3 * r + 1, :] * cols(o_cmp, h)
            + gates[3 * r + 1:3 * r + 2, :] * cols(o_sel, h)
            + gates[3 * r + 2:3 * r + 3, :] * cols(o_win, h)).astype(y_ref.dtype)


def _nsa(qt, gnt, kc, vct, ke, vts, kw, vtw, ovt):
    b, _, seq = qt.shape
    g = NSA_KV_GROUPS
    nkb = seq // KEY_BLOCK
    ncp = kc.shape[1]
    tq = Q_TILE
    n_cols = NSA_HEADS * tq
    whole5 = lambda bi, i: (bi, 0, 0, 0, 0)
    whole4 = lambda bi, i: (bi, 0, 0, 0)
    return pl.pallas_call(
        functools.partial(_nsa_body, n_sel_blocks=seq // SEL_BLOCK),
        grid=(b, seq // tq),
        in_specs=[pl.BlockSpec((None, NSA_WIDTH, tq), lambda bi, i: (bi, 0, i)),
                  pl.BlockSpec((None, NSA_WIDTH, tq), lambda bi, i: (bi, 1, i)),
                  pl.BlockSpec((None, g, GATE_ROWS, tq), lambda bi, i: (bi, 0, 0, i)),
                  pl.BlockSpec((None, ncp, LANES), lambda bi, i: (bi, 0, 0)),
                  pl.BlockSpec((None, g, V_ROWS, ncp), whole4),
                  pl.BlockSpec((None, nkb, KEY_BLOCK, K_AUG), whole4),
                  pl.BlockSpec((None, g, nkb, V_ROWS, KEY_BLOCK), whole5),
                  pl.BlockSpec((None, nkb, KEY_BLOCK, LANES), whole4),
                  pl.BlockSpec((None, g, nkb, V_ROWS, KEY_BLOCK), whole5),
                  pl.BlockSpec((HEAD_DIM, ncp), lambda bi, i: (0, 0))],
        out_specs=pl.BlockSpec((None, NSA_WIDTH, tq), lambda bi, i: (bi, 0, i)),
        out_shape=jax.ShapeDtypeStruct((b, NSA_WIDTH, seq), BF16),
        scratch_shapes=[pltpu.VMEM((2, KEY_BLOCK, n_cols), F32),
                        pltpu.VMEM((2, 1, n_cols), F32),
                        pltpu.VMEM((1, n_cols), F32),
                        pltpu.VMEM((V_ROWS, n_cols), F32),
                        pltpu.VMEM((HEAD_DIM, g * tq), F32)],
        compiler_params=pltpu.CompilerParams(
            dimension_semantics=("parallel", "arbitrary"), vmem_limit_bytes=VMEM_LIMIT),
        name="nsa",
    )(qt, qt, gnt, kc, vct, ke, vts, kw, vtw, ovt)


def _moba_body(q_ref, km_ref, ke_ref, vt_ref, y_ref, s_buf, bm_ref, m_ref, acc_ref, *, n_blocks):
    tq = q_ref.shape[1]
    nh = km_ref.shape[0]
    i = pl.program_id(2)
    q = q_ref[...] * SCALE
    qb = q.astype(BF16)
    head = lambda a, h: a[h * HEAD_DIM:(h + 1) * HEAD_DIM, :]
    zero = jnp.zeros((HEAD_DIM, tq), BF16)
    n_bias = km_ref.shape[1]
    pad = jnp.zeros((K_AUG - LANES - n_bias, tq), BF16)

    def aug(h, b_rows):
        pair = [head(qb, h), zero] if h % 2 == 0 else [zero, head(qb, h)]
        return jnp.concatenate(pair + [b_rows, pad], axis=0)

    def keys(kb, h):
        return ke_ref[kb, :, (h // 2) * K_AUG:(h // 2 + 1) * K_AUG]

    def diag_scores():
        open_rows = jnp.zeros((n_bias, tq), BF16)
        return jnp.concatenate([_dot(keys(i, h), aug(h, open_rows)) for h in range(nh)], axis=1)

    def gate():
        q_lo = (q - qb.astype(F32)).astype(BF16)
        gates = []
        for h in range(nh):
            km = km_ref[h]
            km_hi = km.astype(BF16)
            km_lo = (km - km_hi.astype(F32)).astype(BF16)
            gates.append(_dot(km_hi, head(qb, h)) + (_dot(km_hi, head(q_lo, h)) + _dot(km_lo, head(qb, h))))
        return jnp.concatenate(gates, axis=1)

    def select(gate):
        jb = lax.broadcasted_iota(jnp.int32, (gate.shape[0], 1), 0)
        gate = jnp.where(jb < i, gate, -jnp.inf)
        sel = _topk_mask(gate, n_blocks, min(MOBA_TOPK, n_blocks)) & (gate > -jnp.inf)
        bias = jnp.where(sel, 0.0, NEG).astype(BF16)
        off = jnp.full((n_bias, tq), NEG, BF16)
        qa = [aug(h, bias[:, h * tq:(h + 1) * tq]) for h in range(nh)]
        qa_off = [aug(h, off) for h in range(nh)]
        return lambda kb, live: jnp.concatenate(
            [_dot(keys(kb, h), jnp.where(live, qa[h], qa_off[h])) for h in range(nh)], axis=1)

    def values(kb, pb):
        return jnp.concatenate([_dot(vt_ref[h, kb], pb[:, h * tq:(h + 1) * tq]) for h in range(nh)], axis=1)

    causal = (lax.broadcasted_iota(jnp.int32, (KEY_BLOCK, 1), 0)
              <= (lax.broadcasted_iota(jnp.int32, (1, nh * tq), 1) & (tq - 1)))
    o = _finish(_online_blocks_diag_first(i, (s_buf, bm_ref), m_ref, acc_ref,
                                          gate, diag_scores, select, values, causal))
    for h in range(nh):
        y_ref[h * HEAD_DIM:(h + 1) * HEAD_DIM, :] = o[:, h * tq:(h + 1) * tq].astype(y_ref.dtype)


def _moba(qmt, kmean, ke, vt):
    b, _, seq = qmt.shape
    nkb = seq // KEY_BLOCK
    tq = MOBA_BLOCK
    nh = MOBA_HEADS_PER_STEP
    return pl.pallas_call(
        functools.partial(_moba_body, n_blocks=nkb),
        grid=(b, MOBA_HEADS // nh, seq // tq),
        in_specs=[pl.BlockSpec((None, nh * HEAD_DIM, tq), lambda bi, hi, i: (bi, hi, i)),
                  pl.BlockSpec((None, nh, 16, HEAD_DIM), lambda bi, hi, i: (bi, hi, 0, 0)),
                  pl.BlockSpec((None, nkb, KEY_BLOCK, (nh // 2) * K_AUG), lambda bi, hi, i: (bi, 0, 0, hi)),
                  pl.BlockSpec((None, nh, nkb, V_ROWS, KEY_BLOCK), lambda bi, hi, i: (bi, hi, 0, 0, 0))],
        out_specs=pl.BlockSpec((None, nh * HEAD_DIM, tq), lambda bi, hi, i: (bi, hi, i)),
        out_shape=jax.ShapeDtypeStruct((b, MOBA_WIDTH, seq), BF16),
        scratch_shapes=[pltpu.VMEM((2, KEY_BLOCK, nh * tq), F32),
                        pltpu.VMEM((2, 1, nh * tq), F32),
                        pltpu.VMEM((1, nh * tq), F32),
                        pltpu.VMEM((V_ROWS, nh * tq), F32)],
        compiler_params=pltpu.CompilerParams(
            dimension_semantics=("parallel", "parallel", "arbitrary"), vmem_limit_bytes=VMEM_LIMIT),
        name="moba",
    )(qmt, kmean, ke, vt)


def _merge_ffn_body(x_ref, yn_ref, ym_ref, gm_ref, wun_ref, wum_ref, wo_ref, g2_ref,
                    w1_ref, w2_ref, gf_ref, o_ref):
    tn = (((0,), (0,)), ((), ()))
    a = lax.dot_general(yn_ref[...], wun_ref[...], tn, preferred_element_type=F32)
    c = lax.dot_general(ym_ref[...], wum_ref[...], tn, preferred_element_type=F32)
    mixed = gm_ref[:, 0:D_MODEL] * a + gm_ref[:, D_MODEL:2 * D_MODEL] * c
    x1 = x_ref[...] + _dot(mixed.astype(BF16), wo_ref[...])
    ms = jnp.mean(x1 * x1, axis=-1, keepdims=True)
    h2 = (x1 * lax.rsqrt(ms + NORM_EPS) * g2_ref[...]).astype(BF16)
    x2 = x1
    for k in range(D_FF // FF_CHUNK):
        u = jnp.maximum(_dot(h2, w1_ref[:, k * FF_CHUNK:(k + 1) * FF_CHUNK]), 0.0)
        x2 = x2 + _dot((u * u).astype(BF16), w2_ref[k * FF_CHUNK:(k + 1) * FF_CHUNK, :])
    ms = jnp.mean(x2 * x2, axis=-1, keepdims=True)
    o_ref[...] = x2 * lax.rsqrt(ms + NORM_EPS) * gf_ref[...]


def _merge_ffn(x2, ynt, ymt, gm, wun, wum, wo, g2, w1, w2, gf):
    m = x2.shape[0]
    seq = ynt.shape[2]
    tm = 512
    nt = seq // tm
    row = lambda i: (i, 0)
    const = lambda i: (0, 0)
    feat = lambda i: (i // nt, 0, i % nt)
    resident = lambda shape: pl.BlockSpec(shape, const, pipeline_mode=pl.Buffered(1))
    return pl.pallas_call(
        _merge_ffn_body,
        grid=(m // tm,),
        in_specs=[pl.BlockSpec((tm, D_MODEL), row),
                  pl.BlockSpec((None, NSA_WIDTH, tm), feat),
                  pl.BlockSpec((None, MOBA_WIDTH, tm), feat),
                  pl.BlockSpec((tm, 2 * D_MODEL), row),
                  resident((NSA_WIDTH, D_MODEL)),
                  resident((MOBA_WIDTH, D_MODEL)),
                  resident((D_MODEL, D_MODEL)),
                  pl.BlockSpec((1, D_MODEL), const),
                  resident((D_MODEL, D_FF)),
                  resident((D_FF, D_MODEL)),
                  pl.BlockSpec((1, D_MODEL), const)],
        out_specs=pl.BlockSpec((tm, D_MODEL), row),
        out_shape=jax.ShapeDtypeStruct((m, D_MODEL), F32),
        compiler_params=pltpu.CompilerParams(dimension_semantics=("parallel",),
                                             vmem_limit_bytes=VMEM_LIMIT),
        name="merge_ffn",
    )(x2, ynt, ymt, gm, wun, wum, wo, g2, w1, w2, gf)


def _rope_tables(seq):
    half = ROT_DIM // 2
    inv = ROPE_THETA ** (-jnp.arange(0, ROT_DIM, 2, dtype=F32) / ROT_DIM)
    ang = jnp.arange(seq, dtype=F32)[:, None] * inv[None, :]
    cos, sin = jnp.cos(ang), jnp.sin(ang)
    pad = HEAD_DIM - ROT_DIM
    one_head = lambda a, b, fill: jnp.concatenate([a, b, jnp.full((seq, pad), fill, F32)], axis=1)
    zeros = jnp.zeros((seq, half), F32)
    two = lambda a: jnp.concatenate([a, a], axis=1)
    return (two(one_head(cos, cos, 1.0)),
            two(one_head(-sin, zeros, 0.0)),
            two(one_head(zeros, sin, 0.0)),
            cos.T, sin.T)


def _overlap_t(ncp, nsb):
    i = np.arange(ncp)[None, :]
    j = np.arange(HEAD_DIM)[:, None]
    start = i * CMP_STRIDE
    end = start + CMP_BLOCK - 1
    ov = (end >= j * SEL_BLOCK) & (start <= j * SEL_BLOCK + SEL_BLOCK - 1) & (i < ncp - 1) & (j < nsb)
    return jnp.asarray(ov.astype(np.float32), dtype=BF16)


def _block_diag(w):
    z = jnp.zeros_like(w)
    return jnp.concatenate([jnp.concatenate([w, z], axis=-1), jnp.concatenate([z, w], axis=-1)], axis=-2)


def _projection_weights(w):
    kv = lambda j: w[:, OFF_KV + j * KV_WIDTH:OFF_KV + (j + 1) * KV_WIDTH]
    mo = lambda j: w[:, OFF_M + j * MOBA_WIDTH:OFF_M + (j + 1) * MOBA_WIDTH]
    wa = jnp.concatenate([kv(0), kv(1), kv(2), kv(4), mo(1), w[:, OFF_GM:]], axis=1)
    gn = w[:, OFF_GN:OFF_M].T.reshape(NSA_KV_GROUPS, 3 * NSA_GROUP_SIZE, D_MODEL)
    gn = jnp.concatenate([gn, jnp.zeros((NSA_KV_GROUPS, GATE_ROWS - 3 * NSA_GROUP_SIZE, D_MODEL), w.dtype)], axis=1)
    wb = jnp.concatenate([w[:, :OFF_KV].T, mo(0).T, kv(3).T, kv(5).T, mo(2).T,
                          gn.reshape(NSA_KV_GROUPS * GATE_ROWS, D_MODEL)], axis=0)
    return wa.astype(BF16), wb.astype(BF16)


def _mixers(x, norm1_g, w_in, cmp_pe_k, cmp_pe_v, cmp_k_w1, cmp_k_w2, cmp_v_w1, cmp_v_w2):
    b, seq, _ = x.shape
    g, d = NSA_KV_GROUPS, HEAD_DIM
    nkb = seq // KEY_BLOCK
    ncp = seq // CMP_STRIDE
    nsb = seq // SEL_BLOCK
    layer = 0

    wa, wb = _projection_weights(w_in[layer])
    x2 = x.reshape(b * seq, D_MODEL)
    (qt, qmt, gnt, kcx, vcx, ke, kw, vts, vtw, kem, vtm, kmean8, ogm) = _in_proj(
        x2, norm1_g[layer][None, :], wa, wb, _rope_tables(seq), b, seq)

    per_group = lambda a: jnp.concatenate([a] * g, axis=-1)
    pe_rows = lambda pe: jnp.broadcast_to(per_group(pe)[:, None, :], (CMP_BLOCK, SUBLANES, LANES)).astype(BF16)
    w1_blocks = lambda w1: _block_diag(w1.reshape(CMP_BLOCK, d, d)).astype(BF16)
    kc, vct = _compress(
        kcx.reshape(b, seq, KV_WIDTH), vcx.reshape(b, seq, KV_WIDTH),
        pe_rows(cmp_pe_k[layer]), pe_rows(cmp_pe_v[layer]),
        w1_blocks(cmp_k_w1[layer]), w1_blocks(cmp_v_w1[layer]),
        _block_diag(cmp_k_w2[layer]).astype(BF16), _block_diag(cmp_v_w2[layer].T).astype(BF16))
    ynt = _nsa(qt, gnt, kc, vct, ke, vts, kw, vtw, _overlap_t(ncp, nsb))

    kmean = kmean8[:, ::SUBLANES, :].reshape(b, nkb, MOBA_HEADS, d)
    kmean = jnp.swapaxes(kmean, 1, 2)
    kmean = jnp.concatenate([kmean, jnp.zeros((b, MOBA_HEADS, 16 - nkb, d), F32)], axis=2)
    ymt = _moba(qmt, kmean, kem, vtm)
    return x2, ynt, ymt, ogm


def kernel(x, norm1_g, w_in, cmp_pe_k, cmp_pe_v, cmp_k_w1, cmp_k_w2, cmp_v_w1, cmp_v_w2,
           w_up_nsa, w_up_moba, w_out, norm2_g, w_ff1, w_ff2, norm_f_g):
    b, seq, _ = x.shape
    layer = 0
    x2, ynt, ymt, ogm = _mixers(x, norm1_g, w_in, cmp_pe_k, cmp_pe_v, cmp_k_w1, cmp_k_w2, cmp_v_w1, cmp_v_w2)
    out = _merge_ffn(x2, ynt, ymt, ogm, w_up_nsa[layer].astype(BF16), w_up_moba[layer].astype(BF16),
                     w_out[layer].astype(BF16), norm2_g[layer][None, :],
                     w_ff1[layer].astype(BF16), w_ff2[layer].astype(BF16), norm_f_g[None, :])
    return out.reshape(b, seq, D_MODEL)
```

```python
import functools

import numpy as np
import jax
import jax.numpy as jnp
from jax import lax
from jax.experimental import pallas as pl
from jax.experimental.pallas import tpu as pltpu

F32 = jnp.float32
BF16 = jnp.bfloat16

D_MODEL = 1024
HEAD_DIM = 64
ROT_DIM = HEAD_DIM // 4
ROPE_THETA = 500000.0
NORM_EPS = 1e-6

NSA_HEADS = 8
NSA_KV_GROUPS = 2
NSA_GROUP_SIZE = NSA_HEADS // NSA_KV_GROUPS
CMP_BLOCK = 32
CMP_STRIDE = 16
SEL_BLOCK = 64
SEL_TOPN = 16
WINDOW = 512

MOBA_HEADS = 8
MOBA_BLOCK = 256
MOBA_TOPK = 3

D_FF = 4 * D_MODEL
NSA_WIDTH = NSA_HEADS * HEAD_DIM
KV_WIDTH = NSA_KV_GROUPS * HEAD_DIM
MOBA_WIDTH = MOBA_HEADS * HEAD_DIM
OFF_KV = NSA_WIDTH
OFF_GN = OFF_KV + 6 * KV_WIDTH
OFF_M = OFF_GN + 3 * NSA_HEADS
OFF_GM = OFF_M + 3 * MOBA_WIDTH
IN_WIDTH = OFF_GM + 2 * D_MODEL

LANES = 128
SUBLANES = 8
KEY_BLOCK = 256
Q_TILE = 256
V_ROWS = 80
K_AUG = 2 * LANES
NEG = -1e30
SCALE = float(HEAD_DIM ** -0.5 * np.log2(np.e))
GATE_ROWS = 16
FF_CHUNK = 1024
VMEM_LIMIT = 56 * 1024 * 1024

A_KC, A_KSW, A_KM, A_GM = 0, 256, 512, 1024
A_WIDTH = A_GM + 2 * D_MODEL
B_QN, B_QM, B_V, B_VM, B_GN = 0, 512, 1024, 1280, 1792
B_ROWS = B_GN + NSA_KV_GROUPS * GATE_ROWS


def _dot(a, b):
    return jnp.dot(a, b, preferred_element_type=F32)


def _dot_nt(a, b):
    return lax.dot_general(a, b, (((1,), (1,)), ((), ())), preferred_element_type=F32)


def _sigmoid(y):
    return 1.0 / (1.0 + jnp.exp(-y))


def _in_proj_body(x_ref, g_ref, wa_ref, wb_ref, c_ref, s1_ref, s2_ref, ct_ref, st_ref,
                  qt_ref, qmt_ref, gnt_ref, kcx_ref, vcx_ref, ke_ref, kw_ref, vts_ref, vtw_ref,
                  kem_ref, vtm_ref, kmean_ref, gm_ref, h_ref, *, tiles_per_seq):
    tm = x_ref.shape[0]
    it = pl.program_id(0) % tiles_per_seq
    x = x_ref[...]
    ms = jnp.mean(x * x, axis=-1, keepdims=True)
    h_ref[...] = (x * lax.rsqrt(ms + NORM_EPS) * g_ref[...]).astype(BF16)

    cos = c_ref[...]
    sin_hi = s1_ref[...]
    sin_lo = s2_ref[...]

    def proj(c0, width):
        return _dot(h_ref[...], wa_ref[:, c0:c0 + width])

    def rope(yc):
        return (yc * cos + pltpu.roll(yc, LANES - ROT_DIM // 2, 1) * sin_hi
                + pltpu.roll(yc, ROT_DIM // 2, 1) * sin_lo)

    y = proj(A_KC, 2 * KV_WIDTH)
    kcx_ref[...] = y[:, 0:KV_WIDTH]
    vcx_ref[...] = y[:, KV_WIDTH:2 * KV_WIDTH]

    row = lax.broadcasted_iota(jnp.int32, (tm, 1), 0)
    lane = lax.broadcasted_iota(jnp.int32, (1, LANES), 1)
    y = proj(A_KSW, 2 * KV_WIDTH)
    ke_ref[:, 0:LANES] = rope(y[:, 0:LANES]).astype(BF16)
    sel_blk = (it * tm + row) >> (SEL_BLOCK.bit_length() - 1)
    ke_ref[:, LANES:K_AUG] = jnp.where(sel_blk == lane, 1.0, 0.0).astype(BF16)
    kw_ref[...] = rope(y[:, LANES:2 * LANES]).astype(BF16)

    y = proj(A_KM, MOBA_WIDTH)
    own_blk = jnp.where(lane == it, 1.0, 0.0).astype(BF16)
    sums = []
    for j in range(MOBA_WIDTH // LANES):
        kr = rope(y[:, j * LANES:(j + 1) * LANES])
        sums.append(jnp.mean(kr, axis=0, keepdims=True))
        kem_ref[:, j * K_AUG:j * K_AUG + LANES] = kr.astype(BF16)
        kem_ref[:, j * K_AUG + LANES:(j + 1) * K_AUG] = jnp.broadcast_to(own_blk, (tm, LANES))
    kmean_ref[...] = jnp.broadcast_to(jnp.concatenate(sums, axis=1), kmean_ref.shape)

    for part in range(4):
        w = 2 * D_MODEL // 4
        gm_ref[:, part * w:(part + 1) * w] = _sigmoid(proj(A_GM + part * w, w)).astype(BF16)

    cos_t = ct_ref[...]
    sin_t = st_ref[...]

    def proj_t(r0, rows):
        return _dot_nt(wb_ref[r0:r0 + rows, :], h_ref[...])

    def rope_t(yt):
        half = ROT_DIM // 2
        out = []
        for hd in range(yt.shape[0] // HEAD_DIM):
            a = yt[hd * HEAD_DIM:hd * HEAD_DIM + half, :]
            b = yt[hd * HEAD_DIM + half:hd * HEAD_DIM + ROT_DIM, :]
            out += [a * cos_t - b * sin_t, a * sin_t + b * cos_t,
                    yt[hd * HEAD_DIM + ROT_DIM:(hd + 1) * HEAD_DIM, :]]
        return jnp.concatenate(out, axis=0)

    y = proj_t(B_QN, NSA_WIDTH)
    qt_ref[0:NSA_WIDTH, :] = (y * SCALE).astype(BF16)
    qt_ref[NSA_WIDTH:2 * NSA_WIDTH, :] = (rope_t(y) * SCALE).astype(BF16)
    qmt_ref[...] = rope_t(proj_t(B_QM, MOBA_WIDTH))

    ones_row = jnp.where(lax.broadcasted_iota(jnp.int32, (V_ROWS - HEAD_DIM, tm), 0) == 0, 1.0, 0.0).astype(BF16)
    y = proj_t(B_V, 4 * HEAD_DIM)
    for g in range(NSA_KV_GROUPS):
        vts_ref[g, 0:HEAD_DIM, :] = y[g * HEAD_DIM:(g + 1) * HEAD_DIM, :].astype(BF16)
        vts_ref[g, HEAD_DIM:V_ROWS, :] = ones_row
        vtw_ref[g, 0:HEAD_DIM, :] = y[(2 + g) * HEAD_DIM:(3 + g) * HEAD_DIM, :].astype(BF16)
        vtw_ref[g, HEAD_DIM:V_ROWS, :] = ones_row
    y = proj_t(B_VM, MOBA_WIDTH)
    for hd in range(MOBA_HEADS):
        vtm_ref[hd, 0:HEAD_DIM, :] = y[hd * HEAD_DIM:(hd + 1) * HEAD_DIM, :].astype(BF16)
        vtm_ref[hd, HEAD_DIM:V_ROWS, :] = ones_row
    y = _sigmoid(proj_t(B_GN, NSA_KV_GROUPS * GATE_ROWS))
    for g in range(NSA_KV_GROUPS):
        gnt_ref[g] = y[g * GATE_ROWS:(g + 1) * GATE_ROWS, :]


def _in_proj(x2, g1, wa, wb, tabs, batch, seq):
    m = x2.shape[0]
    tm = KEY_BLOCK
    nt = seq // tm
    row = lambda i: (i, 0)
    const = lambda i: (0, 0)
    tab = lambda i: (i % nt, 0)
    tab_t = lambda i: (0, i % nt)
    feat = lambda i: (i // nt, 0, i % nt)
    blk4 = lambda i: (i // nt, i % nt, 0, 0)
    blk5 = lambda i: (i // nt, 0, i % nt, 0, 0)
    g = NSA_KV_GROUPS
    outs = [
        (jax.ShapeDtypeStruct((batch, 2 * NSA_WIDTH, seq), BF16), pl.BlockSpec((None, 2 * NSA_WIDTH, tm), feat)),
        (jax.ShapeDtypeStruct((batch, MOBA_WIDTH, seq), F32), pl.BlockSpec((None, MOBA_WIDTH, tm), feat)),
        (jax.ShapeDtypeStruct((batch, g, GATE_ROWS, seq), F32),
         pl.BlockSpec((None, g, GATE_ROWS, tm), lambda i: (i // nt, 0, 0, i % nt))),
        (jax.ShapeDtypeStruct((m, KV_WIDTH), F32), pl.BlockSpec((tm, KV_WIDTH), row)),
        (jax.ShapeDtypeStruct((m, KV_WIDTH), F32), pl.BlockSpec((tm, KV_WIDTH), row)),
        (jax.ShapeDtypeStruct((batch, nt, tm, K_AUG), BF16), pl.BlockSpec((None, None, tm, K_AUG), blk4)),
        (jax.ShapeDtypeStruct((batch, nt, tm, LANES), BF16), pl.BlockSpec((None, None, tm, LANES), blk4)),
        (jax.ShapeDtypeStruct((batch, g, nt, V_ROWS, tm), BF16), pl.BlockSpec((None, g, None, V_ROWS, tm), blk5)),
        (jax.ShapeDtypeStruct((batch, g, nt, V_ROWS, tm), BF16), pl.BlockSpec((None, g, None, V_ROWS, tm), blk5)),
        (jax.ShapeDtypeStruct((batch, nt, tm, (MOBA_HEADS // 2) * K_AUG), BF16),
         pl.BlockSpec((None, None, tm, (MOBA_HEADS // 2) * K_AUG), blk4)),
        (jax.ShapeDtypeStruct((batch, MOBA_HEADS, nt, V_ROWS, tm), BF16),
         pl.BlockSpec((None, MOBA_HEADS, None, V_ROWS, tm), blk5)),
        (jax.ShapeDtypeStruct((batch, nt * SUBLANES, MOBA_WIDTH), F32),
         pl.BlockSpec((None, SUBLANES, MOBA_WIDTH), lambda i: (i // nt, i % nt, 0))),
        (jax.ShapeDtypeStruct((m, 2 * D_MODEL), BF16), pl.BlockSpec((tm, 2 * D_MODEL), row)),
    ]
    return pl.pallas_call(
        functools.partial(_in_proj_body, tiles_per_seq=nt),
        grid=(m // tm,),
        in_specs=[pl.BlockSpec((tm, D_MODEL), row),
                  pl.BlockSpec((1, D_MODEL), const),
                  pl.BlockSpec((D_MODEL, A_WIDTH), const),
                  pl.BlockSpec((B_ROWS, D_MODEL), const),
                  pl.BlockSpec((tm, LANES), tab),
                  pl.BlockSpec((tm, LANES), tab),
                  pl.BlockSpec((tm, LANES), tab),
                  pl.BlockSpec((SUBLANES, tm), tab_t),
                  pl.BlockSpec((SUBLANES, tm), tab_t)],
        out_specs=[o[1] for o in outs],
        out_shape=[o[0] for o in outs],
        scratch_shapes=[pltpu.VMEM((tm, D_MODEL), BF16)],
        compiler_params=pltpu.CompilerParams(dimension_semantics=("parallel",),
                                             vmem_limit_bytes=VMEM_LIMIT),
        name="in_proj",
    )(x2, g1, wa, wb, *tabs)


def _gelu_tanh(x):
    return 0.5 * x * (1.0 + jnp.tanh(np.sqrt(2.0 / np.pi) * (x + 0.044715 * (x * x * x))))


def _compress_body(kx_ref, vx_ref, pek_ref, pev_ref, kw1_ref, vw1_ref, kw2_ref, vw2t_ref,
                   kc_ref, vct_ref):
    ncp = kc_ref.shape[0]

    def hidden(x_ref, pe_ref, w1_ref):
        ya = jnp.zeros((ncp, LANES), F32)
        yb = jnp.zeros((ncp, LANES), F32)
        pe = jnp.zeros((SUBLANES, LANES), F32)
        for r in range(CMP_STRIDE):
            xr = x_ref[pl.ds(r, ncp, stride=CMP_STRIDE), :].astype(BF16)
            ya = ya + _dot(xr, w1_ref[r])
            yb = yb + _dot(xr, w1_ref[CMP_STRIDE + r])
        for l in range(CMP_BLOCK):
            pe = pe + _dot(pe_ref[l], w1_ref[l])
        return _gelu_tanh(ya + pltpu.roll(yb, ncp - 1, 0) + pe[0:1, :])

    hk = hidden(kx_ref, pek_ref, kw1_ref)
    kc_ref[...] = _dot(hk.astype(BF16), kw2_ref[...]).astype(BF16)
    hv = hidden(vx_ref, pev_ref, vw1_ref)
    vct = _dot_nt(vw2t_ref[...], hv.astype(BF16)).astype(BF16)
    ones_row = jnp.where(lax.broadcasted_iota(jnp.int32, (V_ROWS - HEAD_DIM, ncp), 0) == 0, 1.0, 0.0).astype(BF16)
    for g in range(NSA_KV_GROUPS):
        vct_ref[g, 0:HEAD_DIM, :] = vct[g * HEAD_DIM:(g + 1) * HEAD_DIM, :]
        vct_ref[g, HEAD_DIM:V_ROWS, :] = ones_row


def _compress(kx, vx, pek, pev, kw1, vw1, kw2, vw2t):
    b, seq, _ = kx.shape
    ncp = seq // CMP_STRIDE
    g = NSA_KV_GROUPS
    tok = lambda bi: (bi, 0, 0)
    const3 = lambda bi: (0, 0, 0)
    const2 = lambda bi: (0, 0)
    return pl.pallas_call(
        _compress_body,
        grid=(b,),
        in_specs=[pl.BlockSpec((None, seq, LANES), tok),
                  pl.BlockSpec((None, seq, LANES), tok),
                  pl.BlockSpec((CMP_BLOCK, SUBLANES, LANES), const3),
                  pl.BlockSpec((CMP_BLOCK, SUBLANES, LANES), const3),
                  pl.BlockSpec((CMP_BLOCK, LANES, LANES), const3),
                  pl.BlockSpec((CMP_BLOCK, LANES, LANES), const3),
                  pl.BlockSpec((LANES, LANES), const2),
                  pl.BlockSpec((LANES, LANES), const2)],
        out_specs=[pl.BlockSpec((None, ncp, LANES), tok),
                   pl.BlockSpec((None, g, V_ROWS, ncp), lambda bi: (bi, 0, 0, 0))],
        out_shape=[jax.ShapeDtypeStruct((b, ncp, LANES), BF16),
                   jax.ShapeDtypeStruct((b, g, V_ROWS, ncp), BF16)],
        compiler_params=pltpu.CompilerParams(dimension_semantics=("parallel",)),
        name="compress",
    )(kx, vx, pek, pev, kw1, vw1, kw2, vw2t)


def _rank_rows(score, rank, k0, k1):
    rows = score.shape[0]
    j = lax.broadcasted_iota(jnp.int32, (SUBLANES, 1), 0)
    blocks = [score[b:b + SUBLANES, :] for b in range(0, rows, SUBLANES)]
    for k in range(k0, k1):
        rk = score[k:k + 1, :]
        before = lambda sb: jnp.where(rk > sb, 1.0, 0.0)
        unless = lambda sb: jnp.where(sb > rk, 0.0, 1.0)
        counts = []
        for b, sb in zip(range(0, rows, SUBLANES), blocks):
            if b + SUBLANES <= k:
                counts.append(before(sb))
            elif b > k:
                counts.append(unless(sb))
            else:
                counts.append(jnp.where(j + b > k, unless(sb), before(sb)))
        rank = rank + jnp.concatenate(counts, axis=0)
    return rank


def _topk_mask(score, n_rows, k_top):
    return _rank_rows(score, jnp.zeros(score.shape, F32), 0, n_rows) < k_top


def _finish(acc):
    return acc[0:HEAD_DIM, :] / acc[HEAD_DIM:HEAD_DIM + 1, :]


def _softmax_reset(m_ref, acc_ref):
    m_ref[...] = jnp.full(m_ref.shape, NEG, F32)
    acc_ref[...] = jnp.zeros(acc_ref.shape, F32)


def _park(parked, slot, sb):
    s_buf, bm_ref = parked
    s_buf[slot, 0:sb.shape[0], :] = sb
    bm_ref[slot] = jnp.max(sb, axis=0, keepdims=True)


def _softmax_update(m_ref, acc_ref, parked, slot, values):
    s_buf, bm_ref = parked
    m_old = m_ref[...]
    m_new = jnp.maximum(m_old, bm_ref[slot])
    pb = jnp.exp2(s_buf[slot] - m_new).astype(BF16)
    acc_ref[...] = jnp.exp2(m_old - m_new) * acc_ref[...] + values(pb)
    m_ref[...] = m_new


def _online_blocks(i, parked, m_ref, acc_ref, scores, values, causal):
    block = lambda kb: jnp.maximum(kb, 0)

    def park(slot, kb, live, mask=None):
        sb = scores(block(kb), live)
        _park(parked, slot, sb if mask is None else jnp.where(mask, sb, NEG))

    def update(slot, kb):
        _softmax_update(m_ref, acc_ref, parked, slot, lambda pb: values(block(kb), pb))

    _softmax_reset(m_ref, acc_ref)
    first = -((i + 1) & 1)
    n_pairs = (i + 2) // 2
    park(0, first, first >= 0)

    def trip(p):
        ka = first + 2 * p
        park(1, ka + 1, True)
        update(0, ka)
        park(0, ka + 2, True)
        update(1, ka + 1)

    def finish():
        park(1, i, True, causal)
        update(0, i - 1)
        update(1, i)
        return acc_ref[...]

    return trip, finish


def _online_blocks_diag_first(i, parked, m_ref, acc_ref, gate, diag_scores, select, values, causal):
    past = lambda x: jnp.clip(x - 1, 0, jnp.maximum(i - 1, 0))

    def park(slot, x):
        _park(parked, slot, scores(past(x), x - 1 < i))

    def update(slot, kb):
        _softmax_update(m_ref, acc_ref, parked, slot, lambda pb: values(kb, pb))

    _softmax_reset(m_ref, acc_ref)
    gates = gate()
    _park(parked, 0, jnp.where(causal, diag_scores(), NEG))
    scores = select(gates)
    park(1, 1)
    update(0, i)
    park(0, 2)
    update(1, past(1))

    def trip(p):
        park(1, 2 * p + 1)
        update(0, past(2 * p))
        park(0, 2 * p + 2)
        update(1, past(2 * p + 1))

    return trip, lambda: acc_ref[...]


def _nsa_body(q_ref, qr_ref, gn_ref, kc_ref, vct_ref, ke_ref, vts_ref, kw_ref, vtw_ref, ovt_ref,
              y_ref, s_buf, bm_ref, m_ref, acc_ref, rank_ref, *, n_sel_blocks):
    tq = q_ref.shape[1]
    ng, rg = NSA_KV_GROUPS, NSA_GROUP_SIZE
    nh = ng * rg
    i = pl.program_id(1)
    t0 = i * tq
    head = lambda a, h: a[h * HEAD_DIM:(h + 1) * HEAD_DIM, :]
    cols = lambda a, h, n=1: a[:, h * tq:(h + n) * tq]
    q = q_ref[...]
    qr = qr_ref[...]
    tpos = t0 + lax.broadcasted_iota(jnp.int32, (1, tq), 1)
    tpos_n = jnp.concatenate([tpos] * nh, axis=1)

    zero = jnp.zeros((HEAD_DIM, tq), BF16)
    place = lambda h, a: jnp.concatenate([a, zero] if h < rg else [zero, a], axis=0)
    qc = jnp.concatenate([place(h, head(q, h)) for h in range(nh)], axis=1)
    qw = jnp.concatenate([place(h, head(qr, h)) for h in range(nh)], axis=1)
    per_group = lambda v_ref, kb, pb: jnp.concatenate(
        [_dot(v_ref[g, kb], cols(pb, g * rg, rg)) for g in range(ng)], axis=1)

    causal = (lax.broadcasted_iota(jnp.int32, (KEY_BLOCK, 1), 0)
              <= (lax.broadcasted_iota(jnp.int32, (1, nh * tq), 1) & (tq - 1)))

    ncp = kc_ref.shape[0]
    i1 = jnp.maximum(i - 1, 0)
    i2 = jnp.maximum(i - 2, 0)
    n = lax.broadcasted_iota(jnp.int32, (ncp, 1), 0)
    parked = (s_buf, bm_ref)
    _park(parked, 0, jnp.where((n * CMP_STRIDE + (CMP_BLOCK - 1)) <= tpos_n, _dot(kc_ref[...], qc), NEG))
    _park(parked, 1, jnp.where(causal, _dot(kw_ref[i], qw), NEG))

    p = jnp.exp2(s_buf[0, 0:ncp, :] - bm_ref[0])
    pb16 = p.astype(BF16)
    acc = jnp.concatenate([_dot(vct_ref[g], cols(pb16, g * rg, rg)) for g in range(ng)], axis=1)
    inv = jnp.where(tpos_n >= CMP_BLOCK - 1, 1.0 / jnp.maximum(acc[HEAD_DIM:HEAD_DIM + 1, :], 1e-30), 0.0)
    o_cmp = acc[0:HEAD_DIM, :] * inv

    pn = p * inv
    ps = jnp.concatenate([sum(cols(pn, g * rg + r) for r in range(1, rg)) + cols(pn, g * rg)
                          for g in range(ng)], axis=1)
    ps_hi = ps.astype(BF16)
    ps_lo = (ps - ps_hi.astype(F32)).astype(BF16)
    imp = _dot(ovt_ref[...], ps_hi) + _dot(ovt_ref[...], ps_lo)
    jb = lax.broadcasted_iota(jnp.int32, (imp.shape[0], 1), 0)
    cur = jnp.concatenate([tpos >> (SEL_BLOCK.bit_length() - 1)] * ng, axis=1)
    forced = (jb == 0) | (jb == cur) | (jb == cur - 1)
    imp = jnp.where(forced, jnp.inf, imp)
    imp = jnp.where(jb <= cur, imp, -jnp.inf)

    _park(parked, 0, jnp.where(i >= 1, _dot(kw_ref[i1], qw), NEG))
    _softmax_reset(m_ref, acc_ref)
    _softmax_update(m_ref, acc_ref, parked, 1, lambda pb: per_group(vtw_ref, i, pb))
    _park(parked, 1, jnp.where(jnp.logical_or(causal, i < 2), NEG, _dot(kw_ref[i2], qw)))
    _softmax_update(m_ref, acc_ref, parked, 0, lambda pb: per_group(vtw_ref, i1, pb))
    _softmax_update(m_ref, acc_ref, parked, 1, lambda pb: per_group(vtw_ref, i2, pb))
    o_win = _finish(acc_ref[...])

    quarter = n_sel_blocks // 4
    last_blk = (tq // SEL_BLOCK) * (i + 1) - 1
    rank_ref[...] = _rank_rows(imp, jnp.zeros(imp.shape, F32), 0, quarter)
    for c in range(1, 4):
        @pl.when(last_blk >= c * quarter)
        def _():
            rank_ref[...] = _rank_rows(imp, rank_ref[...], c * quarter, (c + 1) * quarter)
    rank = rank_ref[...]
    bias = jnp.where(rank < min(SEL_TOPN, n_sel_blocks), 0.0, NEG).astype(BF16)

    off = jnp.full((bias.shape[0], tq), NEG, BF16)
    aug = lambda b_rows: jnp.concatenate(
        [jnp.concatenate([cols(qw, h), b_rows(h // rg), zero], axis=0) for h in range(nh)], axis=1)
    qa = aug(lambda g: cols(bias, g))
    qa_off = aug(lambda g: off)
    trip, last_blocks = _online_blocks(
        i, parked, m_ref, acc_ref,
        lambda kb, live: _dot(ke_ref[kb], qa if live is True else jnp.where(live, qa, qa_off)),
        lambda kb, pb: per_group(vts_ref, kb, pb),
        causal)

    def finish():
        o_sel = _finish(last_blocks())
        for h in range(nh):
            gates = gn_ref[h // rg]
            r = h % rg
            y_ref[h * HEAD_DIM:(h + 1) * HEAD_DIM, :] = (
                gates[3 * r:3 * r + 1, :] * cols(o_cmp, h)
                + gates[3 * r + 1:3 * r + 2, :] * cols(o_sel, h)
                + gates[3 * r + 2:3 * r + 3, :] * cols(o_win, h)).astype(y_ref.dtype)

    return trip, finish


def _moba_body(q_ref, km_ref, ke_ref, vt_ref, y_ref, s_buf, bm_ref, m_ref, acc_ref, *, n_blocks):
    tq = q_ref.shape[1]
    nh = km_ref.shape[0]
    i = pl.program_id(1)
    q = q_ref[...] * SCALE
    qb = q.astype(BF16)
    head = lambda a, h: a[h * HEAD_DIM:(h + 1) * HEAD_DIM, :]
    zero = jnp.zeros((HEAD_DIM, tq), BF16)
    n_bias = km_ref.shape[1]
    pad = jnp.zeros((K_AUG - LANES - n_bias, tq), BF16)

    def aug(h, b_rows):
        pair = [head(qb, h), zero] if h % 2 == 0 else [zero, head(qb, h)]
        return jnp.concatenate(pair + [b_rows, pad], axis=0)

    def keys(kb, h):
        return ke_ref[kb, :, (h // 2) * K_AUG:(h // 2 + 1) * K_AUG]

    def diag_scores():
        open_rows = jnp.zeros((n_bias, tq), BF16)
        return jnp.concatenate([_dot(keys(i, h), aug(h, open_rows)) for h in range(nh)], axis=1)

    def gate():
        q_lo = (q - qb.astype(F32)).astype(BF16)
        gates = []
        for h in range(nh):
            km = km_ref[h]
            km_hi = km.astype(BF16)
            km_lo = (km - km_hi.astype(F32)).astype(BF16)
            gates.append(_dot(km_hi, head(qb, h)) + (_dot(km_hi, head(q_lo, h)) + _dot(km_lo, head(qb, h))))
        return jnp.concatenate(gates, axis=1)

    def select(gate):
        jb = lax.broadcasted_iota(jnp.int32, (gate.shape[0], 1), 0)
        gate = jnp.where(jb < i, gate, -jnp.inf)
        sel = _topk_mask(gate, n_blocks, min(MOBA_TOPK, n_blocks)) & (gate > -jnp.inf)
        bias = jnp.where(sel, 0.0, NEG).astype(BF16)
        off = jnp.full((n_bias, tq), NEG, BF16)
        qa = [aug(h, bias[:, h * tq:(h + 1) * tq]) for h in range(nh)]
        qa_off = [aug(h, off) for h in range(nh)]
        return lambda kb, live: jnp.concatenate(
            [_dot(keys(kb, h), jnp.where(live, qa[h], qa_off[h])) for h in range(nh)], axis=1)

    def values(kb, pb):
        return jnp.concatenate([_dot(vt_ref[h, kb], pb[:, h * tq:(h + 1) * tq]) for h in range(nh)], axis=1)

    causal = (lax.broadcasted_iota(jnp.int32, (KEY_BLOCK, 1), 0)
              <= (lax.broadcasted_iota(jnp.int32, (1, nh * tq), 1) & (tq - 1)))
    trip, last_blocks = _online_blocks_diag_first(i, (s_buf, bm_ref), m_ref, acc_ref,
                                                  gate, diag_scores, select, values, causal)

    def finish():
        o = _finish(last_blocks())
        for h in range(nh):
            y_ref[h * HEAD_DIM:(h + 1) * HEAD_DIM, :] = o[:, h * tq:(h + 1) * tq].astype(y_ref.dtype)

    return trip, finish


N_NSA_IN = 10
N_MOBA_IN = 4


def _mixers_body(*refs, n_sel_blocks, n_blocks):
    nsa_in = refs[:N_NSA_IN]
    moba_in = refs[N_NSA_IN:N_NSA_IN + N_MOBA_IN]
    y_nsa, y_moba = refs[N_NSA_IN + N_MOBA_IN:N_NSA_IN + N_MOBA_IN + 2]
    scratch = refs[N_NSA_IN + N_MOBA_IN + 2:]
    nsa_trip, nsa_finish = _nsa_body(*nsa_in, y_nsa, *scratch[:5], n_sel_blocks=n_sel_blocks)
    moba_trip, moba_finish = _moba_body(*moba_in, y_moba, *scratch[5:], n_blocks=n_blocks)
    i = pl.program_id(1)

    @pl.loop(0, (i + 2) // 2 - 1)
    def _(p):
        nsa_trip(p)
        moba_trip(p + 1)

    nsa_finish()
    moba_finish()


def _mixers_call(qt, gnt, kc, vct, ke, vts, kw, vtw, ovt, qmt, kmean, kem, vtm):
    b, _, seq = qt.shape
    g = NSA_KV_GROUPS
    nkb = seq // KEY_BLOCK
    ncp = kc.shape[1]
    tq = Q_TILE
    n_cols = NSA_HEADS * tq
    tile = lambda bi, i: (bi, 0, i)
    per_batch = lambda shape: pl.BlockSpec((None,) + shape, lambda bi, i: (bi,) + (0,) * len(shape))
    return pl.pallas_call(
        functools.partial(_mixers_body, n_sel_blocks=seq // SEL_BLOCK, n_blocks=nkb),
        grid=(b, seq // tq),
        in_specs=[pl.BlockSpec((None, NSA_WIDTH, tq), tile),
                  pl.BlockSpec((None, NSA_WIDTH, tq), lambda bi, i: (bi, 1, i)),
                  pl.BlockSpec((None, g, GATE_ROWS, tq), lambda bi, i: (bi, 0, 0, i)),
                  per_batch((ncp, LANES)),
                  per_batch((g, V_ROWS, ncp)),
                  per_batch((nkb, KEY_BLOCK, K_AUG)),
                  per_batch((g, nkb, V_ROWS, KEY_BLOCK)),
                  per_batch((nkb, KEY_BLOCK, LANES)),
                  per_batch((g, nkb, V_ROWS, KEY_BLOCK)),
                  pl.BlockSpec((HEAD_DIM, ncp), lambda bi, i: (0, 0)),
                  pl.BlockSpec((None, MOBA_WIDTH, tq), tile),
                  per_batch((MOBA_HEADS, 16, HEAD_DIM)),
                  per_batch((nkb, KEY_BLOCK, (MOBA_HEADS // 2) * K_AUG)),
                  per_batch((MOBA_HEADS, nkb, V_ROWS, KEY_BLOCK))],
        out_specs=[pl.BlockSpec((None, NSA_WIDTH, tq), tile), pl.BlockSpec((None, MOBA_WIDTH, tq), tile)],
        out_shape=[jax.ShapeDtypeStruct((b, NSA_WIDTH, seq), BF16),
                   jax.ShapeDtypeStruct((b, MOBA_WIDTH, seq), BF16)],
        scratch_shapes=[pltpu.VMEM((2, KEY_BLOCK, n_cols), F32),
                        pltpu.VMEM((2, 1, n_cols), F32),
                        pltpu.VMEM((1, n_cols), F32),
                        pltpu.VMEM((V_ROWS, n_cols), F32),
                        pltpu.VMEM((HEAD_DIM, g * tq), F32),
                        pltpu.VMEM((2, KEY_BLOCK, n_cols), F32),
                        pltpu.VMEM((2, 1, n_cols), F32),
                        pltpu.VMEM((1, n_cols), F32),
                        pltpu.VMEM((V_ROWS, n_cols), F32)],
        compiler_params=pltpu.CompilerParams(
            dimension_semantics=("parallel", "arbitrary"), vmem_limit_bytes=VMEM_LIMIT),
        name="mixers",
    )(qt, qt, gnt, kc, vct, ke, vts, kw, vtw, ovt, qmt, kmean, kem, vtm)


def _merge_ffn_body(x_ref, yn_ref, ym_ref, gm_ref, wun_ref, wum_ref, wo_ref, g2_ref,
                    w1_ref, w2_ref, gf_ref, o_ref):
    tn = (((0,), (0,)), ((), ()))
    a = lax.dot_general(yn_ref[...], wun_ref[...], tn, preferred_element_type=F32)
    c = lax.dot_general(ym_ref[...], wum_ref[...], tn, preferred_element_type=F32)
    mixed = gm_ref[:, 0:D_MODEL] * a + gm_ref[:, D_MODEL:2 * D_MODEL] * c
    x1 = x_ref[...] + _dot(mixed.astype(BF16), wo_ref[...])
    ms = jnp.mean(x1 * x1, axis=-1, keepdims=True)
    h2 = (x1 * lax.rsqrt(ms + NORM_EPS) * g2_ref[...]).astype(BF16)
    x2 = x1
    for k in range(D_FF // FF_CHUNK):
        u = jnp.maximum(_dot(h2, w1_ref[:, k * FF_CHUNK:(k + 1) * FF_CHUNK]), 0.0)
        x2 = x2 + _dot((u * u).astype(BF16), w2_ref[k * FF_CHUNK:(k + 1) * FF_CHUNK, :])
    ms = jnp.mean(x2 * x2, axis=-1, keepdims=True)
    o_ref[...] = x2 * lax.rsqrt(ms + NORM_EPS) * gf_ref[...]


def _merge_ffn(x2, ynt, ymt, gm, wun, wum, wo, g2, w1, w2, gf):
    m = x2.shape[0]
    seq = ynt.shape[2]
    tm = 512
    nt = seq // tm
    row = lambda i: (i, 0)
    const = lambda i: (0, 0)
    feat = lambda i: (i // nt, 0, i % nt)
    resident = lambda shape: pl.BlockSpec(shape, const, pipeline_mode=pl.Buffered(1))
    return pl.pallas_call(
        _merge_ffn_body,
        grid=(m // tm,),
        in_specs=[pl.BlockSpec((tm, D_MODEL), row),
                  pl.BlockSpec((None, NSA_WIDTH, tm), feat),
                  pl.BlockSpec((None, MOBA_WIDTH, tm), feat),
                  pl.BlockSpec((tm, 2 * D_MODEL), row),
                  resident((NSA_WIDTH, D_MODEL)),
                  resident((MOBA_WIDTH, D_MODEL)),
                  resident((D_MODEL, D_MODEL)),
                  pl.BlockSpec((1, D_MODEL), const),
                  resident((D_MODEL, D_FF)),
                  resident((D_FF, D_MODEL)),
                  pl.BlockSpec((1, D_MODEL), const)],
        out_specs=pl.BlockSpec((tm, D_MODEL), row),
        out_shape=jax.ShapeDtypeStruct((m, D_MODEL), F32),
        compiler_params=pltpu.CompilerParams(dimension_semantics=("parallel",),
                                             vmem_limit_bytes=VMEM_LIMIT),
        name="merge_ffn",
    )(x2, ynt, ymt, gm, wun, wum, wo, g2, w1, w2, gf)


def _rope_tables(seq):
    half = ROT_DIM // 2
    inv = ROPE_THETA ** (-jnp.arange(0, ROT_DIM, 2, dtype=F32) / ROT_DIM)
    ang = jnp.arange(seq, dtype=F32)[:, None] * inv[None, :]
    cos, sin = jnp.cos(ang), jnp.sin(ang)
    pad = HEAD_DIM - ROT_DIM
    one_head = lambda a, b, fill: jnp.concatenate([a, b, jnp.full((seq, pad), fill, F32)], axis=1)
    zeros = jnp.zeros((seq, half), F32)
    two = lambda a: jnp.concatenate([a, a], axis=1)
    return (two(one_head(cos, cos, 1.0)),
            two(one_head(-sin, zeros, 0.0)),
            two(one_head(zeros, sin, 0.0)),
            cos.T, sin.T)


def _overlap_t(ncp, nsb):
    i = np.arange(ncp)[None, :]
    j = np.arange(HEAD_DIM)[:, None]
    start = i * CMP_STRIDE
    end = start + CMP_BLOCK - 1
    ov = (end >= j * SEL_BLOCK) & (start <= j * SEL_BLOCK + SEL_BLOCK - 1) & (i < ncp - 1) & (j < nsb)
    return jnp.asarray(ov.astype(np.float32), dtype=BF16)


def _block_diag(w):
    z = jnp.zeros_like(w)
    return jnp.concatenate([jnp.concatenate([w, z], axis=-1), jnp.concatenate([z, w], axis=-1)], axis=-2)


def _projection_weights(w):
    kv = lambda j: w[:, OFF_KV + j * KV_WIDTH:OFF_KV + (j + 1) * KV_WIDTH]
    mo = lambda j: w[:, OFF_M + j * MOBA_WIDTH:OFF_M + (j + 1) * MOBA_WIDTH]
    wa = jnp.concatenate([kv(0), kv(1), kv(2), kv(4), mo(1), w[:, OFF_GM:]], axis=1)
    gn = w[:, OFF_GN:OFF_M].T.reshape(NSA_KV_GROUPS, 3 * NSA_GROUP_SIZE, D_MODEL)
    gn = jnp.concatenate([gn, jnp.zeros((NSA_KV_GROUPS, GATE_ROWS - 3 * NSA_GROUP_SIZE, D_MODEL), w.dtype)], axis=1)
    wb = jnp.concatenate([w[:, :OFF_KV].T, mo(0).T, kv(3).T, kv(5).T, mo(2).T,
                          gn.reshape(NSA_KV_GROUPS * GATE_ROWS, D_MODEL)], axis=0)
    return wa.astype(BF16), wb.astype(BF16)


def _mixers(x, norm1_g, w_in, cmp_pe_k, cmp_pe_v, cmp_k_w1, cmp_k_w2, cmp_v_w1, cmp_v_w2):
    b, seq, _ = x.shape
    g, d = NSA_KV_GROUPS, HEAD_DIM
    nkb = seq // KEY_BLOCK
    ncp = seq // CMP_STRIDE
    nsb = seq // SEL_BLOCK
    layer = 0

    wa, wb = _projection_weights(w_in[layer])
    x2 = x.reshape(b * seq, D_MODEL)
    (qt, qmt, gnt, kcx, vcx, ke, kw, vts, vtw, kem, vtm, kmean8, ogm) = _in_proj(
        x2, norm1_g[layer][None, :], wa, wb, _rope_tables(seq), b, seq)

    per_group = lambda a: jnp.concatenate([a] * g, axis=-1)
    pe_rows = lambda pe: jnp.broadcast_to(per_group(pe)[:, None, :], (CMP_BLOCK, SUBLANES, LANES)).astype(BF16)
    w1_blocks = lambda w1: _block_diag(w1.reshape(CMP_BLOCK, d, d)).astype(BF16)
    kc, vct = _compress(
        kcx.reshape(b, seq, KV_WIDTH), vcx.reshape(b, seq, KV_WIDTH),
        pe_rows(cmp_pe_k[layer]), pe_rows(cmp_pe_v[layer]),
        w1_blocks(cmp_k_w1[layer]), w1_blocks(cmp_v_w1[layer]),
        _block_diag(cmp_k_w2[layer]).astype(BF16), _block_diag(cmp_v_w2[layer].T).astype(BF16))
    kmean = kmean8[:, ::SUBLANES, :].reshape(b, nkb, MOBA_HEADS, d)
    kmean = jnp.swapaxes(kmean, 1, 2)
    kmean = jnp.concatenate([kmean, jnp.zeros((b, MOBA_HEADS, 16 - nkb, d), F32)], axis=2)
    ynt, ymt = _mixers_call(qt, gnt, kc, vct, ke, vts, kw, vtw, _overlap_t(ncp, nsb),
                            qmt, kmean, kem, vtm)
    return x2, ynt, ymt, ogm


def kernel(x, norm1_g, w_in, cmp_pe_k, cmp_pe_v, cmp_k_w1, cmp_k_w2, cmp_v_w1, cmp_v_w2,
           w_up_nsa, w_up_moba, w_out, norm2_g, w_ff1, w_ff2, norm_f_g):
    b, seq, _ = x.shape
    layer = 0
    x2, ynt, ymt, ogm = _mixers(x, norm1_g, w_in, cmp_pe_k, cmp_pe_v, cmp_k_w1, cmp_k_w2, cmp_v_w1, cmp_v_w2)
    out = _merge_ffn(x2, ynt, ymt, ogm, w_up_nsa[layer].astype(BF16), w_up_moba[layer].astype(BF16),
                     w_out[layer].astype(BF16), norm2_g[layer][None, :],
                     w_ff1[layer].astype(BF16), w_ff2[layer].astype(BF16), norm_f_g[None, :])
    return out.reshape(b, seq, D_MODEL)
```

```python
import functools

import numpy as np
import jax
import jax.numpy as jnp
from jax import lax
from jax.experimental import pallas as pl
from jax.experimental.pallas import tpu as pltpu

F32 = jnp.float32
BF16 = jnp.bfloat16

D_MODEL = 1024
HEAD_DIM = 64
ROT_DIM = HEAD_DIM // 4
ROPE_THETA = 500000.0
NORM_EPS = 1e-6

NSA_HEADS = 8
NSA_KV_GROUPS = 2
NSA_GROUP_SIZE = NSA_HEADS // NSA_KV_GROUPS
CMP_BLOCK = 32
CMP_STRIDE = 16
SEL_BLOCK = 64
SEL_TOPN = 16
WINDOW = 512

MOBA_HEADS = 8
MOBA_BLOCK = 256
MOBA_TOPK = 3

D_FF = 4 * D_MODEL
NSA_WIDTH = NSA_HEADS * HEAD_DIM
KV_WIDTH = NSA_KV_GROUPS * HEAD_DIM
MOBA_WIDTH = MOBA_HEADS * HEAD_DIM
OFF_KV = NSA_WIDTH
OFF_GN = OFF_KV + 6 * KV_WIDTH
OFF_M = OFF_GN + 3 * NSA_HEADS
OFF_GM = OFF_M + 3 * MOBA_WIDTH
IN_WIDTH = OFF_GM + 2 * D_MODEL

LANES = 128
SUBLANES = 8
KEY_BLOCK = 256
Q_TILE = 256
V_ROWS = 80
K_AUG = 2 * LANES
NEG = -1e30
SCALE = float(HEAD_DIM ** -0.5 * np.log2(np.e))
GATE_ROWS = 16
MOBA_MEAN_ROWS = 16
FF_CHUNK = 1024
VMEM_LIMIT = 56 * 1024 * 1024

A_KC, A_KSW, A_KM, A_GM = 0, 256, 512, 1024
A_WIDTH = A_GM + 2 * D_MODEL
B_QN, B_QM, B_V, B_VM, B_GN = 0, 512, 1024, 1280, 1792
B_ROWS = B_GN + NSA_KV_GROUPS * GATE_ROWS


def _dot(a, b):
    return jnp.dot(a, b, preferred_element_type=F32)


def _dot_nt(a, b):
    return lax.dot_general(a, b, (((1,), (1,)), ((), ())), preferred_element_type=F32)


def _sigmoid(y):
    return 1.0 / (1.0 + jnp.exp(-y))


def _in_proj_body(x_ref, g_ref, wa_ref, wb_ref, c_ref, s1_ref, s2_ref, ct_ref, st_ref,
                  qt_ref, qmt_ref, gnt_ref, kcx_ref, vcx_ref, ke_ref, kw_ref, vts_ref, vtw_ref,
                  kem_ref, vtm_ref, kmean_ref, gm_ref, h_ref, *, tiles_per_seq):
    tm = x_ref.shape[0]
    it = pl.program_id(0) % tiles_per_seq
    x = x_ref[...]
    ms = jnp.mean(x * x, axis=-1, keepdims=True)
    h_ref[...] = (x * lax.rsqrt(ms + NORM_EPS) * g_ref[...]).astype(BF16)

    cos = c_ref[...]
    sin_hi = s1_ref[...]
    sin_lo = s2_ref[...]

    y_all = _dot(h_ref[...], wa_ref[...])

    def proj(c0, width):
        return y_all[:, c0:c0 + width]

    def rope(yc):
        return (yc * cos + pltpu.roll(yc, LANES - ROT_DIM // 2, 1) * sin_hi
                + pltpu.roll(yc, ROT_DIM // 2, 1) * sin_lo)

    y = proj(A_KC, 2 * KV_WIDTH)
    kcx_ref[...] = y[:, 0:KV_WIDTH]
    vcx_ref[...] = y[:, KV_WIDTH:2 * KV_WIDTH]

    row = lax.broadcasted_iota(jnp.int32, (tm, 1), 0)
    lane = lax.broadcasted_iota(jnp.int32, (1, LANES), 1)
    y = proj(A_KSW, 2 * KV_WIDTH)
    ke_ref[:, 0:LANES] = rope(y[:, 0:LANES]).astype(BF16)
    sel_blk = (it * tm + row) >> (SEL_BLOCK.bit_length() - 1)
    ke_ref[:, LANES:K_AUG] = jnp.where(sel_blk == lane, 1.0, 0.0).astype(BF16)
    kw_ref[...] = rope(y[:, LANES:2 * LANES]).astype(BF16)

    y = proj(A_KM, MOBA_WIDTH)
    own_blk = jnp.where(lane == it, 1.0, 0.0).astype(BF16)
    sums = []
    for j in range(MOBA_WIDTH // LANES):
        kr = rope(y[:, j * LANES:(j + 1) * LANES])
        sums.append(jnp.mean(kr, axis=0, keepdims=True))
        kem_ref[:, j * K_AUG:j * K_AUG + LANES] = kr.astype(BF16)
        kem_ref[:, j * K_AUG + LANES:(j + 1) * K_AUG] = jnp.broadcast_to(own_blk, (tm, LANES))
    kmean_ref[...] = jnp.broadcast_to(jnp.concatenate(sums, axis=1), kmean_ref.shape)

    for part in range(4):
        w = 2 * D_MODEL // 4
        gm_ref[:, part * w:(part + 1) * w] = _sigmoid(proj(A_GM + part * w, w)).astype(BF16)

    cos_t = ct_ref[...]
    sin_t = st_ref[...]

    yt_all = _dot_nt(wb_ref[...], h_ref[...])

    def proj_t(r0, rows):
        return yt_all[r0:r0 + rows, :]

    def rope_t(yt):
        half = ROT_DIM // 2
        out = []
        for hd in range(yt.shape[0] // HEAD_DIM):
            a = yt[hd * HEAD_DIM:hd * HEAD_DIM + half, :]
            b = yt[hd * HEAD_DIM + half:hd * HEAD_DIM + ROT_DIM, :]
            out += [a * cos_t - b * sin_t, a * sin_t + b * cos_t,
                    yt[hd * HEAD_DIM + ROT_DIM:(hd + 1) * HEAD_DIM, :]]
        return jnp.concatenate(out, axis=0)

    y = proj_t(B_QN, NSA_WIDTH)
    qt_ref[0:NSA_WIDTH, :] = (y * SCALE).astype(BF16)
    qt_ref[NSA_WIDTH:2 * NSA_WIDTH, :] = (rope_t(y) * SCALE).astype(BF16)
    qmt_ref[...] = rope_t(proj_t(B_QM, MOBA_WIDTH))

    ones_row = jnp.where(lax.broadcasted_iota(jnp.int32, (V_ROWS - HEAD_DIM, tm), 0) == 0, 1.0, 0.0).astype(BF16)
    y = proj_t(B_V, 4 * HEAD_DIM)
    for g in range(NSA_KV_GROUPS):
        vts_ref[g, 0:HEAD_DIM, :] = y[g * HEAD_DIM:(g + 1) * HEAD_DIM, :].astype(BF16)
        vts_ref[g, HEAD_DIM:V_ROWS, :] = ones_row
        vtw_ref[g, 0:HEAD_DIM, :] = y[(2 + g) * HEAD_DIM:(3 + g) * HEAD_DIM, :].astype(BF16)
        vtw_ref[g, HEAD_DIM:V_ROWS, :] = ones_row
    y = proj_t(B_VM, MOBA_WIDTH)
    for hd in range(MOBA_HEADS):
        vtm_ref[hd, 0:HEAD_DIM, :] = y[hd * HEAD_DIM:(hd + 1) * HEAD_DIM, :].astype(BF16)
        vtm_ref[hd, HEAD_DIM:V_ROWS, :] = ones_row
    y = _sigmoid(proj_t(B_GN, NSA_KV_GROUPS * GATE_ROWS))
    for g in range(NSA_KV_GROUPS):
        gnt_ref[g] = y[g * GATE_ROWS:(g + 1) * GATE_ROWS, :]


def _in_proj(x2, g1, wa, wb, tabs, batch, seq):
    m = x2.shape[0]
    tm = KEY_BLOCK
    nt = seq // tm
    row = lambda i: (i, 0)
    const = lambda i: (0, 0)
    tab = lambda i: (i % nt, 0)
    tab_t = lambda i: (0, i % nt)
    feat = lambda i: (i // nt, 0, i % nt)
    blk4 = lambda i: (i // nt, i % nt, 0, 0)
    blk5 = lambda i: (i // nt, 0, i % nt, 0, 0)
    g = NSA_KV_GROUPS
    outs = [
        (jax.ShapeDtypeStruct((batch, 2 * NSA_WIDTH, seq), BF16), pl.BlockSpec((None, 2 * NSA_WIDTH, tm), feat)),
        (jax.ShapeDtypeStruct((batch, MOBA_WIDTH, seq), F32), pl.BlockSpec((None, MOBA_WIDTH, tm), feat)),
        (jax.ShapeDtypeStruct((batch, g, GATE_ROWS, seq), F32),
         pl.BlockSpec((None, g, GATE_ROWS, tm), lambda i: (i // nt, 0, 0, i % nt))),
        (jax.ShapeDtypeStruct((m, KV_WIDTH), F32), pl.BlockSpec((tm, KV_WIDTH), row)),
        (jax.ShapeDtypeStruct((m, KV_WIDTH), F32), pl.BlockSpec((tm, KV_WIDTH), row)),
        (jax.ShapeDtypeStruct((batch, nt, tm, K_AUG), BF16), pl.BlockSpec((None, None, tm, K_AUG), blk4)),
        (jax.ShapeDtypeStruct((batch, nt, tm, LANES), BF16), pl.BlockSpec((None, None, tm, LANES), blk4)),
        (jax.ShapeDtypeStruct((batch, g, nt, V_ROWS, tm), BF16), pl.BlockSpec((None, g, None, V_ROWS, tm), blk5)),
        (jax.ShapeDtypeStruct((batch, g, nt, V_ROWS, tm), BF16), pl.BlockSpec((None, g, None, V_ROWS, tm), blk5)),
        (jax.ShapeDtypeStruct((batch, nt, tm, (MOBA_HEADS // 2) * K_AUG), BF16),
         pl.BlockSpec((None, None, tm, (MOBA_HEADS // 2) * K_AUG), blk4)),
        (jax.ShapeDtypeStruct((batch, MOBA_HEADS, nt, V_ROWS, tm), BF16),
         pl.BlockSpec((None, MOBA_HEADS, None, V_ROWS, tm), blk5)),
        (jax.ShapeDtypeStruct((batch, nt * SUBLANES, MOBA_WIDTH), F32),
         pl.BlockSpec((None, SUBLANES, MOBA_WIDTH), lambda i: (i // nt, i % nt, 0))),
        (jax.ShapeDtypeStruct((m, 2 * D_MODEL), BF16), pl.BlockSpec((tm, 2 * D_MODEL), row)),
    ]
    return pl.pallas_call(
        functools.partial(_in_proj_body, tiles_per_seq=nt),
        grid=(m // tm,),
        in_specs=[pl.BlockSpec((tm, D_MODEL), row),
                  pl.BlockSpec((1, D_MODEL), const),
                  pl.BlockSpec((D_MODEL, A_WIDTH), const),
                  pl.BlockSpec((B_ROWS, D_MODEL), const),
                  pl.BlockSpec((tm, LANES), tab),
                  pl.BlockSpec((tm, LANES), tab),
                  pl.BlockSpec((tm, LANES), tab),
                  pl.BlockSpec((SUBLANES, tm), tab_t),
                  pl.BlockSpec((SUBLANES, tm), tab_t)],
        out_specs=[o[1] for o in outs],
        out_shape=[o[0] for o in outs],
        scratch_shapes=[pltpu.VMEM((tm, D_MODEL), BF16)],
        compiler_params=pltpu.CompilerParams(dimension_semantics=("parallel",),
                                             vmem_limit_bytes=VMEM_LIMIT),
        name="in_proj",
    )(x2, g1, wa, wb, *tabs)


def _gelu_tanh(x):
    return 0.5 * x * (1.0 + jnp.tanh(np.sqrt(2.0 / np.pi) * (x + 0.044715 * (x * x * x))))


def _compress_body(kx_ref, vx_ref, pek_ref, pev_ref, kw1_ref, vw1_ref, kw2_ref, vw2t_ref,
                   kc_ref, vct_ref):
    ncp = kc_ref.shape[0]

    def hidden(x_ref, pe_ref, w1_ref):
        ya = jnp.zeros((ncp, LANES), F32)
        yb = jnp.zeros((ncp, LANES), F32)
        pe = jnp.zeros((SUBLANES, LANES), F32)
        for r in range(CMP_STRIDE):
            xr = x_ref[pl.ds(r, ncp, stride=CMP_STRIDE), :].astype(BF16)
            ya = ya + _dot(xr, w1_ref[r])
            yb = yb + _dot(xr, w1_ref[CMP_STRIDE + r])
        for l in range(CMP_BLOCK):
            pe = pe + _dot(pe_ref[l], w1_ref[l])
        return _gelu_tanh(ya + pltpu.roll(yb, ncp - 1, 0) + pe[0:1, :])

    hk = hidden(kx_ref, pek_ref, kw1_ref)
    kc_ref[...] = _dot(hk.astype(BF16), kw2_ref[...]).astype(BF16)
    hv = hidden(vx_ref, pev_ref, vw1_ref)
    vct = _dot_nt(vw2t_ref[...], hv.astype(BF16)).astype(BF16)
    ones_row = jnp.where(lax.broadcasted_iota(jnp.int32, (V_ROWS - HEAD_DIM, ncp), 0) == 0, 1.0, 0.0).astype(BF16)
    for g in range(NSA_KV_GROUPS):
        vct_ref[g, 0:HEAD_DIM, :] = vct[g * HEAD_DIM:(g + 1) * HEAD_DIM, :]
        vct_ref[g, HEAD_DIM:V_ROWS, :] = ones_row


def _compress(kx, vx, pek, pev, kw1, vw1, kw2, vw2t):
    b, seq, _ = kx.shape
    ncp = seq // CMP_STRIDE
    g = NSA_KV_GROUPS
    tok = lambda bi: (bi, 0, 0)
    const3 = lambda bi: (0, 0, 0)
    const2 = lambda bi: (0, 0)
    return pl.pallas_call(
        _compress_body,
        grid=(b,),
        in_specs=[pl.BlockSpec((None, seq, LANES), tok),
                  pl.BlockSpec((None, seq, LANES), tok),
                  pl.BlockSpec((CMP_BLOCK, SUBLANES, LANES), const3),
                  pl.BlockSpec((CMP_BLOCK, SUBLANES, LANES), const3),
                  pl.BlockSpec((CMP_BLOCK, LANES, LANES), const3),
                  pl.BlockSpec((CMP_BLOCK, LANES, LANES), const3),
                  pl.BlockSpec((LANES, LANES), const2),
                  pl.BlockSpec((LANES, LANES), const2)],
        out_specs=[pl.BlockSpec((None, ncp, LANES), tok),
                   pl.BlockSpec((None, g, V_ROWS, ncp), lambda bi: (bi, 0, 0, 0))],
        out_shape=[jax.ShapeDtypeStruct((b, ncp, LANES), BF16),
                   jax.ShapeDtypeStruct((b, g, V_ROWS, ncp), BF16)],
        compiler_params=pltpu.CompilerParams(dimension_semantics=("parallel",)),
        name="compress",
    )(kx, vx, pek, pev, kw1, vw1, kw2, vw2t)


def _rank_rows(score, rank, k0, k1):
    rows = score.shape[0]
    j = lax.broadcasted_iota(jnp.int32, (SUBLANES, 1), 0)
    blocks = [score[b:b + SUBLANES, :] for b in range(0, rows, SUBLANES)]
    for k in range(k0, k1):
        rk = score[k:k + 1, :]
        before = lambda sb: jnp.where(rk > sb, 1.0, 0.0)
        unless = lambda sb: jnp.where(sb > rk, 0.0, 1.0)
        counts = []
        for b, sb in zip(range(0, rows, SUBLANES), blocks):
            if b + SUBLANES <= k:
                counts.append(before(sb))
            elif b > k:
                counts.append(unless(sb))
            else:
                counts.append(jnp.where(j + b > k, unless(sb), before(sb)))
        rank = rank + jnp.concatenate(counts, axis=0)
    return rank


def _topk_mask(score, n_rows, k_top):
    return _rank_rows(score, jnp.zeros(score.shape, F32), 0, n_rows) < k_top


def _finish(acc):
    return acc[0:HEAD_DIM, :] / acc[HEAD_DIM:HEAD_DIM + 1, :]


def _softmax_reset(m_ref, acc_ref):
    m_ref[...] = jnp.full(m_ref.shape, NEG, F32)
    acc_ref[...] = jnp.zeros(acc_ref.shape, F32)


def _park(parked, slot, sb):
    s_buf, bm_ref = parked
    s_buf[slot, 0:sb.shape[0], :] = sb
    bm_ref[slot] = jnp.max(sb, axis=0, keepdims=True)


def _softmax_update(m_ref, acc_ref, parked, slot, values):
    s_buf, bm_ref = parked
    m_old = m_ref[...]
    m_new = jnp.maximum(m_old, bm_ref[slot])
    pb = jnp.exp2(s_buf[slot] - m_new).astype(BF16)
    acc_ref[...] = jnp.exp2(m_old - m_new) * acc_ref[...] + values(pb)
    m_ref[...] = m_new


def _online_blocks(i, parked, m_ref, acc_ref, scores, values, causal):
    block = lambda kb: jnp.maximum(kb, 0)

    def park(slot, kb, live, mask=None):
        sb = scores(block(kb), live)
        _park(parked, slot, sb if mask is None else jnp.where(mask, sb, NEG))

    def update(slot, kb):
        _softmax_update(m_ref, acc_ref, parked, slot, lambda pb: values(block(kb), pb))

    _softmax_reset(m_ref, acc_ref)
    first = -((i + 1) & 1)
    n_pairs = (i + 2) // 2
    park(0, first, first >= 0)

    def trip(p):
        ka = first + 2 * p
        park(1, ka + 1, True)
        update(0, ka)
        park(0, ka + 2, True)
        update(1, ka + 1)

    def finish():
        park(1, i, True, causal)
        update(0, i - 1)
        update(1, i)
        return acc_ref[...]

    return trip, finish


def _online_blocks_diag_first(i, parked, m_ref, acc_ref, gate, diag_scores, select, values, causal):
    past = lambda x: jnp.clip(x - 1, 0, jnp.maximum(i - 1, 0))

    def park(slot, x):
        _park(parked, slot, scores(past(x), x - 1 < i))

    def update(slot, kb):
        _softmax_update(m_ref, acc_ref, parked, slot, lambda pb: values(kb, pb))

    _softmax_reset(m_ref, acc_ref)
    gates = gate()
    _park(parked, 0, jnp.where(causal, diag_scores(), NEG))
    scores = select(gates)
    park(1, 1)
    update(0, i)
    park(0, 2)
    update(1, past(1))

    def trip(p):
        park(1, 2 * p + 1)
        update(0, past(2 * p))
        park(0, 2 * p + 2)
        update(1, past(2 * p + 1))

    return trip, lambda: acc_ref[...]


def _nsa_body(q_ref, qr_ref, gn_ref, kc_ref, vct_ref, ke_ref, vts_ref, kw_ref, vtw_ref, ovt_ref,
              y_ref, s_buf, bm_ref, m_ref, acc_ref, rank_ref, *, n_sel_blocks):
    tq = q_ref.shape[1]
    ng, rg = NSA_KV_GROUPS, NSA_GROUP_SIZE
    nh = ng * rg
    i = pl.program_id(1)
    t0 = i * tq
    head = lambda a, h: a[h * HEAD_DIM:(h + 1) * HEAD_DIM, :]
    cols = lambda a, h, n=1: a[:, h * tq:(h + n) * tq]
    q = q_ref[...]
    qr = qr_ref[...]
    tpos = t0 + lax.broadcasted_iota(jnp.int32, (1, tq), 1)
    tpos_n = jnp.concatenate([tpos] * nh, axis=1)

    zero = jnp.zeros((HEAD_DIM, tq), BF16)
    place = lambda h, a: jnp.concatenate([a, zero] if h < rg else [zero, a], axis=0)
    qc = jnp.concatenate([place(h, head(q, h)) for h in range(nh)], axis=1)
    qw = jnp.concatenate([place(h, head(qr, h)) for h in range(nh)], axis=1)
    per_group = lambda v_ref, kb, pb: jnp.concatenate(
        [_dot(v_ref[g, kb], cols(pb, g * rg, rg)) for g in range(ng)], axis=1)

    causal = (lax.broadcasted_iota(jnp.int32, (KEY_BLOCK, 1), 0)
              <= (lax.broadcasted_iota(jnp.int32, (1, nh * tq), 1) & (tq - 1)))

    ncp = kc_ref.shape[0]
    i1 = jnp.maximum(i - 1, 0)
    i2 = jnp.maximum(i - 2, 0)
    n = lax.broadcasted_iota(jnp.int32, (ncp, 1), 0)
    parked = (s_buf, bm_ref)
    _park(parked, 0, jnp.where((n * CMP_STRIDE + (CMP_BLOCK - 1)) <= tpos_n, _dot(kc_ref[...], qc), NEG))
    _park(parked, 1, jnp.where(causal, _dot(kw_ref[i], qw), NEG))

    p = jnp.exp2(s_buf[0, 0:ncp, :] - bm_ref[0])
    pb16 = p.astype(BF16)
    acc = jnp.concatenate([_dot(vct_ref[g], cols(pb16, g * rg, rg)) for g in range(ng)], axis=1)
    inv = jnp.where(tpos_n >= CMP_BLOCK - 1, 1.0 / jnp.maximum(acc[HEAD_DIM:HEAD_DIM + 1, :], 1e-30), 0.0)
    o_cmp = acc[0:HEAD_DIM, :] * inv

    pn = p * inv
    ps = jnp.concatenate([sum(cols(pn, g * rg + r) for r in range(1, rg)) + cols(pn, g * rg)
                          for g in range(ng)], axis=1)
    ps_hi = ps.astype(BF16)
    ps_lo = (ps - ps_hi.astype(F32)).astype(BF16)
    imp = _dot(ovt_ref[...], ps_hi) + _dot(ovt_ref[...], ps_lo)
    jb = lax.broadcasted_iota(jnp.int32, (imp.shape[0], 1), 0)
    cur = jnp.concatenate([tpos >> (SEL_BLOCK.bit_length() - 1)] * ng, axis=1)
    forced = (jb == 0) | (jb == cur) | (jb == cur - 1)
    imp = jnp.where(forced, jnp.inf, imp)
    imp = jnp.where(jb <= cur, imp, -jnp.inf)

    _park(parked, 0, jnp.where(i >= 1, _dot(kw_ref[i1], qw), NEG))
    _softmax_reset(m_ref, acc_ref)
    _softmax_update(m_ref, acc_ref, parked, 1, lambda pb: per_group(vtw_ref, i, pb))
    _park(parked, 1, jnp.where(jnp.logical_or(causal, i < 2), NEG, _dot(kw_ref[i2], qw)))
    _softmax_update(m_ref, acc_ref, parked, 0, lambda pb: per_group(vtw_ref, i1, pb))
    _softmax_update(m_ref, acc_ref, parked, 1, lambda pb: per_group(vtw_ref, i2, pb))
    o_win = _finish(acc_ref[...])

    quarter = n_sel_blocks // 4
    last_blk = (tq // SEL_BLOCK) * (i + 1) - 1
    rank_ref[...] = _rank_rows(imp, jnp.zeros(imp.shape, F32), 0, quarter)
    for c in range(1, 4):
        @pl.when(last_blk >= c * quarter)
        def _():
            rank_ref[...] = _rank_rows(imp, rank_ref[...], c * quarter, (c + 1) * quarter)
    rank = rank_ref[...]
    bias = jnp.where(rank < min(SEL_TOPN, n_sel_blocks), 0.0, NEG).astype(BF16)

    off = jnp.full((bias.shape[0], tq), NEG, BF16)
    aug = lambda b_rows: jnp.concatenate(
        [jnp.concatenate([cols(qw, h), b_rows(h // rg), zero], axis=0) for h in range(nh)], axis=1)
    qa = aug(lambda g: cols(bias, g))
    qa_off = aug(lambda g: off)
    trip, last_blocks = _online_blocks(
        i, parked, m_ref, acc_ref,
        lambda kb, live: _dot(ke_ref[kb], qa if live is True else jnp.where(live, qa, qa_off)),
        lambda kb, pb: per_group(vts_ref, kb, pb),
        causal)

    def finish():
        o_sel = _finish(last_blocks())
        for h in range(nh):
            gates = gn_ref[h // rg]
            r = h % rg
            y_ref[h * HEAD_DIM:(h + 1) * HEAD_DIM, :] = (
                gates[3 * r:3 * r + 1, :] * cols(o_cmp, h)
                + gates[3 * r + 1:3 * r + 2, :] * cols(o_sel, h)
                + gates[3 * r + 2:3 * r + 3, :] * cols(o_win, h)).astype(y_ref.dtype)

    return trip, finish


def _moba_body(q_ref, km_ref, ke_ref, vt_ref, y_ref, s_buf, bm_ref, m_ref, acc_ref, *, n_blocks):
    tq = q_ref.shape[1]
    nh = km_ref.shape[0]
    i = pl.program_id(1)
    q = q_ref[...] * SCALE
    qb = q.astype(BF16)
    head = lambda a, h: a[h * HEAD_DIM:(h + 1) * HEAD_DIM, :]
    zero = jnp.zeros((HEAD_DIM, tq), BF16)
    n_bias = km_ref.shape[1]
    pad = jnp.zeros((K_AUG - LANES - n_bias, tq), BF16)

    def aug(h, b_rows):
        pair = [head(qb, h), zero] if h % 2 == 0 else [zero, head(qb, h)]
        return jnp.concatenate(pair + [b_rows, pad], axis=0)

    def keys(kb, h):
        return ke_ref[kb, :, (h // 2) * K_AUG:(h // 2 + 1) * K_AUG]

    def diag_scores():
        open_rows = jnp.zeros((n_bias, tq), BF16)
        return jnp.concatenate([_dot(keys(i, h), aug(h, open_rows)) for h in range(nh)], axis=1)

    def gate():
        q_lo = (q - qb.astype(F32)).astype(BF16)
        gates = []
        for h in range(nh):
            km = km_ref[h]
            km_hi = km.astype(BF16)
            km_lo = (km - km_hi.astype(F32)).astype(BF16)
            gates.append(_dot(km_hi, head(qb, h)) + (_dot(km_hi, head(q_lo, h)) + _dot(km_lo, head(qb, h))))
        return jnp.concatenate(gates, axis=1)

    def select(gate):
        jb = lax.broadcasted_iota(jnp.int32, (gate.shape[0], 1), 0)
        gate = jnp.where(jb < i, gate, -jnp.inf)
        sel = _topk_mask(gate, n_blocks, min(MOBA_TOPK, n_blocks)) & (gate > -jnp.inf)
        bias = jnp.where(sel, 0.0, NEG).astype(BF16)
        off = jnp.full((n_bias, tq), NEG, BF16)
        qa = [aug(h, bias[:, h * tq:(h + 1) * tq]) for h in range(nh)]
        qa_off = [aug(h, off) for h in range(nh)]
        return lambda kb, live: jnp.concatenate(
            [_dot(keys(kb, h), jnp.where(live, qa[h], qa_off[h])) for h in range(nh)], axis=1)

    def values(kb, pb):
        return jnp.concatenate([_dot(vt_ref[h, kb], pb[:, h * tq:(h + 1) * tq]) for h in range(nh)], axis=1)

    causal = (lax.broadcasted_iota(jnp.int32, (KEY_BLOCK, 1), 0)
              <= (lax.broadcasted_iota(jnp.int32, (1, nh * tq), 1) & (tq - 1)))
    trip, last_blocks = _online_blocks_diag_first(i, (s_buf, bm_ref), m_ref, acc_ref,
                                                  gate, diag_scores, select, values, causal)

    def finish():
        o = _finish(last_blocks())
        for h in range(nh):
            y_ref[h * HEAD_DIM:(h + 1) * HEAD_DIM, :] = o[:, h * tq:(h + 1) * tq].astype(y_ref.dtype)

    return trip, finish


N_NSA_IN = 10
N_MOBA_IN = 4


def _mixers_body(*refs, n_sel_blocks, n_blocks):
    nsa_in = refs[:N_NSA_IN]
    moba_in = refs[N_NSA_IN:N_NSA_IN + N_MOBA_IN]
    y_nsa, y_moba = refs[N_NSA_IN + N_MOBA_IN:N_NSA_IN + N_MOBA_IN + 2]
    scratch = refs[N_NSA_IN + N_MOBA_IN + 2:]
    nsa_trip, nsa_finish = _nsa_body(*nsa_in, y_nsa, *scratch[:5], n_sel_blocks=n_sel_blocks)
    moba_trip, moba_finish = _moba_body(*moba_in, y_moba, *scratch[5:], n_blocks=n_blocks)
    i = pl.program_id(1)

    @pl.loop(0, (i + 2) // 2 - 1)
    def _(p):
        nsa_trip(p)
        moba_trip(p + 1)

    nsa_finish()
    moba_finish()


def _mixers_call(qt, gnt, kc, vct, ke, vts, kw, vtw, ovt, qmt, kmean, kem, vtm):
    b, _, seq = qt.shape
    g = NSA_KV_GROUPS
    nkb = seq // KEY_BLOCK
    ncp = kc.shape[1]
    tq = Q_TILE
    n_cols = NSA_HEADS * tq
    tile = lambda bi, i: (bi, 0, i)
    per_batch = lambda shape: pl.BlockSpec((None,) + shape, lambda bi, i: (bi,) + (0,) * len(shape))
    return pl.pallas_call(
        functools.partial(_mixers_body, n_sel_blocks=seq // SEL_BLOCK, n_blocks=nkb),
        grid=(b, seq // tq),
        in_specs=[pl.BlockSpec((None, NSA_WIDTH, tq), tile),
                  pl.BlockSpec((None, NSA_WIDTH, tq), lambda bi, i: (bi, 1, i)),
                  pl.BlockSpec((None, g, GATE_ROWS, tq), lambda bi, i: (bi, 0, 0, i)),
                  per_batch((ncp, LANES)),
                  per_batch((g, V_ROWS, ncp)),
                  per_batch((nkb, KEY_BLOCK, K_AUG)),
                  per_batch((g, nkb, V_ROWS, KEY_BLOCK)),
                  per_batch((nkb, KEY_BLOCK, LANES)),
                  per_batch((g, nkb, V_ROWS, KEY_BLOCK)),
                  pl.BlockSpec((HEAD_DIM, ncp), lambda bi, i: (0, 0)),
                  pl.BlockSpec((None, MOBA_WIDTH, tq), tile),
                  per_batch((MOBA_HEADS, MOBA_MEAN_ROWS, HEAD_DIM)),
                  per_batch((nkb, KEY_BLOCK, (MOBA_HEADS // 2) * K_AUG)),
                  per_batch((MOBA_HEADS, nkb, V_ROWS, KEY_BLOCK))],
        out_specs=[pl.BlockSpec((None, NSA_WIDTH, tq), tile), pl.BlockSpec((None, MOBA_WIDTH, tq), tile)],
        out_shape=[jax.ShapeDtypeStruct((b, NSA_WIDTH, seq), BF16),
                   jax.ShapeDtypeStruct((b, MOBA_WIDTH, seq), BF16)],
        scratch_shapes=[pltpu.VMEM((2, KEY_BLOCK, n_cols), F32),
                        pltpu.VMEM((2, 1, n_cols), F32),
                        pltpu.VMEM((1, n_cols), F32),
                        pltpu.VMEM((V_ROWS, n_cols), F32),
                        pltpu.VMEM((HEAD_DIM, g * tq), F32),
                        pltpu.VMEM((2, KEY_BLOCK, n_cols), F32),
                        pltpu.VMEM((2, 1, n_cols), F32),
                        pltpu.VMEM((1, n_cols), F32),
                        pltpu.VMEM((V_ROWS, n_cols), F32)],
        compiler_params=pltpu.CompilerParams(
            dimension_semantics=("parallel", "arbitrary"), vmem_limit_bytes=VMEM_LIMIT),
        name="mixers",
    )(qt, qt, gnt, kc, vct, ke, vts, kw, vtw, ovt, qmt, kmean, kem, vtm)


def _merge_ffn_body(x_ref, yn_ref, ym_ref, gm_ref, wun_ref, wum_ref, wo_ref, g2_ref,
                    w1_ref, w2_ref, gf_ref, o_ref):
    tn = (((0,), (0,)), ((), ()))
    a = lax.dot_general(yn_ref[...], wun_ref[...], tn, preferred_element_type=F32)
    c = lax.dot_general(ym_ref[...], wum_ref[...], tn, preferred_element_type=F32)
    mixed = gm_ref[:, 0:D_MODEL] * a + gm_ref[:, D_MODEL:2 * D_MODEL] * c
    x1 = x_ref[...] + _dot(mixed.astype(BF16), wo_ref[...])
    ms = jnp.mean(x1 * x1, axis=-1, keepdims=True)
    h2 = (x1 * lax.rsqrt(ms + NORM_EPS) * g2_ref[...]).astype(BF16)
    x2 = x1
    for k in range(D_FF // FF_CHUNK):
        u = jnp.maximum(_dot(h2, w1_ref[:, k * FF_CHUNK:(k + 1) * FF_CHUNK]), 0.0)
        x2 = x2 + _dot((u * u).astype(BF16), w2_ref[k * FF_CHUNK:(k + 1) * FF_CHUNK, :])
    ms = jnp.mean(x2 * x2, axis=-1, keepdims=True)
    o_ref[...] = x2 * lax.rsqrt(ms + NORM_EPS) * gf_ref[...]


def _merge_ffn(x2, ynt, ymt, gm, wun, wum, wo, g2, w1, w2, gf):
    m = x2.shape[0]
    seq = ynt.shape[2]
    tm = 512
    nt = seq // tm
    row = lambda i: (i, 0)
    const = lambda i: (0, 0)
    feat = lambda i: (i // nt, 0, i % nt)
    resident = lambda shape: pl.BlockSpec(shape, const, pipeline_mode=pl.Buffered(1))
    return pl.pallas_call(
        _merge_ffn_body,
        grid=(m // tm,),
        in_specs=[pl.BlockSpec((tm, D_MODEL), row),
                  pl.BlockSpec((None, NSA_WIDTH, tm), feat),
                  pl.BlockSpec((None, MOBA_WIDTH, tm), feat),
                  pl.BlockSpec((tm, 2 * D_MODEL), row),
                  resident((NSA_WIDTH, D_MODEL)),
                  resident((MOBA_WIDTH, D_MODEL)),
                  resident((D_MODEL, D_MODEL)),
                  pl.BlockSpec((1, D_MODEL), const),
                  resident((D_MODEL, D_FF)),
                  resident((D_FF, D_MODEL)),
                  pl.BlockSpec((1, D_MODEL), const)],
        out_specs=pl.BlockSpec((tm, D_MODEL), row),
        out_shape=jax.ShapeDtypeStruct((m, D_MODEL), F32),
        compiler_params=pltpu.CompilerParams(dimension_semantics=("parallel",),
                                             vmem_limit_bytes=VMEM_LIMIT),
        name="merge_ffn",
    )(x2, ynt, ymt, gm, wun, wum, wo, g2, w1, w2, gf)


def _rope_tables(seq):
    half = ROT_DIM // 2
    inv = ROPE_THETA ** (-jnp.arange(0, ROT_DIM, 2, dtype=F32) / ROT_DIM)
    ang = jnp.arange(seq, dtype=F32)[:, None] * inv[None, :]
    cos, sin = jnp.cos(ang), jnp.sin(ang)
    pad = HEAD_DIM - ROT_DIM
    one_head = lambda a, b, fill: jnp.concatenate([a, b, jnp.full((seq, pad), fill, F32)], axis=1)
    zeros = jnp.zeros((seq, half), F32)
    two = lambda a: jnp.concatenate([a, a], axis=1)
    return (two(one_head(cos, cos, 1.0)),
            two(one_head(-sin, zeros, 0.0)),
            two(one_head(zeros, sin, 0.0)),
            cos.T, sin.T)


def _overlap_t(ncp, nsb):
    i = np.arange(ncp)[None, :]
    j = np.arange(HEAD_DIM)[:, None]
    start = i * CMP_STRIDE
    end = start + CMP_BLOCK - 1
    ov = (end >= j * SEL_BLOCK) & (start <= j * SEL_BLOCK + SEL_BLOCK - 1) & (i < ncp - 1) & (j < nsb)
    return jnp.asarray(ov.astype(np.float32), dtype=BF16)


def _block_diag(w):
    z = jnp.zeros_like(w)
    return jnp.concatenate([jnp.concatenate([w, z], axis=-1), jnp.concatenate([z, w], axis=-1)], axis=-2)


def _projection_weights(w):
    kv = lambda j: w[:, OFF_KV + j * KV_WIDTH:OFF_KV + (j + 1) * KV_WIDTH]
    mo = lambda j: w[:, OFF_M + j * MOBA_WIDTH:OFF_M + (j + 1) * MOBA_WIDTH]
    wa = jnp.concatenate([kv(0), kv(1), kv(2), kv(4), mo(1), w[:, OFF_GM:]], axis=1)
    gn = w[:, OFF_GN:OFF_M].T.reshape(NSA_KV_GROUPS, 3 * NSA_GROUP_SIZE, D_MODEL)
    gn = jnp.concatenate([gn, jnp.zeros((NSA_KV_GROUPS, GATE_ROWS - 3 * NSA_GROUP_SIZE, D_MODEL), w.dtype)], axis=1)
    wb = jnp.concatenate([w[:, :OFF_KV].T, mo(0).T, kv(3).T, kv(5).T, mo(2).T,
                          gn.reshape(NSA_KV_GROUPS * GATE_ROWS, D_MODEL)], axis=0)
    return wa.astype(BF16), wb.astype(BF16)


def _mixers(x, norm1_g, w_in, cmp_pe_k, cmp_pe_v, cmp_k_w1, cmp_k_w2, cmp_v_w1, cmp_v_w2):
    b, seq, _ = x.shape
    g, d = NSA_KV_GROUPS, HEAD_DIM
    nkb = seq // KEY_BLOCK
    ncp = seq // CMP_STRIDE
    nsb = seq // SEL_BLOCK
    layer = 0

    wa, wb = _projection_weights(w_in[layer])
    x2 = x.reshape(b * seq, D_MODEL)
    (qt, qmt, gnt, kcx, vcx, ke, kw, vts, vtw, kem, vtm, kmean8, ogm) = _in_proj(
        x2, norm1_g[layer][None, :], wa, wb, _rope_tables(seq), b, seq)

    per_group = lambda a: jnp.concatenate([a] * g, axis=-1)
    pe_rows = lambda pe: jnp.broadcast_to(per_group(pe)[:, None, :], (CMP_BLOCK, SUBLANES, LANES)).astype(BF16)
    w1_blocks = lambda w1: _block_diag(w1.reshape(CMP_BLOCK, d, d)).astype(BF16)
    kc, vct = _compress(
        kcx.reshape(b, seq, KV_WIDTH), vcx.reshape(b, seq, KV_WIDTH),
        pe_rows(cmp_pe_k[layer]), pe_rows(cmp_pe_v[layer]),
        w1_blocks(cmp_k_w1[layer]), w1_blocks(cmp_v_w1[layer]),
        _block_diag(cmp_k_w2[layer]).astype(BF16), _block_diag(cmp_v_w2[layer].T).astype(BF16))
    kmean = kmean8[:, ::SUBLANES, :].reshape(b, nkb, MOBA_HEADS, d)
    kmean = jnp.swapaxes(kmean, 1, 2)
    kmean = jnp.concatenate([kmean, jnp.zeros((b, MOBA_HEADS, MOBA_MEAN_ROWS - nkb, d), F32)], axis=2)
    ynt, ymt = _mixers_call(qt, gnt, kc, vct, ke, vts, kw, vtw, _overlap_t(ncp, nsb),
                            qmt, kmean, kem, vtm)
    return x2, ynt, ymt, ogm


def kernel(x, norm1_g, w_in, cmp_pe_k, cmp_pe_v, cmp_k_w1, cmp_k_w2, cmp_v_w1, cmp_v_w2,
           w_up_nsa, w_up_moba, w_out, norm2_g, w_ff1, w_ff2, norm_f_g):
    b, seq, _ = x.shape
    layer = 0
    x2, ynt, ymt, ogm = _mixers(x, norm1_g, w_in, cmp_pe_k, cmp_pe_v, cmp_k_w1, cmp_k_w2, cmp_v_w1, cmp_v_w2)
    out = _merge_ffn(x2, ynt, ymt, ogm, w_up_nsa[layer].astype(BF16), w_up_moba[layer].astype(BF16),
                     w_out[layer].astype(BF16), norm2_g[layer][None, :],
                     w_ff1[layer].astype(BF16), w_ff2[layer].astype(BF16), norm_f_g[None, :])
    return out.reshape(b, seq, D_MODEL)
```

```python
import functools

import numpy as np
import jax
import jax.numpy as jnp
from jax import lax
from jax.experimental import pallas as pl
from jax.experimental.pallas import tpu as pltpu

F32 = jnp.float32
BF16 = jnp.bfloat16

D_MODEL = 1024
HEAD_DIM = 64
ROT_DIM = HEAD_DIM // 4
ROPE_THETA = 500000.0
NORM_EPS = 1e-6

NSA_HEADS = 8
NSA_KV_GROUPS = 2
NSA_GROUP_SIZE = NSA_HEADS // NSA_KV_GROUPS
CMP_BLOCK = 32
CMP_STRIDE = 16
SEL_BLOCK = 64
SEL_TOPN = 16
WINDOW = 512

MOBA_HEADS = 8
MOBA_BLOCK = 256
MOBA_TOPK = 3

D_FF = 4 * D_MODEL
NSA_WIDTH = NSA_HEADS * HEAD_DIM
KV_WIDTH = NSA_KV_GROUPS * HEAD_DIM
MOBA_WIDTH = MOBA_HEADS * HEAD_DIM
OFF_KV = NSA_WIDTH
OFF_GN = OFF_KV + 6 * KV_WIDTH
OFF_M = OFF_GN + 3 * NSA_HEADS
OFF_GM = OFF_M + 3 * MOBA_WIDTH
IN_WIDTH = OFF_GM + 2 * D_MODEL

LANES = 128
SUBLANES = 8
KEY_BLOCK = 256
Q_TILE = 256
V_ROWS = 80
K_AUG = 2 * LANES
NEG = -1e30
SCALE = float(HEAD_DIM ** -0.5 * np.log2(np.e))
GATE_ROWS = 16
FF_CHUNK = 1024
VMEM_LIMIT = 56 * 1024 * 1024

A_KC, A_KSW, A_KM, A_GM = 0, 256, 512, 1024
A_WIDTH = A_GM + 2 * D_MODEL
B_QN, B_QM, B_V, B_VM, B_GN = 0, 512, 1024, 1280, 1792
B_ROWS = B_GN + NSA_KV_GROUPS * GATE_ROWS


def _dot(a, b):
    return jnp.dot(a, b, preferred_element_type=F32)


def _dot_nt(a, b):
    return lax.dot_general(a, b, (((1,), (1,)), ((), ())), preferred_element_type=F32)


def _sigmoid(y):
    return 1.0 / (1.0 + jnp.exp(-y))


def _in_proj_body(x_ref, g_ref, wa_ref, wb_ref, c_ref, s1_ref, s2_ref, ct_ref, st_ref,
                  qt_ref, qmt_ref, gnt_ref, kcx_ref, vcx_ref, ke_ref, kw_ref, vts_ref, vtw_ref,
                  kem_ref, vtm_ref, kmean_ref, gm_ref, h_ref, *, tiles_per_seq):
    tm = x_ref.shape[0]
    it = pl.program_id(0) % tiles_per_seq
    x = x_ref[...]
    ms = jnp.mean(x * x, axis=-1, keepdims=True)
    h_ref[...] = (x * lax.rsqrt(ms + NORM_EPS) * g_ref[...]).astype(BF16)

    cos = c_ref[...]
    sin_hi = s1_ref[...]
    sin_lo = s2_ref[...]

    def proj(c0, width):
        return _dot(h_ref[...], wa_ref[:, c0:c0 + width])

    def rope(yc):
        return (yc * cos + pltpu.roll(yc, LANES - ROT_DIM // 2, 1) * sin_hi
                + pltpu.roll(yc, ROT_DIM // 2, 1) * sin_lo)

    y = proj(A_KC, 2 * KV_WIDTH)
    kcx_ref[...] = y[:, 0:KV_WIDTH]
    vcx_ref[...] = y[:, KV_WIDTH:2 * KV_WIDTH]

    row = lax.broadcasted_iota(jnp.int32, (tm, 1), 0)
    lane = lax.broadcasted_iota(jnp.int32, (1, LANES), 1)
    y = proj(A_KSW, 2 * KV_WIDTH)
    ke_ref[:, 0:LANES] = rope(y[:, 0:LANES]).astype(BF16)
    sel_blk = (it * tm + row) >> (SEL_BLOCK.bit_length() - 1)
    ke_ref[:, LANES:K_AUG] = jnp.where(sel_blk == lane, 1.0, 0.0).astype(BF16)
    kw_ref[...] = rope(y[:, LANES:2 * LANES]).astype(BF16)

    y = proj(A_KM, MOBA_WIDTH)
    own_blk = jnp.where(lane == it, 1.0, 0.0).astype(BF16)
    sums = []
    for j in range(MOBA_WIDTH // LANES):
        kr = rope(y[:, j * LANES:(j + 1) * LANES])
        sums.append(jnp.mean(kr, axis=0, keepdims=True))
        kem_ref[:, j * K_AUG:j * K_AUG + LANES] = kr.astype(BF16)
        kem_ref[:, j * K_AUG + LANES:(j + 1) * K_AUG] = jnp.broadcast_to(own_blk, (tm, LANES))
    kmean_ref[...] = jnp.broadcast_to(jnp.concatenate(sums, axis=1), kmean_ref.shape)

    for part in range(4):
        w = 2 * D_MODEL // 4
        gm_ref[:, part * w:(part + 1) * w] = _sigmoid(proj(A_GM + part * w, w)).astype(BF16)

    cos_t = ct_ref[...]
    sin_t = st_ref[...]

    def proj_t(r0, rows):
        return _dot_nt(wb_ref[r0:r0 + rows, :], h_ref[...])

    def rope_t(yt):
        half = ROT_DIM // 2
        out = []
        for hd in range(yt.shape[0] // HEAD_DIM):
            a = yt[hd * HEAD_DIM:hd * HEAD_DIM + half, :]
            b = yt[hd * HEAD_DIM + half:hd * HEAD_DIM + ROT_DIM, :]
            out += [a * cos_t - b * sin_t, a * sin_t + b * cos_t,
                    yt[hd * HEAD_DIM + ROT_DIM:(hd + 1) * HEAD_DIM, :]]
        return jnp.concatenate(out, axis=0)

    y = proj_t(B_QN, NSA_WIDTH)
    qt_ref[0:NSA_WIDTH, :] = (y * SCALE).astype(BF16)
    qt_ref[NSA_WIDTH:2 * NSA_WIDTH, :] = (rope_t(y) * SCALE).astype(BF16)
    qm = rope_t(proj_t(B_QM, MOBA_WIDTH)) * SCALE
    qm_hi = qm.astype(BF16)
    qmt_ref[0:MOBA_WIDTH, :] = qm_hi
    qmt_ref[MOBA_WIDTH:2 * MOBA_WIDTH, :] = (qm - qm_hi.astype(F32)).astype(BF16)

    ones_row = jnp.where(lax.broadcasted_iota(jnp.int32, (V_ROWS - HEAD_DIM, tm), 0) == 0, 1.0, 0.0).astype(BF16)
    y = proj_t(B_V, 4 * HEAD_DIM)
    for g in range(NSA_KV_GROUPS):
        vts_ref[g, 0:HEAD_DIM, :] = y[g * HEAD_DIM:(g + 1) * HEAD_DIM, :].astype(BF16)
        vts_ref[g, HEAD_DIM:V_ROWS, :] = ones_row
        vtw_ref[g, 0:HEAD_DIM, :] = y[(2 + g) * HEAD_DIM:(3 + g) * HEAD_DIM, :].astype(BF16)
        vtw_ref[g, HEAD_DIM:V_ROWS, :] = ones_row
    y = proj_t(B_VM, MOBA_WIDTH)
    for hd in range(MOBA_HEADS):
        vtm_ref[hd, 0:HEAD_DIM, :] = y[hd * HEAD_DIM:(hd + 1) * HEAD_DIM, :].astype(BF16)
        vtm_ref[hd, HEAD_DIM:V_ROWS, :] = ones_row
    y = _sigmoid(proj_t(B_GN, NSA_KV_GROUPS * GATE_ROWS))
    for g in range(NSA_KV_GROUPS):
        gnt_ref[g] = y[g * GATE_ROWS:(g + 1) * GATE_ROWS, :]


def _in_proj(x2, g1, wa, wb, tabs, batch, seq):
    m = x2.shape[0]
    tm = KEY_BLOCK
    nt = seq // tm
    row = lambda i: (i, 0)
    const = lambda i: (0, 0)
    tab = lambda i: (i % nt, 0)
    tab_t = lambda i: (0, i % nt)
    feat = lambda i: (i // nt, 0, i % nt)
    blk4 = lambda i: (i // nt, i % nt, 0, 0)
    blk5 = lambda i: (i // nt, 0, i % nt, 0, 0)
    g = NSA_KV_GROUPS
    outs = [
        (jax.ShapeDtypeStruct((batch, 2 * NSA_WIDTH, seq), BF16), pl.BlockSpec((None, 2 * NSA_WIDTH, tm), feat)),
        (jax.ShapeDtypeStruct((batch, 2 * MOBA_WIDTH, seq), BF16), pl.BlockSpec((None, 2 * MOBA_WIDTH, tm), feat)),
        (jax.ShapeDtypeStruct((batch, g, GATE_ROWS, seq), F32),
         pl.BlockSpec((None, g, GATE_ROWS, tm), lambda i: (i // nt, 0, 0, i % nt))),
        (jax.ShapeDtypeStruct((m, KV_WIDTH), F32), pl.BlockSpec((tm, KV_WIDTH), row)),
        (jax.ShapeDtypeStruct((m, KV_WIDTH), F32), pl.BlockSpec((tm, KV_WIDTH), row)),
        (jax.ShapeDtypeStruct((batch, nt, tm, K_AUG), BF16), pl.BlockSpec((None, None, tm, K_AUG), blk4)),
        (jax.ShapeDtypeStruct((batch, nt, tm, LANES), BF16), pl.BlockSpec((None, None, tm, LANES), blk4)),
        (jax.ShapeDtypeStruct((batch, g, nt, V_ROWS, tm), BF16), pl.BlockSpec((None, g, None, V_ROWS, tm), blk5)),
        (jax.ShapeDtypeStruct((batch, g, nt, V_ROWS, tm), BF16), pl.BlockSpec((None, g, None, V_ROWS, tm), blk5)),
        (jax.ShapeDtypeStruct((batch, nt, tm, (MOBA_HEADS // 2) * K_AUG), BF16),
         pl.BlockSpec((None, None, tm, (MOBA_HEADS // 2) * K_AUG), blk4)),
        (jax.ShapeDtypeStruct((batch, MOBA_HEADS, nt, V_ROWS, tm), BF16),
         pl.BlockSpec((None, MOBA_HEADS, None, V_ROWS, tm), blk5)),
        (jax.ShapeDtypeStruct((batch, nt * SUBLANES, MOBA_WIDTH), F32),
         pl.BlockSpec((None, SUBLANES, MOBA_WIDTH), lambda i: (i // nt, i % nt, 0))),
        (jax.ShapeDtypeStruct((m, 2 * D_MODEL), BF16), pl.BlockSpec((tm, 2 * D_MODEL), row)),
    ]
    return pl.pallas_call(
        functools.partial(_in_proj_body, tiles_per_seq=nt),
        grid=(m // tm,),
        in_specs=[pl.BlockSpec((tm, D_MODEL), row),
                  pl.BlockSpec((1, D_MODEL), const),
                  pl.BlockSpec((D_MODEL, A_WIDTH), const),
                  pl.BlockSpec((B_ROWS, D_MODEL), const),
                  pl.BlockSpec((tm, LANES), tab),
                  pl.BlockSpec((tm, LANES), tab),
                  pl.BlockSpec((tm, LANES), tab),
                  pl.BlockSpec((SUBLANES, tm), tab_t),
                  pl.BlockSpec((SUBLANES, tm), tab_t)],
        out_specs=[o[1] for o in outs],
        out_shape=[o[0] for o in outs],
        scratch_shapes=[pltpu.VMEM((tm, D_MODEL), BF16)],
        compiler_params=pltpu.CompilerParams(dimension_semantics=("parallel",),
                                             vmem_limit_bytes=VMEM_LIMIT),
        name="in_proj",
    )(x2, g1, wa, wb, *tabs)


def _gelu_tanh(x):
    return 0.5 * x * (1.0 + jnp.tanh(np.sqrt(2.0 / np.pi) * (x + 0.044715 * (x * x * x))))


def _compress_body(kx_ref, vx_ref, pek_ref, pev_ref, kw1_ref, vw1_ref, kw2_ref, vw2t_ref,
                   kc_ref, vct_ref):
    ncp = kc_ref.shape[0]

    def hidden(x_ref, pe_ref, w1_ref):
        ya = jnp.zeros((ncp, LANES), F32)
        yb = jnp.zeros((ncp, LANES), F32)
        pe = jnp.zeros((SUBLANES, LANES), F32)
        for r in range(CMP_STRIDE):
            xr = x_ref[pl.ds(r, ncp, stride=CMP_STRIDE), :].astype(BF16)
            ya = ya + _dot(xr, w1_ref[r])
            yb = yb + _dot(xr, w1_ref[CMP_STRIDE + r])
        for l in range(CMP_BLOCK):
            pe = pe + _dot(pe_ref[l], w1_ref[l])
        return _gelu_tanh(ya + pltpu.roll(yb, ncp - 1, 0) + pe[0:1, :])

    hk = hidden(kx_ref, pek_ref, kw1_ref)
    kc_ref[...] = _dot(hk.astype(BF16), kw2_ref[...]).astype(BF16)
    hv = hidden(vx_ref, pev_ref, vw1_ref)
    vct = _dot_nt(vw2t_ref[...], hv.astype(BF16)).astype(BF16)
    ones_row = jnp.where(lax.broadcasted_iota(jnp.int32, (V_ROWS - HEAD_DIM, ncp), 0) == 0, 1.0, 0.0).astype(BF16)
    for g in range(NSA_KV_GROUPS):
        vct_ref[g, 0:HEAD_DIM, :] = vct[g * HEAD_DIM:(g + 1) * HEAD_DIM, :]
        vct_ref[g, HEAD_DIM:V_ROWS, :] = ones_row


def _compress(kx, vx, pek, pev, kw1, vw1, kw2, vw2t):
    b, seq, _ = kx.shape
    ncp = seq // CMP_STRIDE
    g = NSA_KV_GROUPS
    tok = lambda bi: (bi, 0, 0)
    const3 = lambda bi: (0, 0, 0)
    const2 = lambda bi: (0, 0)
    return pl.pallas_call(
        _compress_body,
        grid=(b,),
        in_specs=[pl.BlockSpec((None, seq, LANES), tok),
                  pl.BlockSpec((None, seq, LANES), tok),
                  pl.BlockSpec((CMP_BLOCK, SUBLANES, LANES), const3),
                  pl.BlockSpec((CMP_BLOCK, SUBLANES, LANES), const3),
                  pl.BlockSpec((CMP_BLOCK, LANES, LANES), const3),
                  pl.BlockSpec((CMP_BLOCK, LANES, LANES), const3),
                  pl.BlockSpec((LANES, LANES), const2),
                  pl.BlockSpec((LANES, LANES), const2)],
        out_specs=[pl.BlockSpec((None, ncp, LANES), tok),
                   pl.BlockSpec((None, g, V_ROWS, ncp), lambda bi: (bi, 0, 0, 0))],
        out_shape=[jax.ShapeDtypeStruct((b, ncp, LANES), BF16),
                   jax.ShapeDtypeStruct((b, g, V_ROWS, ncp), BF16)],
        compiler_params=pltpu.CompilerParams(dimension_semantics=("parallel",)),
        name="compress",
    )(kx, vx, pek, pev, kw1, vw1, kw2, vw2t)


def _rank_rows(score, rank, k0, k1):
    rows = score.shape[0]
    j = lax.broadcasted_iota(jnp.int32, (SUBLANES, 1), 0)
    blocks = [score[b:b + SUBLANES, :] for b in range(0, rows, SUBLANES)]
    for k in range(k0, k1):
        rk = score[k:k + 1, :]
        before = lambda sb: jnp.where(rk > sb, 1.0, 0.0)
        unless = lambda sb: jnp.where(sb > rk, 0.0, 1.0)
        counts = []
        for b, sb in zip(range(0, rows, SUBLANES), blocks):
            if b + SUBLANES <= k:
                counts.append(before(sb))
            elif b > k:
                counts.append(unless(sb))
            else:
                counts.append(jnp.where(j + b > k, unless(sb), before(sb)))
        rank = rank + jnp.concatenate(counts, axis=0)
    return rank


def _topk_mask(score, n_rows, k_top):
    return _rank_rows(score, jnp.zeros(score.shape, F32), 0, n_rows) < k_top


def _finish(acc):
    return acc[0:HEAD_DIM, :] / acc[HEAD_DIM:HEAD_DIM + 1, :]


def _softmax_reset(m_ref, acc_ref):
    m_ref[...] = jnp.full(m_ref.shape, NEG, F32)
    acc_ref[...] = jnp.zeros(acc_ref.shape, F32)


def _park(parked, slot, sb):
    s_buf, bm_ref = parked
    s_buf[slot, 0:sb.shape[0], :] = sb
    bm_ref[slot] = jnp.max(sb, axis=0, keepdims=True)


def _softmax_update(m_ref, acc_ref, parked, slot, values):
    s_buf, bm_ref = parked
    m_old = m_ref[...]
    m_new = jnp.maximum(m_old, bm_ref[slot])
    pb = jnp.exp2(s_buf[slot] - m_new).astype(BF16)
    acc_ref[...] = jnp.exp2(m_old - m_new) * acc_ref[...] + values(pb)
    m_ref[...] = m_new


def _online_blocks(i, parked, m_ref, acc_ref, scores, values, causal):
    block = lambda kb: jnp.maximum(kb, 0)

    def park(slot, kb, live, mask=None):
        sb = scores(block(kb), live)
        _park(parked, slot, sb if mask is None else jnp.where(mask, sb, NEG))

    def update(slot, kb):
        _softmax_update(m_ref, acc_ref, parked, slot, lambda pb: values(block(kb), pb))

    _softmax_reset(m_ref, acc_ref)
    first = -((i + 1) & 1)
    n_pairs = (i + 2) // 2
    park(0, first, first >= 0)

    def trip(p):
        ka = first + 2 * p
        park(1, ka + 1, True)
        update(0, ka)
        park(0, ka + 2, True)
        update(1, ka + 1)

    def finish():
        park(1, i, True, causal)
        update(0, i - 1)
        update(1, i)
        return acc_ref[...]

    return trip, finish


def _online_blocks_diag_first(i, parked, m_ref, acc_ref, gate, diag_scores, select, values, causal):
    past = lambda x: jnp.clip(x - 1, 0, jnp.maximum(i - 1, 0))

    def park(slot, x):
        _park(parked, slot, scores(past(x), x - 1 < i))

    def update(slot, kb):
        _softmax_update(m_ref, acc_ref, parked, slot, lambda pb: values(kb, pb))

    _softmax_reset(m_ref, acc_ref)
    gates = gate()
    _park(parked, 0, jnp.where(causal, diag_scores(), NEG))
    scores = select(gates)
    park(1, 1)
    update(0, i)
    park(0, 2)
    update(1, past(1))

    def trip(p):
        park(1, 2 * p + 1)
        update(0, past(2 * p))
        park(0, 2 * p + 2)
        update(1, past(2 * p + 1))

    return trip, lambda: acc_ref[...]


def _nsa_body(q_ref, qr_ref, gn_ref, kc_ref, vct_ref, ke_ref, vts_ref, kw_ref, vtw_ref, ovt_ref,
              y_ref, s_buf, bm_ref, m_ref, acc_ref, rank_ref, *, n_sel_blocks):
    tq = q_ref.shape[1]
    ng, rg = NSA_KV_GROUPS, NSA_GROUP_SIZE
    nh = ng * rg
    i = pl.program_id(1)
    t0 = i * tq
    head = lambda a, h: a[h * HEAD_DIM:(h + 1) * HEAD_DIM, :]
    cols = lambda a, h, n=1: a[:, h * tq:(h + n) * tq]
    q = q_ref[...]
    qr = qr_ref[...]
    tpos = t0 + lax.broadcasted_iota(jnp.int32, (1, tq), 1)
    tpos_n = jnp.concatenate([tpos] * nh, axis=1)

    zero = jnp.zeros((HEAD_DIM, tq), BF16)
    place = lambda h, a: jnp.concatenate([a, zero] if h < rg else [zero, a], axis=0)
    qc = jnp.concatenate([place(h, head(q, h)) for h in range(nh)], axis=1)
    qw = jnp.concatenate([place(h, head(qr, h)) for h in range(nh)], axis=1)
    per_group = lambda v_ref, kb, pb: jnp.concatenate(
        [_dot(v_ref[g, kb], cols(pb, g * rg, rg)) for g in range(ng)], axis=1)

    causal = (lax.broadcasted_iota(jnp.int32, (KEY_BLOCK, 1), 0)
              <= (lax.broadcasted_iota(jnp.int32, (1, nh * tq), 1) & (tq - 1)))

    ncp = kc_ref.shape[0]
    i1 = jnp.maximum(i - 1, 0)
    i2 = jnp.maximum(i - 2, 0)
    n = lax.broadcasted_iota(jnp.int32, (ncp, 1), 0)
    parked = (s_buf, bm_ref)
    _park(parked, 0, jnp.where((n * CMP_STRIDE + (CMP_BLOCK - 1)) <= tpos_n, _dot(kc_ref[...], qc), NEG))
    _park(parked, 1, jnp.where(causal, _dot(kw_ref[i], qw), NEG))

    p = jnp.exp2(s_buf[0, 0:ncp, :] - bm_ref[0])
    pb16 = p.astype(BF16)
    acc = jnp.concatenate([_dot(vct_ref[g], cols(pb16, g * rg, rg)) for g in range(ng)], axis=1)
    inv = jnp.where(tpos_n >= CMP_BLOCK - 1, 1.0 / jnp.maximum(acc[HEAD_DIM:HEAD_DIM + 1, :], 1e-30), 0.0)
    o_cmp = acc[0:HEAD_DIM, :] * inv

    pn = p * inv
    ps = jnp.concatenate([sum(cols(pn, g * rg + r) for r in range(1, rg)) + cols(pn, g * rg)
                          for g in range(ng)], axis=1)
    ps_hi = ps.astype(BF16)
    ps_lo = (ps - ps_hi.astype(F32)).astype(BF16)
    imp = _dot(ovt_ref[...], ps_hi) + _dot(ovt_ref[...], ps_lo)
    jb = lax.broadcasted_iota(jnp.int32, (imp.shape[0], 1), 0)
    cur = jnp.concatenate([tpos >> (SEL_BLOCK.bit_length() - 1)] * ng, axis=1)
    forced = (jb == 0) | (jb == cur) | (jb == cur - 1)
    imp = jnp.where(forced, jnp.inf, imp)
    imp = jnp.where(jb <= cur, imp, -jnp.inf)

    _park(parked, 0, jnp.where(i >= 1, _dot(kw_ref[i1], qw), NEG))
    _softmax_reset(m_ref, acc_ref)
    _softmax_update(m_ref, acc_ref, parked, 1, lambda pb: per_group(vtw_ref, i, pb))
    _park(parked, 1, jnp.where(jnp.logical_or(causal, i < 2), NEG, _dot(kw_ref[i2], qw)))
    _softmax_update(m_ref, acc_ref, parked, 0, lambda pb: per_group(vtw_ref, i1, pb))
    _softmax_update(m_ref, acc_ref, parked, 1, lambda pb: per_group(vtw_ref, i2, pb))
    o_win = _finish(acc_ref[...])

    quarter = n_sel_blocks // 4
    last_blk = (tq // SEL_BLOCK) * (i + 1) - 1
    rank_ref[...] = _rank_rows(imp, jnp.zeros(imp.shape, F32), 0, quarter)
    for c in range(1, 4):
        @pl.when(last_blk >= c * quarter)
        def _():
            rank_ref[...] = _rank_rows(imp, rank_ref[...], c * quarter, (c + 1) * quarter)
    rank = rank_ref[...]
    bias = jnp.where(rank < min(SEL_TOPN, n_sel_blocks), 0.0, NEG).astype(BF16)

    off = jnp.full((bias.shape[0], tq), NEG, BF16)
    aug = lambda b_rows: jnp.concatenate(
        [jnp.concatenate([cols(qw, h), b_rows(h // rg), zero], axis=0) for h in range(nh)], axis=1)
    qa = aug(lambda g: cols(bias, g))
    qa_off = aug(lambda g: off)
    trip, last_blocks = _online_blocks(
        i, parked, m_ref, acc_ref,
        lambda kb, live: _dot(ke_ref[kb], qa if live is True else jnp.where(live, qa, qa_off)),
        lambda kb, pb: per_group(vts_ref, kb, pb),
        causal)

    def finish():
        o_sel = _finish(last_blocks())
        for h in range(nh):
            gates = gn_ref[h // rg]
            r = h % rg
            y_ref[h * HEAD_DIM:(h + 1) * HEAD_DIM, :] = (
                gates[3 * r:3 * r + 1, :] * cols(o_cmp, h)
                + gates[3 * r + 1:3 * r + 2, :] * cols(o_sel, h)
                + gates[3 * r + 2:3 * r + 3, :] * cols(o_win, h)).astype(y_ref.dtype)

    return trip, finish


def _moba_body(q_ref, qlo_ref, km_ref, ke_ref, vt_ref, y_ref, s_buf, bm_ref, m_ref, acc_ref, *, n_blocks):
    tq = q_ref.shape[1]
    nh = km_ref.shape[0]
    i = pl.program_id(1)
    qb = q_ref[...]
    head = lambda a, h: a[h * HEAD_DIM:(h + 1) * HEAD_DIM, :]
    zero = jnp.zeros((HEAD_DIM, tq), BF16)
    n_bias = km_ref.shape[1]
    pad = jnp.zeros((K_AUG - LANES - n_bias, tq), BF16)

    def aug(h, b_rows):
        pair = [head(qb, h), zero] if h % 2 == 0 else [zero, head(qb, h)]
        return jnp.concatenate(pair + [b_rows, pad], axis=0)

    def keys(kb, h):
        return ke_ref[kb, :, (h // 2) * K_AUG:(h // 2 + 1) * K_AUG]

    def diag_scores():
        open_rows = jnp.zeros((n_bias, tq), BF16)
        return jnp.concatenate([_dot(keys(i, h), aug(h, open_rows)) for h in range(nh)], axis=1)

    def gate():
        q_lo = qlo_ref[...]
        gates = []
        for h in range(nh):
            km = km_ref[h]
            km_hi = km.astype(BF16)
            km_lo = (km - km_hi.astype(F32)).astype(BF16)
            gates.append(_dot(km_hi, head(qb, h)) + (_dot(km_hi, head(q_lo, h)) + _dot(km_lo, head(qb, h))))
        return jnp.concatenate(gates, axis=1)

    def select(gate):
        jb = lax.broadcasted_iota(jnp.int32, (gate.shape[0], 1), 0)
        gate = jnp.where(jb < i, gate, -jnp.inf)
        sel = _topk_mask(gate, n_blocks, min(MOBA_TOPK, n_blocks)) & (gate > -jnp.inf)
        bias = jnp.where(sel, 0.0, NEG).astype(BF16)
        off = jnp.full((n_bias, tq), NEG, BF16)
        qa = [aug(h, bias[:, h * tq:(h + 1) * tq]) for h in range(nh)]
        qa_off = [aug(h, off) for h in range(nh)]
        return lambda kb, live: jnp.concatenate(
            [_dot(keys(kb, h), jnp.where(live, qa[h], qa_off[h])) for h in range(nh)], axis=1)

    def values(kb, pb):
        return jnp.concatenate([_dot(vt_ref[h, kb], pb[:, h * tq:(h + 1) * tq]) for h in range(nh)], axis=1)

    causal = (lax.broadcasted_iota(jnp.int32, (KEY_BLOCK, 1), 0)
              <= (lax.broadcasted_iota(jnp.int32, (1, nh * tq), 1) & (tq - 1)))
    trip, last_blocks = _online_blocks_diag_first(i, (s_buf, bm_ref), m_ref, acc_ref,
                                                  gate, diag_scores, select, values, causal)

    def finish():
        o = _finish(last_blocks())
        for h in range(nh):
            y_ref[h * HEAD_DIM:(h + 1) * HEAD_DIM, :] = o[:, h * tq:(h + 1) * tq].astype(y_ref.dtype)

    return trip, finish


N_NSA_IN = 10
N_MOBA_IN = 5


def _mixers_body(*refs, n_sel_blocks, n_blocks):
    nsa_in = refs[:N_NSA_IN]
    moba_in = refs[N_NSA_IN:N_NSA_IN + N_MOBA_IN]
    y_nsa, y_moba = refs[N_NSA_IN + N_MOBA_IN:N_NSA_IN + N_MOBA_IN + 2]
    scratch = refs[N_NSA_IN + N_MOBA_IN + 2:]
    nsa_trip, nsa_finish = _nsa_body(*nsa_in, y_nsa, *scratch[:5], n_sel_blocks=n_sel_blocks)
    moba_trip, moba_finish = _moba_body(*moba_in, y_moba, *scratch[5:], n_blocks=n_blocks)
    i = pl.program_id(1)

    @pl.loop(0, (i + 2) // 2 - 1)
    def _(p):
        nsa_trip(p)
        moba_trip(p + 1)

    nsa_finish()
    moba_finish()


def _mixers_call(qt, gnt, kc, vct, ke, vts, kw, vtw, ovt, qmt, kmean, kem, vtm):
    b, _, seq = qt.shape
    g = NSA_KV_GROUPS
    nkb = seq // KEY_BLOCK
    ncp = kc.shape[1]
    tq = Q_TILE
    n_cols = NSA_HEADS * tq
    tile = lambda bi, i: (bi, 0, i)
    per_batch = lambda shape: pl.BlockSpec((None,) + shape, lambda bi, i: (bi,) + (0,) * len(shape))
    return pl.pallas_call(
        functools.partial(_mixers_body, n_sel_blocks=seq // SEL_BLOCK, n_blocks=nkb),
        grid=(b, seq // tq),
        in_specs=[pl.BlockSpec((None, NSA_WIDTH, tq), tile),
                  pl.BlockSpec((None, NSA_WIDTH, tq), lambda bi, i: (bi, 1, i)),
                  pl.BlockSpec((None, g, GATE_ROWS, tq), lambda bi, i: (bi, 0, 0, i)),
                  per_batch((ncp, LANES)),
                  per_batch((g, V_ROWS, ncp)),
                  per_batch((nkb, KEY_BLOCK, K_AUG)),
                  per_batch((g, nkb, V_ROWS, KEY_BLOCK)),
                  per_batch((nkb, KEY_BLOCK, LANES)),
                  per_batch((g, nkb, V_ROWS, KEY_BLOCK)),
                  pl.BlockSpec((HEAD_DIM, ncp), lambda bi, i: (0, 0)),
                  pl.BlockSpec((None, MOBA_WIDTH, tq), tile),
                  pl.BlockSpec((None, MOBA_WIDTH, tq), lambda bi, i: (bi, 1, i)),
                  per_batch((MOBA_HEADS, 16, HEAD_DIM)),
                  per_batch((nkb, KEY_BLOCK, (MOBA_HEADS // 2) * K_AUG)),
                  per_batch((MOBA_HEADS, nkb, V_ROWS, KEY_BLOCK))],
        out_specs=[pl.BlockSpec((None, NSA_WIDTH, tq), tile), pl.BlockSpec((None, MOBA_WIDTH, tq), tile)],
        out_shape=[jax.ShapeDtypeStruct((b, NSA_WIDTH, seq), BF16),
                   jax.ShapeDtypeStruct((b, MOBA_WIDTH, seq), BF16)],
        scratch_shapes=[pltpu.VMEM((2, KEY_BLOCK, n_cols), F32),
                        pltpu.VMEM((2, 1, n_cols), F32),
                        pltpu.VMEM((1, n_cols), F32),
                        pltpu.VMEM((V_ROWS, n_cols), F32),
                        pltpu.VMEM((HEAD_DIM, g * tq), F32),
                        pltpu.VMEM((2, KEY_BLOCK, n_cols), F32),
                        pltpu.VMEM((2, 1, n_cols), F32),
                        pltpu.VMEM((1, n_cols), F32),
                        pltpu.VMEM((V_ROWS, n_cols), F32)],
        compiler_params=pltpu.CompilerParams(
            dimension_semantics=("parallel", "arbitrary"), vmem_limit_bytes=VMEM_LIMIT),
        name="mixers",
    )(qt, qt, gnt, kc, vct, ke, vts, kw, vtw, ovt, qmt, qmt, kmean, kem, vtm)


def _merge_ffn_body(x_ref, yn_ref, ym_ref, gm_ref, wun_ref, wum_ref, wo_ref, g2_ref,
                    w1_ref, w2_ref, gf_ref, o_ref):
    tn = (((0,), (0,)), ((), ()))
    a = lax.dot_general(yn_ref[...], wun_ref[...], tn, preferred_element_type=F32)
    c = lax.dot_general(ym_ref[...], wum_ref[...], tn, preferred_element_type=F32)
    mixed = gm_ref[:, 0:D_MODEL] * a + gm_ref[:, D_MODEL:2 * D_MODEL] * c
    x1 = x_ref[...] + _dot(mixed.astype(BF16), wo_ref[...])
    ms = jnp.mean(x1 * x1, axis=-1, keepdims=True)
    h2 = (x1 * lax.rsqrt(ms + NORM_EPS) * g2_ref[...]).astype(BF16)
    x2 = x1
    for k in range(D_FF // FF_CHUNK):
        u = jnp.maximum(_dot(h2, w1_ref[:, k * FF_CHUNK:(k + 1) * FF_CHUNK]), 0.0)
        x2 = x2 + _dot((u * u).astype(BF16), w2_ref[k * FF_CHUNK:(k + 1) * FF_CHUNK, :])
    ms = jnp.mean(x2 * x2, axis=-1, keepdims=True)
    o_ref[...] = x2 * lax.rsqrt(ms + NORM_EPS) * gf_ref[...]


def _merge_ffn(x2, ynt, ymt, gm, wun, wum, wo, g2, w1, w2, gf):
    m = x2.shape[0]
    seq = ynt.shape[2]
    tm = 512
    nt = seq // tm
    row = lambda i: (i, 0)
    const = lambda i: (0, 0)
    feat = lambda i: (i // nt, 0, i % nt)
    resident = lambda shape: pl.BlockSpec(shape, const, pipeline_mode=pl.Buffered(1))
    return pl.pallas_call(
        _merge_ffn_body,
        grid=(m // tm,),
        in_specs=[pl.BlockSpec((tm, D_MODEL), row),
                  pl.BlockSpec((None, NSA_WIDTH, tm), feat),
                  pl.BlockSpec((None, MOBA_WIDTH, tm), feat),
                  pl.BlockSpec((tm, 2 * D_MODEL), row),
                  resident((NSA_WIDTH, D_MODEL)),
                  resident((MOBA_WIDTH, D_MODEL)),
                  resident((D_MODEL, D_MODEL)),
                  pl.BlockSpec((1, D_MODEL), const),
                  resident((D_MODEL, D_FF)),
                  resident((D_FF, D_MODEL)),
                  pl.BlockSpec((1, D_MODEL), const)],
        out_specs=pl.BlockSpec((tm, D_MODEL), row),
        out_shape=jax.ShapeDtypeStruct((m, D_MODEL), F32),
        compiler_params=pltpu.CompilerParams(dimension_semantics=("parallel",),
                                             vmem_limit_bytes=VMEM_LIMIT),
        name="merge_ffn",
    )(x2, ynt, ymt, gm, wun, wum, wo, g2, w1, w2, gf)


def _rope_tables(seq):
    half = ROT_DIM // 2
    inv = ROPE_THETA ** (-jnp.arange(0, ROT_DIM, 2, dtype=F32) / ROT_DIM)
    ang = jnp.arange(seq, dtype=F32)[:, None] * inv[None, :]
    cos, sin = jnp.cos(ang), jnp.sin(ang)
    pad = HEAD_DIM - ROT_DIM
    one_head = lambda a, b, fill: jnp.concatenate([a, b, jnp.full((seq, pad), fill, F32)], axis=1)
    zeros = jnp.zeros((seq, half), F32)
    two = lambda a: jnp.concatenate([a, a], axis=1)
    return (two(one_head(cos, cos, 1.0)),
            two(one_head(-sin, zeros, 0.0)),
            two(one_head(zeros, sin, 0.0)),
            cos.T, sin.T)


def _overlap_t(ncp, nsb):
    i = np.arange(ncp)[None, :]
    j = np.arange(HEAD_DIM)[:, None]
    start = i * CMP_STRIDE
    end = start + CMP_BLOCK - 1
    ov = (end >= j * SEL_BLOCK) & (start <= j * SEL_BLOCK + SEL_BLOCK - 1) & (i < ncp - 1) & (j < nsb)
    return jnp.asarray(ov.astype(np.float32), dtype=BF16)


def _block_diag(w):
    z = jnp.zeros_like(w)
    return jnp.concatenate([jnp.concatenate([w, z], axis=-1), jnp.concatenate([z, w], axis=-1)], axis=-2)


def _projection_weights(w):
    kv = lambda j: w[:, OFF_KV + j * KV_WIDTH:OFF_KV + (j + 1) * KV_WIDTH]
    mo = lambda j: w[:, OFF_M + j * MOBA_WIDTH:OFF_M + (j + 1) * MOBA_WIDTH]
    wa = jnp.concatenate([kv(0), kv(1), kv(2), kv(4), mo(1), w[:, OFF_GM:]], axis=1)
    gn = w[:, OFF_GN:OFF_M].T.reshape(NSA_KV_GROUPS, 3 * NSA_GROUP_SIZE, D_MODEL)
    gn = jnp.concatenate([gn, jnp.zeros((NSA_KV_GROUPS, GATE_ROWS - 3 * NSA_GROUP_SIZE, D_MODEL), w.dtype)], axis=1)
    wb = jnp.concatenate([w[:, :OFF_KV].T, mo(0).T, kv(3).T, kv(5).T, mo(2).T,
                          gn.reshape(NSA_KV_GROUPS * GATE_ROWS, D_MODEL)], axis=0)
    return wa.astype(BF16), wb.astype(BF16)


def _mixers(x, norm1_g, w_in, cmp_pe_k, cmp_pe_v, cmp_k_w1, cmp_k_w2, cmp_v_w1, cmp_v_w2):
    b, seq, _ = x.shape
    g, d = NSA_KV_GROUPS, HEAD_DIM
    nkb = seq // KEY_BLOCK
    ncp = seq // CMP_STRIDE
    nsb = seq // SEL_BLOCK
    layer = 0

    wa, wb = _projection_weights(w_in[layer])
    x2 = x.reshape(b * seq, D_MODEL)
    (qt, qmt, gnt, kcx, vcx, ke, kw, vts, vtw, kem, vtm, kmean8, ogm) = _in_proj(
        x2, norm1_g[layer][None, :], wa, wb, _rope_tables(seq), b, seq)

    per_group = lambda a: jnp.concatenate([a] * g, axis=-1)
    pe_rows = lambda pe: jnp.broadcast_to(per_group(pe)[:, None, :], (CMP_BLOCK, SUBLANES, LANES)).astype(BF16)
    w1_blocks = lambda w1: _block_diag(w1.reshape(CMP_BLOCK, d, d)).astype(BF16)
    kc, vct = _compress(
        kcx.reshape(b, seq, KV_WIDTH), vcx.reshape(b, seq, KV_WIDTH),
        pe_rows(cmp_pe_k[layer]), pe_rows(cmp_pe_v[layer]),
        w1_blocks(cmp_k_w1[layer]), w1_blocks(cmp_v_w1[layer]),
        _block_diag(cmp_k_w2[layer]).astype(BF16), _block_diag(cmp_v_w2[layer].T).astype(BF16))
    kmean = kmean8[:, ::SUBLANES, :].reshape(b, nkb, MOBA_HEADS, d)
    kmean = jnp.swapaxes(kmean, 1, 2)
    kmean = jnp.concatenate([kmean, jnp.zeros((b, MOBA_HEADS, 16 - nkb, d), F32)], axis=2)
    ynt, ymt = _mixers_call(qt, gnt, kc, vct, ke, vts, kw, vtw, _overlap_t(ncp, nsb),
                            qmt, kmean, kem, vtm)
    return x2, ynt, ymt, ogm


def kernel(x, norm1_g, w_in, cmp_pe_k, cmp_pe_v, cmp_k_w1, cmp_k_w2, cmp_v_w1, cmp_v_w2,
           w_up_nsa, w_up_moba, w_out, norm2_g, w_ff1, w_ff2, norm_f_g):
    b, seq, _ = x.shape
    layer = 0
    x2, ynt, ymt, ogm = _mixers(x, norm1_g, w_in, cmp_pe_k, cmp_pe_v, cmp_k_w1, cmp_k_w2, cmp_v_w1, cmp_v_w2)
    out = _merge_ffn(x2, ynt, ymt, ogm, w_up_nsa[layer].astype(BF16), w_up_moba[layer].astype(BF16),
                     w_out[layer].astype(BF16), norm2_g[layer][None, :],
                     w_ff1[layer].astype(BF16), w_ff2[layer].astype(BF16), norm_f_g[None, :])
    return out.reshape(b, seq, D_MODEL)
```

```python
import functools

import numpy as np
import jax
import jax.numpy as jnp
from jax import lax
from jax.experimental import pallas as pl
from jax.experimental.pallas import tpu as pltpu

F32 = jnp.float32
BF16 = jnp.bfloat16

D_MODEL = 1024
HEAD_DIM = 64
ROT_DIM = HEAD_DIM // 4
ROPE_THETA = 500000.0
NORM_EPS = 1e-6

NSA_HEADS = 8
NSA_KV_GROUPS = 2
NSA_GROUP_SIZE = NSA_HEADS // NSA_KV_GROUPS
CMP_BLOCK = 32
CMP_STRIDE = 16
SEL_BLOCK = 64
SEL_TOPN = 16
WINDOW = 512

MOBA_HEADS = 8
MOBA_BLOCK = 256
MOBA_TOPK = 3

D_FF = 4 * D_MODEL
NSA_WIDTH = NSA_HEADS * HEAD_DIM
KV_WIDTH = NSA_KV_GROUPS * HEAD_DIM
MOBA_WIDTH = MOBA_HEADS * HEAD_DIM
OFF_KV = NSA_WIDTH
OFF_GN = OFF_KV + 6 * KV_WIDTH
OFF_M = OFF_GN + 3 * NSA_HEADS
OFF_GM = OFF_M + 3 * MOBA_WIDTH
IN_WIDTH = OFF_GM + 2 * D_MODEL

LANES = 128
SUBLANES = 8
KEY_BLOCK = 256
Q_TILE = 256
V_ROWS = 80
K_AUG = 2 * LANES
NEG = -1e30
SCALE = float(HEAD_DIM ** -0.5 * np.log2(np.e))
GATE_ROWS = 16
FF_CHUNK = 1024
VMEM_LIMIT = 56 * 1024 * 1024

A_KC, A_KSW, A_KM, A_GM = 0, 256, 512, 1024
A_WIDTH = A_GM + 2 * D_MODEL
B_QN, B_QM, B_V, B_VM, B_GN = 0, 512, 1024, 1280, 1792
B_ROWS = B_GN + NSA_KV_GROUPS * GATE_ROWS


def _dot(a, b):
    return jnp.dot(a, b, preferred_element_type=F32)


def _dot_nt(a, b):
    return lax.dot_general(a, b, (((1,), (1,)), ((), ())), preferred_element_type=F32)


def _sigmoid(y):
    return 1.0 / (1.0 + jnp.exp(-y))


def _in_proj_body(x_ref, g_ref, wa_ref, wb_ref, c_ref, s1_ref, s2_ref, ct_ref, st_ref,
                  qt_ref, qmt_ref, gnt_ref, kcx_ref, vcx_ref, ke_ref, kw_ref, vts_ref, vtw_ref,
                  kem_ref, vtm_ref, kmean_ref, gm_ref, h_ref, *, tiles_per_seq):
    tm = x_ref.shape[0]
    it = pl.program_id(0) % tiles_per_seq
    x = x_ref[...]
    ms = jnp.mean(x * x, axis=-1, keepdims=True)
    h_ref[...] = (x * lax.rsqrt(ms + NORM_EPS) * g_ref[...]).astype(BF16)

    cos = c_ref[...]
    sin_hi = s1_ref[...]
    sin_lo = s2_ref[...]

    def proj(c0, width):
        return _dot(h_ref[...], wa_ref[:, c0:c0 + width])

    def rope(yc):
        return (yc * cos + pltpu.roll(yc, LANES - ROT_DIM // 2, 1) * sin_hi
                + pltpu.roll(yc, ROT_DIM // 2, 1) * sin_lo)

    y = proj(A_KC, 2 * KV_WIDTH)
    kcx_ref[...] = y[:, 0:KV_WIDTH]
    vcx_ref[...] = y[:, KV_WIDTH:2 * KV_WIDTH]

    row = lax.broadcasted_iota(jnp.int32, (tm, 1), 0)
    lane = lax.broadcasted_iota(jnp.int32, (1, LANES), 1)
    y = proj(A_KSW, 2 * KV_WIDTH)
    ke_ref[:, 0:LANES] = rope(y[:, 0:LANES]).astype(BF16)
    sel_blk = (it * tm + row) >> (SEL_BLOCK.bit_length() - 1)
    ke_ref[:, LANES:K_AUG] = jnp.where(sel_blk == lane, 1.0, 0.0).astype(BF16)
    kw_ref[...] = rope(y[:, LANES:2 * LANES]).astype(BF16)

    y = proj(A_KM, MOBA_WIDTH)
    own_blk = jnp.where(lane == it, 1.0, 0.0).astype(BF16)
    sums = []
    for j in range(MOBA_WIDTH // LANES):
        kr = rope(y[:, j * LANES:(j + 1) * LANES])
        sums.append(jnp.mean(kr, axis=0, keepdims=True))
        kem_ref[:, j * K_AUG:j * K_AUG + LANES] = kr.astype(BF16)
        kem_ref[:, j * K_AUG + LANES:(j + 1) * K_AUG] = jnp.broadcast_to(own_blk, (tm, LANES))
    kmean_ref[...] = jnp.broadcast_to(jnp.concatenate(sums, axis=1), kmean_ref.shape)

    for part in range(4):
        w = 2 * D_MODEL // 4
        gm_ref[:, part * w:(part + 1) * w] = _sigmoid(proj(A_GM + part * w, w)).astype(BF16)

    cos_t = ct_ref[...]
    sin_t = st_ref[...]

    def proj_t(r0, rows):
        return _dot_nt(wb_ref[r0:r0 + rows, :], h_ref[...])

    def rope_t(yt):
        half = ROT_DIM // 2
        out = []
        for hd in range(yt.shape[0] // HEAD_DIM):
            a = yt[hd * HEAD_DIM:hd * HEAD_DIM + half, :]
            b = yt[hd * HEAD_DIM + half:hd * HEAD_DIM + ROT_DIM, :]
            out += [a * cos_t - b * sin_t, a * sin_t + b * cos_t,
                    yt[hd * HEAD_DIM + ROT_DIM:(hd + 1) * HEAD_DIM, :]]
        return jnp.concatenate(out, axis=0)

    y = proj_t(B_QN, NSA_WIDTH)
    qt_ref[0:NSA_WIDTH, :] = (y * SCALE).astype(BF16)
    qt_ref[NSA_WIDTH:2 * NSA_WIDTH, :] = (rope_t(y) * SCALE).astype(BF16)
    qm = rope_t(proj_t(B_QM, MOBA_WIDTH)) * SCALE
    qm_hi = qm.astype(BF16)
    qmt_ref[0:MOBA_WIDTH, :] = qm_hi
    qmt_ref[MOBA_WIDTH:2 * MOBA_WIDTH, :] = (qm - qm_hi.astype(F32)).astype(BF16)

    ones_row = jnp.where(lax.broadcasted_iota(jnp.int32, (V_ROWS - HEAD_DIM, tm), 0) == 0, 1.0, 0.0).astype(BF16)
    y = proj_t(B_V, 4 * HEAD_DIM)
    for g in range(NSA_KV_GROUPS):
        vts_ref[g, 0:HEAD_DIM, :] = y[g * HEAD_DIM:(g + 1) * HEAD_DIM, :].astype(BF16)
        vts_ref[g, HEAD_DIM:V_ROWS, :] = ones_row
        vtw_ref[g, 0:HEAD_DIM, :] = y[(2 + g) * HEAD_DIM:(3 + g) * HEAD_DIM, :].astype(BF16)
        vtw_ref[g, HEAD_DIM:V_ROWS, :] = ones_row
    y = proj_t(B_VM, MOBA_WIDTH)
    for hd in range(MOBA_HEADS):
        vtm_ref[hd, 0:HEAD_DIM, :] = y[hd * HEAD_DIM:(hd + 1) * HEAD_DIM, :].astype(BF16)
        vtm_ref[hd, HEAD_DIM:V_ROWS, :] = ones_row
    y = _sigmoid(proj_t(B_GN, NSA_KV_GROUPS * GATE_ROWS))
    for g in range(NSA_KV_GROUPS):
        gnt_ref[g] = y[g * GATE_ROWS:(g + 1) * GATE_ROWS, :]


def _in_proj(x2, g1, wa, wb, tabs, batch, seq):
    m = x2.shape[0]
    tm = KEY_BLOCK
    nt = seq // tm
    row = lambda i: (i, 0)
    const = lambda i: (0, 0)
    tab = lambda i: (i % nt, 0)
    tab_t = lambda i: (0, i % nt)
    feat = lambda i: (i // nt, 0, i % nt)
    blk4 = lambda i: (i // nt, i % nt, 0, 0)
    blk5 = lambda i: (i // nt, 0, i % nt, 0, 0)
    g = NSA_KV_GROUPS
    outs = [
        (jax.ShapeDtypeStruct((batch, 2 * NSA_WIDTH, seq), BF16), pl.BlockSpec((None, 2 * NSA_WIDTH, tm), feat)),
        (jax.ShapeDtypeStruct((batch, 2 * MOBA_WIDTH, seq), BF16), pl.BlockSpec((None, 2 * MOBA_WIDTH, tm), feat)),
        (jax.ShapeDtypeStruct((batch, g, GATE_ROWS, seq), F32),
         pl.BlockSpec((None, g, GATE_ROWS, tm), lambda i: (i // nt, 0, 0, i % nt))),
        (jax.ShapeDtypeStruct((m, KV_WIDTH), F32), pl.BlockSpec((tm, KV_WIDTH), row)),
        (jax.ShapeDtypeStruct((m, KV_WIDTH), F32), pl.BlockSpec((tm, KV_WIDTH), row)),
        (jax.ShapeDtypeStruct((batch, nt, tm, K_AUG), BF16), pl.BlockSpec((None, None, tm, K_AUG), blk4)),
        (jax.ShapeDtypeStruct((batch, nt, tm, LANES), BF16), pl.BlockSpec((None, None, tm, LANES), blk4)),
        (jax.ShapeDtypeStruct((batch, g, nt, V_ROWS, tm), BF16), pl.BlockSpec((None, g, None, V_ROWS, tm), blk5)),
        (jax.ShapeDtypeStruct((batch, g, nt, V_ROWS, tm), BF16), pl.BlockSpec((None, g, None, V_ROWS, tm), blk5)),
        (jax.ShapeDtypeStruct((batch, nt, tm, (MOBA_HEADS // 2) * K_AUG), BF16),
         pl.BlockSpec((None, None, tm, (MOBA_HEADS // 2) * K_AUG), blk4)),
        (jax.ShapeDtypeStruct((batch, MOBA_HEADS, nt, V_ROWS, tm), BF16),
         pl.BlockSpec((None, MOBA_HEADS, None, V_ROWS, tm), blk5)),
        (jax.ShapeDtypeStruct((batch, nt * SUBLANES, MOBA_WIDTH), F32),
         pl.BlockSpec((None, SUBLANES, MOBA_WIDTH), lambda i: (i // nt, i % nt, 0))),
        (jax.ShapeDtypeStruct((m, 2 * D_MODEL), BF16), pl.BlockSpec((tm, 2 * D_MODEL), row)),
    ]
    return pl.pallas_call(
        functools.partial(_in_proj_body, tiles_per_seq=nt),
        grid=(m // tm,),
        in_specs=[pl.BlockSpec((tm, D_MODEL), row),
                  pl.BlockSpec((1, D_MODEL), const),
                  pl.BlockSpec((D_MODEL, A_WIDTH), const),
                  pl.BlockSpec((B_ROWS, D_MODEL), const),
                  pl.BlockSpec((tm, LANES), tab),
                  pl.BlockSpec((tm, LANES), tab),
                  pl.BlockSpec((tm, LANES), tab),
                  pl.BlockSpec((SUBLANES, tm), tab_t),
                  pl.BlockSpec((SUBLANES, tm), tab_t)],
        out_specs=[o[1] for o in outs],
        out_shape=[o[0] for o in outs],
        scratch_shapes=[pltpu.VMEM((tm, D_MODEL), BF16)],
        compiler_params=pltpu.CompilerParams(dimension_semantics=("parallel",),
                                             vmem_limit_bytes=VMEM_LIMIT),
        name="in_proj",
    )(x2, g1, wa, wb, *tabs)


def _gelu_tanh(x):
    return 0.5 * x * (1.0 + jnp.tanh(np.sqrt(2.0 / np.pi) * (x + 0.044715 * (x * x * x))))


def _compress_body(kx_ref, vx_ref, pek_ref, pev_ref, kw1_ref, vw1_ref, kw2_ref, vw2t_ref,
                   kc_ref, vct_ref):
    ncp = kc_ref.shape[0]

    def hidden(x_ref, pe_ref, w1_ref):
        ya = jnp.zeros((ncp, LANES), F32)
        yb = jnp.zeros((ncp, LANES), F32)
        pe = jnp.zeros((SUBLANES, LANES), F32)
        for r in range(CMP_STRIDE):
            xr = x_ref[pl.ds(r, ncp, stride=CMP_STRIDE), :].astype(BF16)
            ya = ya + _dot(xr, w1_ref[r])
            yb = yb + _dot(xr, w1_ref[CMP_STRIDE + r])
        for l in range(CMP_BLOCK):
            pe = pe + _dot(pe_ref[l], w1_ref[l])
        return _gelu_tanh(ya + pltpu.roll(yb, ncp - 1, 0) + pe[0:1, :])

    hk = hidden(kx_ref, pek_ref, kw1_ref)
    kc_ref[...] = _dot(hk.astype(BF16), kw2_ref[...]).astype(BF16)
    hv = hidden(vx_ref, pev_ref, vw1_ref)
    vct = _dot_nt(vw2t_ref[...], hv.astype(BF16)).astype(BF16)
    ones_row = jnp.where(lax.broadcasted_iota(jnp.int32, (V_ROWS - HEAD_DIM, ncp), 0) == 0, 1.0, 0.0).astype(BF16)
    for g in range(NSA_KV_GROUPS):
        vct_ref[g, 0:HEAD_DIM, :] = vct[g * HEAD_DIM:(g + 1) * HEAD_DIM, :]
        vct_ref[g, HEAD_DIM:V_ROWS, :] = ones_row


def _compress(kx, vx, pek, pev, kw1, vw1, kw2, vw2t):
    b, seq, _ = kx.shape
    ncp = seq // CMP_STRIDE
    g = NSA_KV_GROUPS
    tok = lambda bi: (bi, 0, 0)
    const3 = lambda bi: (0, 0, 0)
    const2 = lambda bi: (0, 0)
    return pl.pallas_call(
        _compress_body,
        grid=(b,),
        in_specs=[pl.BlockSpec((None, seq, LANES), tok),
                  pl.BlockSpec((None, seq, LANES), tok),
                  pl.BlockSpec((CMP_BLOCK, SUBLANES, LANES), const3),
                  pl.BlockSpec((CMP_BLOCK, SUBLANES, LANES), const3),
                  pl.BlockSpec((CMP_BLOCK, LANES, LANES), const3),
                  pl.BlockSpec((CMP_BLOCK, LANES, LANES), const3),
                  pl.BlockSpec((LANES, LANES), const2),
                  pl.BlockSpec((LANES, LANES), const2)],
        out_specs=[pl.BlockSpec((None, ncp, LANES), tok),
                   pl.BlockSpec((None, g, V_ROWS, ncp), lambda bi: (bi, 0, 0, 0))],
        out_shape=[jax.ShapeDtypeStruct((b, ncp, LANES), BF16),
                   jax.ShapeDtypeStruct((b, g, V_ROWS, ncp), BF16)],
        compiler_params=pltpu.CompilerParams(dimension_semantics=("parallel",)),
        name="compress",
    )(kx, vx, pek, pev, kw1, vw1, kw2, vw2t)


def _rank_rows(score, rank, k0, k1):
    rows = score.shape[0]
    j = lax.broadcasted_iota(jnp.int32, (SUBLANES, 1), 0)
    blocks = [score[b:b + SUBLANES, :] for b in range(0, rows, SUBLANES)]
    for k in range(k0, k1):
        rk = score[k:k + 1, :]
        before = lambda sb: jnp.where(rk > sb, 1.0, 0.0)
        unless = lambda sb: jnp.where(sb > rk, 0.0, 1.0)
        counts = []
        for b, sb in zip(range(0, rows, SUBLANES), blocks):
            if b + SUBLANES <= k:
                counts.append(before(sb))
            elif b > k:
                counts.append(unless(sb))
            else:
                counts.append(jnp.where(j + b > k, unless(sb), before(sb)))
        rank = rank + jnp.concatenate(counts, axis=0)
    return rank


def _topk_mask(score, n_rows, k_top):
    return _rank_rows(score, jnp.zeros(score.shape, F32), 0, n_rows) < k_top


def _finish(acc):
    return acc[0:HEAD_DIM, :] / acc[HEAD_DIM:HEAD_DIM + 1, :]


def _softmax_reset(m_ref, acc_ref):
    m_ref[...] = jnp.full(m_ref.shape, NEG, F32)
    acc_ref[...] = jnp.zeros(acc_ref.shape, F32)


def _park(parked, slot, sb):
    s_buf, bm_ref = parked
    s_buf[slot, 0:sb.shape[0], :] = sb
    bm_ref[slot] = jnp.max(sb, axis=0, keepdims=True)


def _softmax_update(m_ref, acc_ref, parked, slot, values):
    s_buf, bm_ref = parked
    m_old = m_ref[...]
    m_new = jnp.maximum(m_old, bm_ref[slot])
    pb = jnp.exp2(s_buf[slot] - m_new).astype(BF16)
    acc_ref[...] = jnp.exp2(m_old - m_new) * acc_ref[...] + values(pb)
    m_ref[...] = m_new


def _online_blocks(i, parked, m_ref, acc_ref, scores, values, causal):
    block = lambda kb: jnp.maximum(kb, 0)

    def park(slot, kb, live, mask=None):
        sb = scores(block(kb), live)
        _park(parked, slot, sb if mask is None else jnp.where(mask, sb, NEG))

    def update(slot, kb):
        _softmax_update(m_ref, acc_ref, parked, slot, lambda pb: values(block(kb), pb))

    _softmax_reset(m_ref, acc_ref)
    first = -((i + 1) & 1)
    n_pairs = (i + 2) // 2
    park(0, first, first >= 0)

    def trip(p):
        ka = first + 2 * p
        park(1, ka + 1, True)
        update(0, ka)
        park(0, ka + 2, True)
        update(1, ka + 1)

    def finish():
        park(1, i, True, causal)
        update(0, i - 1)
        update(1, i)
        return acc_ref[...]

    return trip, finish


def _online_blocks_diag_first(i, parked, m_ref, acc_ref, gate, diag_scores, select, values, causal):
    past = lambda x: jnp.clip(x - 1, 0, jnp.maximum(i - 1, 0))

    def park(slot, x):
        _park(parked, slot, scores(past(x), x - 1 < i))

    def update(slot, kb):
        _softmax_update(m_ref, acc_ref, parked, slot, lambda pb: values(kb, pb))

    _softmax_reset(m_ref, acc_ref)
    gates = gate()
    _park(parked, 0, jnp.where(causal, diag_scores(), NEG))
    scores = select(gates)
    park(1, 1)
    update(0, i)
    park(0, 2)
    update(1, past(1))

    def trip(p):
        park(1, 2 * p + 1)
        update(0, past(2 * p))
        park(0, 2 * p + 2)
        update(1, past(2 * p + 1))

    return trip, lambda: acc_ref[...]


def _nsa_body(q_ref, qr_ref, gn_ref, kc_ref, vct_ref, ke_ref, vts_ref, kw_ref, vtw_ref, ovt_ref,
              y_ref, s_buf, bm_ref, m_ref, acc_ref, rank_ref, *, n_sel_blocks):
    tq = q_ref.shape[1]
    ng, rg = NSA_KV_GROUPS, NSA_GROUP_SIZE
    nh = ng * rg
    i = pl.program_id(1)
    t0 = i * tq
    head = lambda a, h: a[h * HEAD_DIM:(h + 1) * HEAD_DIM, :]
    cols = lambda a, h, n=1: a[:, h * tq:(h + n) * tq]
    q = q_ref[...]
    qr = qr_ref[...]
    tpos = t0 + lax.broadcasted_iota(jnp.int32, (1, tq), 1)
    tpos_n = jnp.concatenate([tpos] * nh, axis=1)

    zero = jnp.zeros((HEAD_DIM, tq), BF16)
    place = lambda h, a: jnp.concatenate([a, zero] if h < rg else [zero, a], axis=0)
    qc = jnp.concatenate([place(h, head(q, h)) for h in range(nh)], axis=1)
    qw = jnp.concatenate([place(h, head(qr, h)) for h in range(nh)], axis=1)
    per_group = lambda v_ref, kb, pb: jnp.concatenate(
        [_dot(v_ref[g, kb], cols(pb, g * rg, rg)) for g in range(ng)], axis=1)

    causal = (lax.broadcasted_iota(jnp.int32, (KEY_BLOCK, 1), 0)
              <= (lax.broadcasted_iota(jnp.int32, (1, nh * tq), 1) & (tq - 1)))

    ncp = kc_ref.shape[0]
    i1 = jnp.maximum(i - 1, 0)
    i2 = jnp.maximum(i - 2, 0)
    n = lax.broadcasted_iota(jnp.int32, (ncp, 1), 0)
    parked = (s_buf, bm_ref)
    _park(parked, 0, jnp.where((n * CMP_STRIDE + (CMP_BLOCK - 1)) <= tpos_n, _dot(kc_ref[...], qc), NEG))
    _park(parked, 1, jnp.where(causal, _dot(kw_ref[i], qw), NEG))

    p = jnp.exp2(s_buf[0, 0:ncp, :] - bm_ref[0])
    pb16 = p.astype(BF16)
    acc = jnp.concatenate([_dot(vct_ref[g], cols(pb16, g * rg, rg)) for g in range(ng)], axis=1)
    inv = jnp.where(tpos_n >= CMP_BLOCK - 1, 1.0 / jnp.maximum(acc[HEAD_DIM:HEAD_DIM + 1, :], 1e-30), 0.0)
    o_cmp = acc[0:HEAD_DIM, :] * inv

    pn = p * inv
    ps = jnp.concatenate([sum(cols(pn, g * rg + r) for r in range(1, rg)) + cols(pn, g * rg)
                          for g in range(ng)], axis=1)
    ps_hi = ps.astype(BF16)
    ps_lo = (ps - ps_hi.astype(F32)).astype(BF16)
    imp = _dot(ovt_ref[...], ps_hi) + _dot(ovt_ref[...], ps_lo)
    jb = lax.broadcasted_iota(jnp.int32, (imp.shape[0], 1), 0)
    cur = jnp.concatenate([tpos >> (SEL_BLOCK.bit_length() - 1)] * ng, axis=1)
    forced = (jb == 0) | (jb == cur) | (jb == cur - 1)
    imp = jnp.where(forced, jnp.inf, imp)
    imp = jnp.where(jb <= cur, imp, -jnp.inf)

    _park(parked, 0, jnp.where(i >= 1, _dot(kw_ref[i1], qw), NEG))
    _softmax_reset(m_ref, acc_ref)
    _softmax_update(m_ref, acc_ref, parked, 1, lambda pb: per_group(vtw_ref, i, pb))
    _park(parked, 1, jnp.where(jnp.logical_or(causal, i < 2), NEG, _dot(kw_ref[i2], qw)))
    _softmax_update(m_ref, acc_ref, parked, 0, lambda pb: per_group(vtw_ref, i1, pb))
    _softmax_update(m_ref, acc_ref, parked, 1, lambda pb: per_group(vtw_ref, i2, pb))
    o_win = _finish(acc_ref[...])

    quarter = n_sel_blocks // 4
    last_blk = (tq // SEL_BLOCK) * (i + 1) - 1
    rank_ref[...] = _rank_rows(imp, jnp.zeros(imp.shape, F32), 0, quarter)
    for c in range(1, 4):
        @pl.when(last_blk >= c * quarter)
        def _():
            rank_ref[...] = _rank_rows(imp, rank_ref[...], c * quarter, (c + 1) * quarter)
    rank = rank_ref[...]
    bias = jnp.where(rank < min(SEL_TOPN, n_sel_blocks), 0.0, NEG).astype(BF16)

    off = jnp.full((bias.shape[0], tq), NEG, BF16)
    aug = lambda b_rows: jnp.concatenate(
        [jnp.concatenate([cols(qw, h), b_rows(h // rg), zero], axis=0) for h in range(nh)], axis=1)
    qa = aug(lambda g: cols(bias, g))
    qa_off = aug(lambda g: off)
    trip, last_blocks = _online_blocks(
        i, parked, m_ref, acc_ref,
        lambda kb, live: _dot(ke_ref[kb], qa if live is True else jnp.where(live, qa, qa_off)),
        lambda kb, pb: per_group(vts_ref, kb, pb),
        causal)

    def finish():
        o_sel = _finish(last_blocks())
        for h in range(nh):
            gates = gn_ref[h // rg]
            r = h % rg
            y_ref[h * HEAD_DIM:(h + 1) * HEAD_DIM, :] = (
                gates[3 * r:3 * r + 1, :] * cols(o_cmp, h)
                + gates[3 * r + 1:3 * r + 2, :] * cols(o_sel, h)
                + gates[3 * r + 2:3 * r + 3, :] * cols(o_win, h)).astype(y_ref.dtype)

    return trip, finish


def _moba_body(q_ref, qlo_ref, km_ref, ke_ref, vt_ref, y_ref, s_buf, bm_ref, m_ref, acc_ref, *, n_blocks):
    tq = q_ref.shape[1]
    nh = km_ref.shape[0]
    i = pl.program_id(1)
    qb = q_ref[...]
    head = lambda a, h: a[h * HEAD_DIM:(h + 1) * HEAD_DIM, :]
    zero = jnp.zeros((HEAD_DIM, tq), BF16)
    n_bias = km_ref.shape[1]
    pad = jnp.zeros((K_AUG - LANES - n_bias, tq), BF16)

    def aug(h, b_rows):
        pair = [head(qb, h), zero] if h % 2 == 0 else [zero, head(qb, h)]
        return jnp.concatenate(pair + [b_rows, pad], axis=0)

    def keys(kb, h):
        return ke_ref[kb, :, (h // 2) * K_AUG:(h // 2 + 1) * K_AUG]

    def diag_scores():
        open_rows = jnp.zeros((n_bias, tq), BF16)
        return jnp.concatenate([_dot(keys(i, h), aug(h, open_rows)) for h in range(nh)], axis=1)

    def gate():
        q_lo = qlo_ref[...]
        gates = []
        for h in range(nh):
            km = km_ref[h]
            km_hi = km.astype(BF16)
            km_lo = (km - km_hi.astype(F32)).astype(BF16)
            gates.append(_dot(km_hi, head(qb, h)) + (_dot(km_hi, head(q_lo, h)) + _dot(km_lo, head(qb, h))))
        return jnp.concatenate(gates, axis=1)

    def select(gate):
        jb = lax.broadcasted_iota(jnp.int32, (gate.shape[0], 1), 0)
        gate = jnp.where(jb < i, gate, -jnp.inf)
        sel = _topk_mask(gate, n_blocks, min(MOBA_TOPK, n_blocks)) & (gate > -jnp.inf)
        bias = jnp.where(sel, 0.0, NEG).astype(BF16)
        off = jnp.full((n_bias, tq), NEG, BF16)
        qa = [aug(h, bias[:, h * tq:(h + 1) * tq]) for h in range(nh)]
        qa_off = [aug(h, off) for h in range(nh)]
        return lambda kb, live: jnp.concatenate(
            [_dot(keys(kb, h), jnp.where(live, qa[h], qa_off[h])) for h in range(nh)], axis=1)

    def values(kb, pb):
        return jnp.concatenate([_dot(vt_ref[h, kb], pb[:, h * tq:(h + 1) * tq]) for h in range(nh)], axis=1)

    causal = (lax.broadcasted_iota(jnp.int32, (KEY_BLOCK, 1), 0)
              <= (lax.broadcasted_iota(jnp.int32, (1, nh * tq), 1) & (tq - 1)))
    trip, last_blocks = _online_blocks_diag_first(i, (s_buf, bm_ref), m_ref, acc_ref,
                                                  gate, diag_scores, select, values, causal)

    def finish():
        o = _finish(last_blocks())
        for h in range(nh):
            y_ref[h * HEAD_DIM:(h + 1) * HEAD_DIM, :] = o[:, h * tq:(h + 1) * tq].astype(y_ref.dtype)

    return trip, finish


N_NSA_IN = 10
N_MOBA_IN = 5


def _mixers_body(*refs, n_sel_blocks, n_blocks):
    nsa_in = refs[:N_NSA_IN]
    moba_in = refs[N_NSA_IN:N_NSA_IN + N_MOBA_IN]
    y_nsa, y_moba = refs[N_NSA_IN + N_MOBA_IN:N_NSA_IN + N_MOBA_IN + 2]
    scratch = refs[N_NSA_IN + N_MOBA_IN + 2:]
    nsa_trip, nsa_finish = _nsa_body(*nsa_in, y_nsa, *scratch[:5], n_sel_blocks=n_sel_blocks)
    moba_trip, moba_finish = _moba_body(*moba_in, y_moba, *scratch[5:], n_blocks=n_blocks)
    i = pl.program_id(1)

    @pl.loop(0, (i + 2) // 2 - 1)
    def _(p):
        nsa_trip(p)
        moba_trip(p + 1)

    nsa_finish()
    moba_finish()


def _mixers_call(qt, gnt, kc, vct, ke, vts, kw, vtw, ovt, qmt, kmean, kem, vtm):
    b, _, seq = qt.shape
    g = NSA_KV_GROUPS
    nkb = seq // KEY_BLOCK
    ncp = kc.shape[1]
    tq = Q_TILE
    n_cols = NSA_HEADS * tq
    tile = lambda bi, i: (bi, 0, i)
    per_batch = lambda shape: pl.BlockSpec((None,) + shape, lambda bi, i: (bi,) + (0,) * len(shape),
                                           pipeline_mode=pl.Buffered(1))
    return pl.pallas_call(
        functools.partial(_mixers_body, n_sel_blocks=seq // SEL_BLOCK, n_blocks=nkb),
        grid=(b, seq // tq),
        in_specs=[pl.BlockSpec((None, NSA_WIDTH, tq), tile),
                  pl.BlockSpec((None, NSA_WIDTH, tq), lambda bi, i: (bi, 1, i)),
                  pl.BlockSpec((None, g, GATE_ROWS, tq), lambda bi, i: (bi, 0, 0, i)),
                  per_batch((ncp, LANES)),
                  per_batch((g, V_ROWS, ncp)),
                  per_batch((nkb, KEY_BLOCK, K_AUG)),
                  per_batch((g, nkb, V_ROWS, KEY_BLOCK)),
                  per_batch((nkb, KEY_BLOCK, LANES)),
                  per_batch((g, nkb, V_ROWS, KEY_BLOCK)),
                  pl.BlockSpec((HEAD_DIM, ncp), lambda bi, i: (0, 0)),
                  pl.BlockSpec((None, MOBA_WIDTH, tq), tile),
                  pl.BlockSpec((None, MOBA_WIDTH, tq), lambda bi, i: (bi, 1, i)),
                  per_batch((MOBA_HEADS, 16, HEAD_DIM)),
                  per_batch((nkb, KEY_BLOCK, (MOBA_HEADS // 2) * K_AUG)),
                  per_batch((MOBA_HEADS, nkb, V_ROWS, KEY_BLOCK))],
        out_specs=[pl.BlockSpec((None, NSA_WIDTH, tq), tile), pl.BlockSpec((None, MOBA_WIDTH, tq), tile)],
        out_shape=[jax.ShapeDtypeStruct((b, NSA_WIDTH, seq), BF16),
                   jax.ShapeDtypeStruct((b, MOBA_WIDTH, seq), BF16)],
        scratch_shapes=[pltpu.VMEM((2, KEY_BLOCK, n_cols), F32),
                        pltpu.VMEM((2, 1, n_cols), F32),
                        pltpu.VMEM((1, n_cols), F32),
                        pltpu.VMEM((V_ROWS, n_cols), F32),
                        pltpu.VMEM((HEAD_DIM, g * tq), F32),
                        pltpu.VMEM((2, KEY_BLOCK, n_cols), F32),
                        pltpu.VMEM((2, 1, n_cols), F32),
                        pltpu.VMEM((1, n_cols), F32),
                        pltpu.VMEM((V_ROWS, n_cols), F32)],
        compiler_params=pltpu.CompilerParams(
            dimension_semantics=("parallel", "arbitrary"), vmem_limit_bytes=VMEM_LIMIT),
        name="mixers",
    )(qt, qt, gnt, kc, vct, ke, vts, kw, vtw, ovt, qmt, qmt, kmean, kem, vtm)


def _merge_ffn_body(x_ref, yn_ref, ym_ref, gm_ref, wun_ref, wum_ref, wo_ref, g2_ref,
                    w1_ref, w2_ref, gf_ref, o_ref):
    tn = (((0,), (0,)), ((), ()))
    a = lax.dot_general(yn_ref[...], wun_ref[...], tn, preferred_element_type=F32)
    c = lax.dot_general(ym_ref[...], wum_ref[...], tn, preferred_element_type=F32)
    mixed = gm_ref[:, 0:D_MODEL] * a + gm_ref[:, D_MODEL:2 * D_MODEL] * c
    x1 = x_ref[...] + _dot(mixed.astype(BF16), wo_ref[...])
    ms = jnp.mean(x1 * x1, axis=-1, keepdims=True)
    h2 = (x1 * lax.rsqrt(ms + NORM_EPS) * g2_ref[...]).astype(BF16)
    x2 = x1
    for k in range(D_FF // FF_CHUNK):
        u = jnp.maximum(_dot(h2, w1_ref[:, k * FF_CHUNK:(k + 1) * FF_CHUNK]), 0.0)
        x2 = x2 + _dot((u * u).astype(BF16), w2_ref[k * FF_CHUNK:(k + 1) * FF_CHUNK, :])
    ms = jnp.mean(x2 * x2, axis=-1, keepdims=True)
    o_ref[...] = x2 * lax.rsqrt(ms + NORM_EPS) * gf_ref[...]


def _merge_ffn(x2, ynt, ymt, gm, wun, wum, wo, g2, w1, w2, gf):
    m = x2.shape[0]
    seq = ynt.shape[2]
    tm = 512
    nt = seq // tm
    row = lambda i: (i, 0)
    const = lambda i: (0, 0)
    feat = lambda i: (i // nt, 0, i % nt)
    resident = lambda shape: pl.BlockSpec(shape, const, pipeline_mode=pl.Buffered(1))
    return pl.pallas_call(
        _merge_ffn_body,
        grid=(m // tm,),
        in_specs=[pl.BlockSpec((tm, D_MODEL), row),
                  pl.BlockSpec((None, NSA_WIDTH, tm), feat),
                  pl.BlockSpec((None, MOBA_WIDTH, tm), feat),
                  pl.BlockSpec((tm, 2 * D_MODEL), row),
                  resident((NSA_WIDTH, D_MODEL)),
                  resident((MOBA_WIDTH, D_MODEL)),
                  resident((D_MODEL, D_MODEL)),
                  pl.BlockSpec((1, D_MODEL), const),
                  resident((D_MODEL, D_FF)),
                  resident((D_FF, D_MODEL)),
                  pl.BlockSpec((1, D_MODEL), const)],
        out_specs=pl.BlockSpec((tm, D_MODEL), row),
        out_shape=jax.ShapeDtypeStruct((m, D_MODEL), F32),
        compiler_params=pltpu.CompilerParams(dimension_semantics=("parallel",),
                                             vmem_limit_bytes=VMEM_LIMIT),
        name="merge_ffn",
    )(x2, ynt, ymt, gm, wun, wum, wo, g2, w1, w2, gf)


def _rope_tables(seq):
    half = ROT_DIM // 2
    inv = ROPE_THETA ** (-jnp.arange(0, ROT_DIM, 2, dtype=F32) / ROT_DIM)
    ang = jnp.arange(seq, dtype=F32)[:, None] * inv[None, :]
    cos, sin = jnp.cos(ang), jnp.sin(ang)
    pad = HEAD_DIM - ROT_DIM
    one_head = lambda a, b, fill: jnp.concatenate([a, b, jnp.full((seq, pad), fill, F32)], axis=1)
    zeros = jnp.zeros((seq, half), F32)
    two = lambda a: jnp.concatenate([a, a], axis=1)
    return (two(one_head(cos, cos, 1.0)),
            two(one_head(-sin, zeros, 0.0)),
            two(one_head(zeros, sin, 0.0)),
            cos.T, sin.T)


def _overlap_t(ncp, nsb):
    i = np.arange(ncp)[None, :]
    j = np.arange(HEAD_DIM)[:, None]
    start = i * CMP_STRIDE
    end = start + CMP_BLOCK - 1
    ov = (end >= j * SEL_BLOCK) & (start <= j * SEL_BLOCK + SEL_BLOCK - 1) & (i < ncp - 1) & (j < nsb)
    return jnp.asarray(ov.astype(np.float32), dtype=BF16)


def _block_diag(w):
    z = jnp.zeros_like(w)
    return jnp.concatenate([jnp.concatenate([w, z], axis=-1), jnp.concatenate([z, w], axis=-1)], axis=-2)


def _projection_weights(w):
    kv = lambda j: w[:, OFF_KV + j * KV_WIDTH:OFF_KV + (j + 1) * KV_WIDTH]
    mo = lambda j: w[:, OFF_M + j * MOBA_WIDTH:OFF_M + (j + 1) * MOBA_WIDTH]
    wa = jnp.concatenate([kv(0), kv(1), kv(2), kv(4), mo(1), w[:, OFF_GM:]], axis=1)
    gn = w[:, OFF_GN:OFF_M].T.reshape(NSA_KV_GROUPS, 3 * NSA_GROUP_SIZE, D_MODEL)
    gn = jnp.concatenate([gn, jnp.zeros((NSA_KV_GROUPS, GATE_ROWS - 3 * NSA_GROUP_SIZE, D_MODEL), w.dtype)], axis=1)
    wb = jnp.concatenate([w[:, :OFF_KV].T, mo(0).T, kv(3).T, kv(5).T, mo(2).T,
                          gn.reshape(NSA_KV_GROUPS * GATE_ROWS, D_MODEL)], axis=0)
    return wa.astype(BF16), wb.astype(BF16)


def _mixers(x, norm1_g, w_in, cmp_pe_k, cmp_pe_v, cmp_k_w1, cmp_k_w2, cmp_v_w1, cmp_v_w2):
    b, seq, _ = x.shape
    g, d = NSA_KV_GROUPS, HEAD_DIM
    nkb = seq // KEY_BLOCK
    ncp = seq // CMP_STRIDE
    nsb = seq // SEL_BLOCK
    layer = 0

    wa, wb = _projection_weights(w_in[layer])
    x2 = x.reshape(b * seq, D_MODEL)
    (qt, qmt, gnt, kcx, vcx, ke, kw, vts, vtw, kem, vtm, kmean8, ogm) = _in_proj(
        x2, norm1_g[layer][None, :], wa, wb, _rope_tables(seq), b, seq)

    per_group = lambda a: jnp.concatenate([a] * g, axis=-1)
    pe_rows = lambda pe: jnp.broadcast_to(per_group(pe)[:, None, :], (CMP_BLOCK, SUBLANES, LANES)).astype(BF16)
    w1_blocks = lambda w1: _block_diag(w1.reshape(CMP_BLOCK, d, d)).astype(BF16)
    kc, vct = _compress(
        kcx.reshape(b, seq, KV_WIDTH), vcx.reshape(b, seq, KV_WIDTH),
        pe_rows(cmp_pe_k[layer]), pe_rows(cmp_pe_v[layer]),
        w1_blocks(cmp_k_w1[layer]), w1_blocks(cmp_v_w1[layer]),
        _block_diag(cmp_k_w2[layer]).astype(BF16), _block_diag(cmp_v_w2[layer].T).astype(BF16))
    kmean = kmean8[:, ::SUBLANES, :].reshape(b, nkb, MOBA_HEADS, d)
    kmean = jnp.swapaxes(kmean, 1, 2)
    kmean = jnp.concatenate([kmean, jnp.zeros((b, MOBA_HEADS, 16 - nkb, d), F32)], axis=2)
    ynt, ymt = _mixers_call(qt, gnt, kc, vct, ke, vts, kw, vtw, _overlap_t(ncp, nsb),
                            qmt, kmean, kem, vtm)
    return x2, ynt, ymt, ogm


def kernel(x, norm1_g, w_in, cmp_pe_k, cmp_pe_v, cmp_k_w1, cmp_k_w2, cmp_v_w1, cmp_v_w2,
           w_up_nsa, w_up_moba, w_out, norm2_g, w_ff1, w_ff2, norm_f_g):
    b, seq, _ = x.shape
    layer = 0
    x2, ynt, ymt, ogm = _mixers(x, norm1_g, w_in, cmp_pe_k, cmp_pe_v, cmp_k_w1, cmp_k_w2, cmp_v_w1, cmp_v_w2)
    out = _merge_ffn(x2, ynt, ymt, ogm, w_up_nsa[layer].astype(BF16), w_up_moba[layer].astype(BF16),
                     w_out[layer].astype(BF16), norm2_g[layer][None, :],
                     w_ff1[layer].astype(BF16), w_ff2[layer].astype(BF16), norm_f_g[None, :])
    return out.reshape(b, seq, D_MODEL)
```
